```python
import math
import jax
import jax.numpy as jnp
from jax import lax
import numpy as np


D_MODEL = 4096
BATCH = 4
SEQ = 2048
DEPTH = 2

HEAD_DIM = 128
N_HEADS_A = 16
N_HEADS_B = 16
DIFF_HALF = HEAD_DIM // 2
QBLK = 128
N_BUCKETS = 32
MAX_DISTANCE = 128
A_IN = 3 * N_HEADS_A * HEAD_DIM
B_IN = 3 * N_HEADS_B * HEAD_DIM
EVEN_IN = A_IN + B_IN
EVEN_OUT = (N_HEADS_A + N_HEADS_B) * HEAD_DIM
DIFF_SCALE = DIFF_HALF ** -0.5
SB_SCALE = HEAD_DIM ** -0.5

SSD_HEAD_DIM = 64
SSD_D_INNER = D_MODEL
SSD_N_HEADS = SSD_D_INNER // SSD_HEAD_DIM
SSD_N_GROUPS = 8
SSD_D_STATE = 128
SSD_CONV = 4
SSD_CHUNK = 128
SSD_GN = SSD_N_GROUPS * SSD_D_STATE
SSD_CONV_DIM = SSD_D_INNER + 2 * SSD_GN
C_IN = SSD_D_INNER + SSD_CONV_DIM + SSD_N_HEADS
GMLP_WIDTH = D_MODEL // 2
GMLP_GROUPS = 8
GMLP_GROUP_DIM = GMLP_WIDTH // GMLP_GROUPS
GMLP_CHUNK = 128
D_IN = 2 * GMLP_WIDTH
ODD_IN = C_IN + D_IN
ODD_OUT = SSD_D_INNER + GMLP_WIDTH

N_EXPERTS = 32
N_EXPERT_GROUPS = 8
EXPERTS_PER_GROUP = N_EXPERTS // N_EXPERT_GROUPS
TOP_K = 2
D_FF = 768
MOE_BLK = 128

N_EVEN = (DEPTH + 1) // 2
N_ODD = DEPTH // 2
ALPHA = (2 * DEPTH) ** 0.25
BETA = (8 * DEPTH) ** -0.25
LN_EPS = 1e-5

kernel_name = 'hybrid_diffattn_stickbreak_ssd_gmlp_grouped_moe'


def layer_norm(x, g, b):
    xf = x.astype(jnp.float32)
    mu = jnp.mean(xf, axis=-1, keepdims=True)
    var = jnp.mean(jnp.square(xf - mu), axis=-1, keepdims=True)
    return ((xf - mu) * lax.rsqrt(var + LN_EPS) * g.astype(jnp.float32) + b.astype(jnp.float32)).astype(x.dtype)


def rms_norm(x, g):
    xf = x.astype(jnp.float32)
    ms = jnp.mean(jnp.square(xf), axis=-1, keepdims=True)
    return (xf * lax.rsqrt(ms + LN_EPS) * g.astype(jnp.float32)).astype(x.dtype)


def t5_bucket(rel):
    n = jnp.maximum(rel, 0)
    max_exact = N_BUCKETS // 2
    nf = jnp.maximum(n, 1).astype(jnp.float32)
    large = max_exact + (jnp.log(nf / max_exact) / math.log(MAX_DISTANCE / max_exact)
                         * (N_BUCKETS - max_exact)).astype(jnp.int32)
    large = jnp.minimum(large, N_BUCKETS - 1)
    return jnp.where(n < max_exact, n, large)


def even_mixer(x, w_in, w_out, diff_lambda, subln_g, rel_bias, lambda_init):
    bsz, s, _ = x.shape
    nblk = s // QBLK
    h = x @ w_in
    qa, ka, va = jnp.split(h[..., :A_IN], 3, axis=-1)
    qb, kb, vb = jnp.split(h[..., A_IN:], 3, axis=-1)
    qa = qa.reshape(bsz, nblk, QBLK, N_HEADS_A, 2, DIFF_HALF).transpose(1, 0, 3, 4, 2, 5)
    ka = ka.reshape(bsz, s, N_HEADS_A, 2, DIFF_HALF).transpose(0, 2, 3, 1, 4)
    va = va.reshape(bsz, s, N_HEADS_A, HEAD_DIM).transpose(0, 2, 1, 3)
    qb = qb.reshape(bsz, nblk, QBLK, N_HEADS_B, HEAD_DIM).transpose(1, 0, 3, 2, 4)
    kb = kb.reshape(bsz, s, N_HEADS_B, HEAD_DIM).transpose(0, 2, 1, 3)
    vb = vb.reshape(bsz, s, N_HEADS_B, HEAD_DIM).transpose(0, 2, 1, 3)
    lq1, lk1, lq2, lk2 = diff_lambda[0], diff_lambda[1], diff_lambda[2], diff_lambda[3]
    lam = (jnp.exp(jnp.sum(lq1 * lk1).astype(jnp.float32))
           - jnp.exp(jnp.sum(lq2 * lk2).astype(jnp.float32)) + lambda_init)
    k_pos = jnp.arange(s)

    def block(args):
        i, q_a, q_b = args
        q_pos = i * QBLK + jnp.arange(QBLK)
        rel = q_pos[:, None] - k_pos[None, :]
        bias = jnp.moveaxis(rel_bias[t5_bucket(rel)], -1, 0).astype(jnp.float32)
        la = jnp.einsum('bhmqd,bhmkd->bhmqk', q_a, ka).astype(jnp.float32) * DIFF_SCALE + bias[:, None]
        la = jnp.where(rel >= 0, la, -jnp.inf)
        pa = jax.nn.softmax(la, axis=-1)
        wa = pa[:, :, 0] - lam * pa[:, :, 1]
        oa = jnp.einsum('bhqk,bhkd->bhqd', wa.astype(va.dtype), va)
        z = jnp.einsum('bhqd,bhkd->bhqk', q_b, kb).astype(jnp.float32) * SB_SCALE
        strict = rel > 0
        log_beta = jax.nn.log_sigmoid(z)
        log_1mb = jnp.where(strict, jax.nn.log_sigmoid(-z), 0.0)
        tail = lax.cumsum(log_1mb, axis=log_1mb.ndim - 1, reverse=True) - log_1mb
        wb = jnp.where(strict, jnp.exp(log_beta + tail), 0.0)
        ob = jnp.einsum('bhqk,bhkd->bhqd', wb.astype(vb.dtype), vb)
        return oa, ob

    oa, ob = lax.map(block, (jnp.arange(nblk), qa, qb))
    oa = oa.transpose(1, 0, 3, 2, 4).reshape(bsz, s, N_HEADS_A, HEAD_DIM)
    oa = (rms_norm(oa, subln_g) * (1.0 - lambda_init)).reshape(bsz, s, N_HEADS_A * HEAD_DIM)
    ob = ob.transpose(1, 0, 3, 2, 4).reshape(bsz, s, N_HEADS_B * HEAD_DIM)
    return jnp.concatenate([oa.astype(x.dtype), ob.astype(x.dtype)], axis=-1) @ w_out


def ssd_chunked(xs, dt, a, bm, cm):
    bsz, s, h, p = xs.shape
    g, n = bm.shape[-2:]
    r = h // g
    nc = s // SSD_CHUNK
    L = SSD_CHUNK
    xdt = (xs * dt[..., None]).reshape(bsz, nc, L, g, r, p)
    adt = jnp.moveaxis((dt * a).reshape(bsz, nc, L, g, r), 2, -1)
    a_cs = jnp.cumsum(adt, axis=-1)
    bc = bm.reshape(bsz, nc, L, g, n)
    cc = cm.reshape(bsz, nc, L, g, n)
    causal = jnp.tril(jnp.ones((L, L), dtype=bool))
    decay = jnp.exp(jnp.where(causal, a_cs[..., :, None] - a_cs[..., None, :], -jnp.inf))
    cb = jnp.einsum('bclgn,bcsgn->bcgls', cc, bc)
    y_diag = jnp.einsum('bcgrls,bcsgrp->bclgrp', cb[:, :, :, None] * decay, xdt)
    decay_states = jnp.exp(a_cs[..., -1:] - a_cs)
    states = jnp.einsum('bclgn,bcgrl,bclgrp->bcgrpn', bc, decay_states, xdt)
    chunk_decay = jnp.exp(a_cs[..., -1])

    def step(carry, inp):
        st, dec = inp
        return carry * dec[..., None, None] + st, carry

    init = jnp.zeros((bsz, g, r, p, n), states.dtype)
    _, prev = lax.scan(step, init, (jnp.moveaxis(states, 1, 0), jnp.moveaxis(chunk_decay, 1, 0)))
    prev = jnp.moveaxis(prev, 0, 1)
    y_off = jnp.einsum('bclgn,bcgrpn,bcgrl->bclgrp', cc, prev, jnp.exp(a_cs))
    return (y_diag + y_off).reshape(bsz, s, h, p)


def odd_mixer(x, w_in, w_out, conv_w, conv_b, dt_bias, a_log, d_skip, norm_g, ln_g, ln_b, ws, bs):
    bsz, s, _ = x.shape
    h = x @ w_in
    z = h[..., :SSD_D_INNER]
    xbc = h[..., SSD_D_INNER:SSD_D_INNER + SSD_CONV_DIM]
    dt_raw = h[..., SSD_D_INNER + SSD_CONV_DIM:C_IN]
    uv = h[..., C_IN:]
    xbc = lax.conv_general_dilated(xbc, conv_w[:, None, :], (1,), [(SSD_CONV - 1, 0)],
                                   dimension_numbers=('NWC', 'WIO', 'NWC'),
                                   feature_group_count=SSD_CONV_DIM)
    xbc = jax.nn.silu(xbc + conv_b)
    xs = xbc[..., :SSD_D_INNER].reshape(bsz, s, SSD_N_HEADS, SSD_HEAD_DIM)
    bm = xbc[..., SSD_D_INNER:SSD_D_INNER + SSD_GN].reshape(bsz, s, SSD_N_GROUPS, SSD_D_STATE)
    cm = xbc[..., SSD_D_INNER + SSD_GN:].reshape(bsz, s, SSD_N_GROUPS, SSD_D_STATE)
    dt = jax.nn.softplus((dt_raw + dt_bias).astype(jnp.float32))
    a = -jnp.exp(a_log.astype(jnp.float32))
    y = ssd_chunked(xs, dt, a, bm, cm) + d_skip[:, None] * xs
    y = y.reshape(bsz, s, SSD_D_INNER) * jax.nn.silu(z)
    y = rms_norm(y.reshape(bsz, s, SSD_N_GROUPS, -1), norm_g.reshape(SSD_N_GROUPS, -1)).reshape(bsz, s, SSD_D_INNER)
    uv = jax.nn.gelu(uv)
    u, v = uv[..., :GMLP_WIDTH], uv[..., GMLP_WIDTH:]
    v = layer_norm(v.reshape(bsz, s, GMLP_GROUPS, GMLP_GROUP_DIM),
                   ln_g.reshape(GMLP_GROUPS, GMLP_GROUP_DIM), ln_b.reshape(GMLP_GROUPS, GMLP_GROUP_DIM))
    nc = s // GMLP_CHUNK
    ws_causal = jnp.where(jnp.tril(jnp.ones((GMLP_CHUNK, GMLP_CHUNK), dtype=bool)), ws, 0.0)
    sv = jnp.einsum('gts,bcsgd->bctgd', ws_causal, v.reshape(bsz, nc, GMLP_CHUNK, GMLP_GROUPS, GMLP_GROUP_DIM))
    sv = sv + bs.T[:, :, None]
    od = u * sv.reshape(bsz, s, GMLP_WIDTH)
    return jnp.concatenate([y.astype(x.dtype), od.astype(x.dtype)], axis=-1) @ w_out


def moe(x, layer, router_w, router_bias, w_gate, w_up, w_down):
    bsz, s, d = x.shape
    t = bsz * s
    xt = x.reshape(t, d)
    scores = jax.nn.sigmoid((xt @ router_w).astype(jnp.float32))
    biased = (scores + router_bias).reshape(t, N_EXPERT_GROUPS, EXPERTS_PER_GROUP)
    grp_score = jnp.sum(lax.top_k(biased, 2)[0], axis=-1)
    g_sel = jnp.argmax(grp_score, axis=-1)
    cand = jnp.take_along_axis(biased, g_sel[:, None, None], axis=1)[:, 0]
    _, local = lax.top_k(cand, TOP_K)
    eidx = g_sel[:, None] * EXPERTS_PER_GROUP + local
    gate = jnp.take_along_axis(scores, eidx, axis=1)
    gate = gate / jnp.sum(gate, axis=-1, keepdims=True)
    tk = t * TOP_K
    flat_e = eidx.reshape(tk)
    order = jnp.argsort(flat_e)
    sorted_e = flat_e[order]
    counts = jnp.zeros((N_EXPERTS,), jnp.int32).at[flat_e].add(1)
    padded = (counts + MOE_BLK - 1) // MOE_BLK * MOE_BLK
    pad_end = jnp.cumsum(padded)
    pad_start = pad_end - padded
    start = jnp.cumsum(counts) - counts
    dest_sorted = pad_start[sorted_e] + jnp.arange(tk) - start[sorted_e]
    dest = jnp.zeros((tk,), jnp.int32).at[order].set(dest_sorted.astype(jnp.int32))
    n_blocks = (tk + MOE_BLK - 1) // MOE_BLK + N_EXPERTS
    n_rows = n_blocks * MOE_BLK
    buf = jnp.zeros((n_rows, d), x.dtype).at[dest].set(jnp.repeat(xt, TOP_K, axis=0))
    blk_start = jnp.arange(n_blocks) * MOE_BLK
    blk_e = jnp.minimum(jnp.sum(blk_start[:, None] >= pad_end[None, :], axis=1), N_EXPERTS - 1)

    def expert_block(args):
        xb, e = args
        hid = jax.nn.silu(xb @ w_gate[layer, e]) * (xb @ w_up[layer, e])
        return hid @ w_down[layer, e]

    out_rows = lax.map(expert_block, (buf.reshape(n_blocks, MOE_BLK, d), blk_e)).reshape(n_rows, d)
    y_slots = out_rows[dest].reshape(t, TOP_K, d)
    y = jnp.einsum('tk,tkd->td', gate, y_slots.astype(jnp.float32))
    return y.astype(x.dtype).reshape(bsz, s, d)


def setup_inputs(seed: int = 0) -> dict:
    key = jax.random.key(seed)
    ks = list(jax.random.split(key, 32))
    f32 = jnp.float32

    def nrm(shape, scale):
        return jax.random.normal(ks.pop(), shape, f32) * scale

    dt0 = jnp.exp(jax.random.uniform(ks.pop(), (N_ODD, SSD_N_HEADS), f32, math.log(1e-3), math.log(1e-1)))
    a0 = jax.random.uniform(ks.pop(), (N_ODD, SSD_N_HEADS), f32, 1.0, 16.0)
    return {
        'x': nrm((BATCH, SEQ, D_MODEL), 1.0),
        'rel_bias': nrm((N_BUCKETS, N_HEADS_A), 0.5),
        'even_w_in': nrm((N_EVEN, D_MODEL, EVEN_IN), D_MODEL ** -0.5),
        'even_w_out': nrm((N_EVEN, EVEN_OUT, D_MODEL), EVEN_OUT ** -0.5 * BETA),
        'diff_lambda': nrm((N_EVEN, 4, DIFF_HALF), 0.1),
        'diff_subln_g': 1.0 + nrm((N_EVEN, HEAD_DIM), 0.02),
        'odd_w_in': nrm((N_ODD, D_MODEL, ODD_IN), D_MODEL ** -0.5),
        'odd_w_out': nrm((N_ODD, ODD_OUT, D_MODEL), ODD_OUT ** -0.5 * BETA),
        'ssd_conv_w': nrm((N_ODD, SSD_CONV, SSD_CONV_DIM), SSD_CONV ** -0.5),
        'ssd_conv_b': nrm((N_ODD, SSD_CONV_DIM), 0.02),
        'ssd_dt_bias': dt0 + jnp.log(-jnp.expm1(-dt0)),
        'ssd_a_log': jnp.log(a0),
        'ssd_d': 1.0 + nrm((N_ODD, SSD_N_HEADS), 0.02),
        'ssd_norm_g': 1.0 + nrm((N_ODD, SSD_D_INNER), 0.02),
        'gmlp_ln_g': 1.0 + nrm((N_ODD, GMLP_WIDTH), 0.02),
        'gmlp_ln_b': nrm((N_ODD, GMLP_WIDTH), 0.02),
        'gmlp_ws': nrm((N_ODD, GMLP_GROUPS, GMLP_CHUNK, GMLP_CHUNK), GMLP_CHUNK ** -0.5),
        'gmlp_bs': 1.0 + nrm((N_ODD, GMLP_GROUPS, GMLP_CHUNK), 0.1),
        'router_w': nrm((D_MODEL, N_EXPERTS), D_MODEL ** -0.5),
        'router_bias': nrm((N_EXPERTS,), 0.01),
        'moe_w_gate': nrm((DEPTH, N_EXPERTS, D_MODEL, D_FF), D_MODEL ** -0.5),
        'moe_w_up': nrm((DEPTH, N_EXPERTS, D_MODEL, D_FF), D_MODEL ** -0.5),
        'moe_w_down': nrm((DEPTH, N_EXPERTS, D_FF, D_MODEL), D_FF ** -0.5 * BETA),
        'ln_mix_g': 1.0 + nrm((DEPTH, D_MODEL), 0.02),
        'ln_mix_b': nrm((DEPTH, D_MODEL), 0.02),
        'ln_ffn_g': 1.0 + nrm((DEPTH, D_MODEL), 0.02),
        'ln_ffn_b': nrm((DEPTH, D_MODEL), 0.02),
    }


def reference(x, rel_bias, even_w_in, even_w_out, diff_lambda, diff_subln_g, odd_w_in, odd_w_out,
              ssd_conv_w, ssd_conv_b, ssd_dt_bias, ssd_a_log, ssd_d, ssd_norm_g, gmlp_ln_g, gmlp_ln_b,
              gmlp_ws, gmlp_bs, router_w, router_bias, moe_w_gate, moe_w_up, moe_w_down,
              ln_mix_g, ln_mix_b, ln_ffn_g, ln_ffn_b):
    for layer in range(DEPTH):
        if layer % 2 == 0:
            i = layer // 2
            lambda_init = 0.8 - 0.6 * math.exp(-0.3 * layer)
            mix = even_mixer(x, even_w_in[i], even_w_out[i], diff_lambda[i], diff_subln_g[i],
                             rel_bias, lambda_init)
        else:
            i = layer // 2
            mix = odd_mixer(x, odd_w_in[i], odd_w_out[i], ssd_conv_w[i], ssd_conv_b[i], ssd_dt_bias[i],
                            ssd_a_log[i], ssd_d[i], ssd_norm_g[i], gmlp_ln_g[i], gmlp_ln_b[i],
                            gmlp_ws[i], gmlp_bs[i])
        x = layer_norm(ALPHA * x + mix, ln_mix_g[layer], ln_mix_b[layer])
        ffn = moe(x, layer, router_w, router_bias, moe_w_gate, moe_w_up, moe_w_down)
        x = layer_norm(ALPHA * x + ffn, ln_ffn_g[layer], ln_ffn_b[layer])
    return x
```

```python
import functools
import math

import jax
import jax.numpy as jnp
from jax import lax
from jax.experimental import pallas as pl
from jax.experimental.pallas import tpu as pltpu

F32 = jnp.float32
BF16 = jnp.bfloat16
I32 = jnp.int32

D_MODEL = 4096
DEPTH = 2
HEAD_DIM = 128
N_HEADS_A = 16
N_HEADS_B = 16
DIFF_HALF = HEAD_DIM // 2
N_BUCKETS = 32
MAX_DISTANCE = 128
A_IN = 3 * N_HEADS_A * HEAD_DIM
EVEN_IN = A_IN + 3 * N_HEADS_B * HEAD_DIM
DIFF_SCALE = DIFF_HALF ** -0.5
SB_SCALE = HEAD_DIM ** -0.5

SSD_HEAD_DIM = 64
SSD_D_INNER = D_MODEL
SSD_N_HEADS = SSD_D_INNER // SSD_HEAD_DIM
SSD_N_GROUPS = 8
SSD_HEADS_PER_GROUP = SSD_N_HEADS // SSD_N_GROUPS
SSD_D_STATE = 128
SSD_CONV = 4
SSD_CHUNK = 128
SSD_GN = SSD_N_GROUPS * SSD_D_STATE
SSD_CONV_DIM = SSD_D_INNER + 2 * SSD_GN
SSD_GROUP_DIM = SSD_D_INNER // SSD_N_GROUPS
C_IN = SSD_D_INNER + SSD_CONV_DIM + SSD_N_HEADS
GMLP_WIDTH = D_MODEL // 2
GMLP_GROUPS = 8
GMLP_GROUP_DIM = GMLP_WIDTH // GMLP_GROUPS
GMLP_CHUNK = 128

N_EXPERTS = 32
N_EXPERT_GROUPS = 8
EXPERTS_PER_GROUP = N_EXPERTS // N_EXPERT_GROUPS
TOP_K = 2
D_FF = 768

ALPHA = (2 * DEPTH) ** 0.25
LN_EPS = 1e-5

LANES = 128
SUBLANES = 8
VMEM_LIMIT_BYTES = 58 * 1024 * 1024

ATTN_TILE = 256
MOE_ROWS = 512
MOE_K_STEPS = 4
MOE_O_STEPS = 2
ROUTER_ROWS = 512
LN_ROWS = 256


def _params(*sem):
    return pltpu.CompilerParams(dimension_semantics=sem, vmem_limit_bytes=VMEM_LIMIT_BYTES)


def _proj_kernel(*refs, n_in):
    x_refs = refs[:n_in]
    w_refs = refs[n_in:2 * n_in]
    o_ref = refs[2 * n_in]
    wbf_refs = refs[2 * n_in + 1:]

    @pl.when(pl.program_id(1) == 0)
    def _():
        for w_ref, wbf_ref in zip(w_refs, wbf_refs):
            wbf_ref[...] = w_ref[...].astype(BF16)

    acc = jnp.dot(x_refs[0][...], wbf_refs[0][...], preferred_element_type=F32)
    for x_ref, wbf_ref in zip(x_refs[1:], wbf_refs[1:]):
        acc = acc + jnp.dot(x_ref[...], wbf_ref[...], preferred_element_type=F32)
    o_ref[...] = acc.astype(o_ref.dtype)


def _project(xs, w, row_blocks, col_block0, n_cols, out_dtype, tm, tn):
    m = xs[0].shape[0]
    n_in = len(xs)
    in_specs = [pl.BlockSpec((tm, x.shape[1]), lambda j, i: (i, 0)) for x in xs]
    for x, rb in zip(xs, row_blocks):
        in_specs.append(pl.BlockSpec((x.shape[1], tn), lambda j, i, rb=rb: (rb, j + col_block0)))
    return pl.pallas_call(
        functools.partial(_proj_kernel, n_in=n_in),
        grid=(n_cols // tn, m // tm),
        in_specs=in_specs,
        out_specs=pl.BlockSpec((tm, tn), lambda j, i: (i, j)),
        out_shape=jax.ShapeDtypeStruct((m, n_cols), out_dtype),
        scratch_shapes=[pltpu.VMEM((x.shape[1], tn), BF16) for x in xs],
        compiler_params=_params("arbitrary", "arbitrary"),
        name="proj",
    )(*xs, *([w] * n_in))


def _layer_norm_rows(y, g, b):
    mu = jnp.mean(y, axis=-1, keepdims=True)
    yc = y - mu
    var = jnp.mean(yc * yc, axis=-1, keepdims=True)
    return yc * lax.rsqrt(var + LN_EPS) * g + b


def _res_ln_kernel(x_ref, mix_ref, g_ref, b_ref, o_ref, obf_ref):
    y = _layer_norm_rows(ALPHA * x_ref[...] + mix_ref[...], g_ref[...], b_ref[...])
    o_ref[...] = y
    obf_ref[...] = y.astype(BF16)


def _residual_layer_norm(x, mix, g, b):
    m, d = x.shape
    tm = LN_ROWS
    row = pl.BlockSpec((tm, d), lambda i: (i, 0))
    vec = pl.BlockSpec((1, d), lambda i: (0, 0))
    return pl.pallas_call(
        _res_ln_kernel,
        grid=(m // tm,),
        in_specs=[row, row, vec, vec],
        out_specs=[row, row],
        out_shape=[jax.ShapeDtypeStruct((m, d), F32), jax.ShapeDtypeStruct((m, d), BF16)],
        compiler_params=_params("arbitrary"),
        name="res_ln",
    )(x, mix, g.reshape(1, d), b.reshape(1, d))


def _bias_tile_kernel(rb_ref, o_ref):
    head = pl.program_id(0)
    t = ATTN_TILE
    row = lax.broadcasted_iota(I32, (t, t), 0)
    col = lax.broadcasted_iota(I32, (t, t), 1)
    max_exact = N_BUCKETS // 2
    for d in range(2):
        n = jnp.maximum(d * t + row - col, 0)
        nf = jnp.maximum(n, 1).astype(F32)
        large = max_exact + (jnp.log(nf / max_exact) / math.log(MAX_DISTANCE / max_exact)
                             * (N_BUCKETS - max_exact)).astype(I32)
        large = jnp.minimum(large, N_BUCKETS - 1)
        bucket = jnp.where(n < max_exact, n, large)
        acc = jnp.zeros((t, t), F32)
        for bkt in range(N_BUCKETS):
            acc = jnp.where(bucket == bkt, rb_ref[bkt, head], acc)
        o_ref[d] = acc


def _bias_tiles(rel_bias):
    t = ATTN_TILE
    return pl.pallas_call(
        _bias_tile_kernel,
        grid=(N_HEADS_A,),
        in_specs=[pl.BlockSpec(memory_space=pltpu.SMEM)],
        out_specs=pl.BlockSpec((None, 2, t, t), lambda h: (h, 0, 0, 0)),
        out_shape=jax.ShapeDtypeStruct((N_HEADS_A, 2, t, t), F32),
        compiler_params=_params("arbitrary"),
        name="bias_tiles",
    )(rel_bias)


def _dot_nt(a, b):
    return lax.dot_general(a, b, (((1,), (1,)), ((), ())), preferred_element_type=F32)


def _diff_attention(dl_ref, g_ref, q_ref, k_ref, v_ref, bias_ref, o_ref, m_sc, l_sc, acc_sc, lambda_init):
    t = ATTN_TILE
    qi = pl.program_id(2)
    lane = lax.broadcasted_iota(I32, (t, HEAD_DIM), 1)
    q = q_ref[...] * DIFF_SCALE
    zero = jnp.zeros_like(q)
    q_maps = (jnp.where(lane < DIFF_HALF, q, zero), jnp.where(lane >= DIFF_HALF, q, zero))

    def block(kv_start, bias, mask, first):
        kb = k_ref[pl.ds(kv_start, t), :]
        vb = v_ref[pl.ds(kv_start, t), :]
        for mp in range(2):
            s = _dot_nt(q_maps[mp], kb) + bias
            if mask is not None:
                s = jnp.where(mask, s, -jnp.inf)
            row_max = jnp.max(s, axis=-1, keepdims=True)
            if first:
                m_new = row_max
                p = jnp.exp(s - m_new)
                l_sc[mp] = jnp.sum(p, axis=-1, keepdims=True)
                acc_sc[mp] = jnp.dot(p.astype(BF16), vb, preferred_element_type=F32)
            else:
                m_old = m_sc[mp]
                m_new = jnp.maximum(m_old, row_max)
                alpha = jnp.exp(m_old - m_new)
                p = jnp.exp(s - m_new)
                l_sc[mp] = alpha * l_sc[mp] + jnp.sum(p, axis=-1, keepdims=True)
                acc_sc[mp] = alpha * acc_sc[mp] + jnp.dot(p.astype(BF16), vb, preferred_element_type=F32)
            m_sc[mp] = m_new

    row = lax.broadcasted_iota(I32, (t, t), 0)
    col = lax.broadcasted_iota(I32, (t, t), 1)
    block(pl.multiple_of(qi * t, t), bias_ref[0], row >= col, True)

    @pl.when(qi >= 1)
    def _():
        block(pl.multiple_of((qi - 1) * t, t), bias_ref[1], None, False)

    far_bias = bias_ref[1, t - 1:t, 0:1]

    def far_body(j, carry):
        block(pl.multiple_of(j * t, t), far_bias, None, False)
        return carry

    lax.fori_loop(0, qi - 1, far_body, 0)

    dl = dl_ref[...]
    lam = (jnp.exp(jnp.sum(dl[0:1] * dl[1:2], axis=-1, keepdims=True))
           - jnp.exp(jnp.sum(dl[2:3] * dl[3:4], axis=-1, keepdims=True)) + lambda_init)
    oa = acc_sc[0] / l_sc[0] - lam * (acc_sc[1] / l_sc[1])
    ms = jnp.mean(oa * oa, axis=-1, keepdims=True)
    oa = oa * lax.rsqrt(ms + LN_EPS) * g_ref[...] * (1.0 - lambda_init)
    o_ref[...] = oa.astype(o_ref.dtype)


def _log_sigmoid(z):
    return jnp.minimum(z, 0.0) - jnp.log(1.0 + jnp.exp(-jnp.abs(z)))


def _split_bf16(x):
    hi = x.astype(BF16)
    lo = (x - hi.astype(F32)).astype(BF16)
    return hi, lo


def _stick_breaking_attention(q_ref, k_ref, v_ref, o_ref, c_sc, acc_sc):
    t = ATTN_TILE
    qi = pl.program_id(2)
    q = q_ref[...]
    row = lax.broadcasted_iota(I32, (t, t), 0)
    col = lax.broadcasted_iota(I32, (t, t), 1)
    after = (row > col).astype(BF16)

    def block(kv_start, strict, first):
        kb = k_ref[pl.ds(kv_start, t), :]
        vb = v_ref[pl.ds(kv_start, t), :]
        z = _dot_nt(q, kb) * SB_SCALE
        log_beta = _log_sigmoid(z)
        log_1mb = log_beta - z
        if strict is not None:
            log_1mb = jnp.where(strict, log_1mb, 0.0)
        hi, lo = _split_bf16(log_1mb)
        tail = (jnp.dot(hi, after, preferred_element_type=F32)
                + jnp.dot(lo, after, preferred_element_type=F32))
        if not first:
            tail = tail + c_sc[...]
        w = jnp.exp(log_beta + tail)
        if strict is not None:
            w = jnp.where(strict, w, 0.0)
        pv = jnp.dot(w.astype(BF16), vb, preferred_element_type=F32)
        row_sum = jnp.sum(log_1mb, axis=-1, keepdims=True)
        if first:
            acc_sc[0] = pv
            c_sc[...] = row_sum
        else:
            acc_sc[0] = acc_sc[0] + pv
            c_sc[...] = c_sc[...] + row_sum

    block(pl.multiple_of(qi * t, t), row > col, True)

    def body(step, carry):
        block(pl.multiple_of((qi - 1 - step) * t, t), None, False)
        return carry

    lax.fori_loop(0, qi, body, 0)
    o_ref[...] = acc_sc[0].astype(o_ref.dtype)


def _attn_kernel(dl_ref, g_ref, q_ref, k_ref, v_ref, bias_ref, o_ref, m_sc, l_sc, c_sc, acc_sc, *, lambda_init):
    head = pl.program_id(1)

    @pl.when(head < N_HEADS_A)
    def _():
        _diff_attention(dl_ref, g_ref, q_ref, k_ref, v_ref, bias_ref, o_ref, m_sc, l_sc, acc_sc, lambda_init)

    @pl.when(head >= N_HEADS_A)
    def _():
        _stick_breaking_attention(q_ref, k_ref, v_ref, o_ref, c_sc, acc_sc)


def _even_attention(h, bias_tiles, diff_lambda, subln_g, lambda_init):
    bsz, s, _ = h.shape
    t = ATTN_TILE
    n_heads = N_HEADS_A + N_HEADS_B

    def q_col(hd):
        return hd + jnp.where(hd >= N_HEADS_A, 2 * N_HEADS_A, 0)

    return pl.pallas_call(
        functools.partial(_attn_kernel, lambda_init=lambda_init),
        grid=(bsz, n_heads, s // t),
        in_specs=[
            pl.BlockSpec((4, DIFF_HALF), lambda b, hd, qi: (0, 0)),
            pl.BlockSpec((1, HEAD_DIM), lambda b, hd, qi: (0, 0)),
            pl.BlockSpec((None, t, HEAD_DIM), lambda b, hd, qi: (b, qi, q_col(hd))),
            pl.BlockSpec((None, s, HEAD_DIM), lambda b, hd, qi: (b, 0, q_col(hd) + N_HEADS_A)),
            pl.BlockSpec((None, s, HEAD_DIM), lambda b, hd, qi: (b, 0, q_col(hd) + 2 * N_HEADS_A)),
            pl.BlockSpec((None, 2, t, t), lambda b, hd, qi: (jnp.minimum(hd, N_HEADS_A - 1), 0, 0, 0)),
        ],
        out_specs=pl.BlockSpec((None, t, HEAD_DIM), lambda b, hd, qi: (b, qi, hd)),
        out_shape=jax.ShapeDtypeStruct((bsz, s, n_heads * HEAD_DIM), BF16),
        scratch_shapes=[
            pltpu.VMEM((2, t, 1), F32),
            pltpu.VMEM((2, t, 1), F32),
            pltpu.VMEM((t, 1), F32),
            pltpu.VMEM((2, t, HEAD_DIM), F32),
        ],
        compiler_params=_params("arbitrary", "arbitrary", "arbitrary"),
        name="even_attention",
    )(diff_lambda, subln_g.reshape(1, HEAD_DIM), h, h, h, bias_tiles)


def _silu(x):
    return x * jax.nn.sigmoid(x)


def _softplus(x):
    return jnp.maximum(x, 0.0) + jnp.log(1.0 + jnp.exp(-jnp.abs(x)))


def _ssd_kernel(z_ref, x_ref, b_ref, c_ref, dt_ref, cw_ref, cb_ref, dtb_ref, alog_ref, dskip_ref, ng_ref,
                o_ref, pad_sc, state_sc, acst_sc, dtt_sc):
    grp = pl.program_id(1)
    chunk = pl.program_id(2)
    L = SSD_CHUNK
    P2 = 2 * SSD_HEAD_DIM
    n_conv = SSD_GROUP_DIM + 2 * SSD_D_STATE
    halo = SUBLANES

    @pl.when(chunk == 0)
    def _():
        pad_sc[0:halo, :] = jnp.zeros((halo, n_conv), F32)
        state_sc[...] = jnp.zeros_like(state_sc)

    pad_sc[halo:halo + L, 0:SSD_GROUP_DIM] = x_ref[...]
    pad_sc[halo:halo + L, SSD_GROUP_DIM:SSD_GROUP_DIM + SSD_D_STATE] = b_ref[...]
    pad_sc[halo:halo + L, SSD_GROUP_DIM + SSD_D_STATE:n_conv] = c_ref[...]
    conv = cb_ref[...]
    for j in range(SSD_CONV):
        start = halo - (SSD_CONV - 1) + j
        conv = conv + cw_ref[j:j + 1, :] * pad_sc[start:start + L, :]
    pad_sc[0:halo, :] = pad_sc[L:L + halo, :]
    xbc = _silu(conv)
    xg = xbc[:, 0:SSD_GROUP_DIM]
    bg = xbc[:, SSD_GROUP_DIM:SSD_GROUP_DIM + SSD_D_STATE]
    cg = xbc[:, SSD_GROUP_DIM + SSD_D_STATE:n_conv]
    xg_bf = xg.astype(BF16)

    dt = _softplus(dt_ref[...] + dtb_ref[...])
    adt = dt * (-jnp.exp(alog_ref[...]))
    row = lax.broadcasted_iota(I32, (L, L), 0)
    col = lax.broadcasted_iota(I32, (L, L), 1)
    causal = row >= col
    acs = jnp.dot(causal.astype(F32), adt, preferred_element_type=F32, precision=lax.Precision.HIGHEST)
    acst_sc[...] = acs.T
    dtt_sc[...] = dt.T

    cb = _dot_nt(cg.astype(BF16), bg.astype(BF16))
    bgt = bg.T
    cg_bf = cg.astype(BF16)
    prev = state_sc[...]
    y_off = jnp.dot(cg_bf, prev.astype(BF16), preferred_element_type=F32)

    lane = lax.broadcasted_iota(I32, (L, P2), 1)
    first_head = lane < SSD_HEAD_DIM
    lane_row = lax.broadcasted_iota(I32, (1, P2), 1) < SSD_HEAD_DIM
    y_pairs = []
    for pair in range(SSD_HEADS_PER_GROUP // 2):
        x_pair = xg_bf[:, pair * P2:(pair + 1) * P2]
        y_diag, st, e_acs, decay = [], [], [], []
        for sub in range(2):
            head = grp * SSD_HEADS_PER_GROUP + 2 * pair + sub
            acs_row = acst_sc[pl.ds(head, 1), :]
            dt_row = dtt_sc[pl.ds(head, 1), :]
            row_b = jnp.broadcast_to(acs_row, (L, L))
            col_b = row_b.T
            decay_m = jnp.exp(jnp.where(causal, col_b - row_b, -jnp.inf))
            m = (cb * decay_m * dt_row).astype(BF16)
            y_diag.append(jnp.dot(m, x_pair, preferred_element_type=F32))
            acs_last = col_b[L - 1:L, :]
            w_row = jnp.exp(acs_last - acs_row) * dt_row
            st.append(jnp.dot((bgt * w_row).astype(BF16), x_pair, preferred_element_type=F32))
            e_acs.append(jnp.exp(col_b))
            decay.append(jnp.exp(acs_last))
        cols = slice(pair * P2, (pair + 1) * P2)
        y_pair = (jnp.where(first_head, y_diag[0], y_diag[1])
                  + jnp.where(first_head, e_acs[0], e_acs[1]) * y_off[:, cols])
        state_sc[:, cols] = (prev[:, cols] * jnp.where(lane_row, decay[0], decay[1])
                             + jnp.where(first_head, st[0], st[1]))
        y_pair = y_pair + dskip_ref[:, cols] * xg[:, cols]
        y_pairs.append(y_pair * _silu(z_ref[:, cols]))

    y = jnp.concatenate(y_pairs, axis=-1)
    ms = jnp.mean(y * y, axis=-1, keepdims=True)
    o_ref[...] = (y * lax.rsqrt(ms + LN_EPS) * ng_ref[...]).astype(o_ref.dtype)


def _ssd(h_zx, h_dt, bsz, conv_w, conv_b, dt_bias, a_log, d_skip, norm_g):
    t_tokens = h_zx.shape[0]
    n_chunks = t_tokens // bsz // SSD_CHUNK
    L = SSD_CHUNK
    gd, ns, G = SSD_GROUP_DIM, SSD_D_STATE, SSD_N_GROUPS
    n_conv = gd + 2 * ns

    def pack(p):
        xs = p[..., :SSD_D_INNER].reshape(p.shape[:-1] + (G, gd))
        bs_ = p[..., SSD_D_INNER:SSD_D_INNER + SSD_GN].reshape(p.shape[:-1] + (G, ns))
        cs = p[..., SSD_D_INNER + SSD_GN:].reshape(p.shape[:-1] + (G, ns))
        return jnp.moveaxis(jnp.concatenate([xs, bs_, cs], axis=-1), -2, 0)

    cw = pack(conv_w)
    cb = pack(conv_b.reshape(1, -1))
    pad_heads = LANES - SSD_N_HEADS
    dtb = jnp.pad(dt_bias, (0, pad_heads)).reshape(1, LANES)
    alog = jnp.pad(a_log, (0, pad_heads)).reshape(1, LANES)
    dskip = jnp.repeat(d_skip, SSD_HEAD_DIM).reshape(1, SSD_D_INNER)

    def rows(b, g, c):
        return b * n_chunks + c

    x_blk0 = SSD_D_INNER // gd
    b_blk0 = (2 * SSD_D_INNER) // ns
    c_blk0 = (2 * SSD_D_INNER + SSD_GN) // ns
    return pl.pallas_call(
        _ssd_kernel,
        grid=(bsz, G, n_chunks),
        in_specs=[
            pl.BlockSpec((L, gd), lambda b, g, c: (rows(b, g, c), g)),
            pl.BlockSpec((L, gd), lambda b, g, c: (rows(b, g, c), x_blk0 + g)),
            pl.BlockSpec((L, ns), lambda b, g, c: (rows(b, g, c), b_blk0 + g)),
            pl.BlockSpec((L, ns), lambda b, g, c: (rows(b, g, c), c_blk0 + g)),
            pl.BlockSpec((L, LANES), lambda b, g, c: (rows(b, g, c), 0)),
            pl.BlockSpec((None, SSD_CONV, n_conv), lambda b, g, c: (g, 0, 0)),
            pl.BlockSpec((None, 1, n_conv), lambda b, g, c: (g, 0, 0)),
            pl.BlockSpec((1, LANES), lambda b, g, c: (0, 0)),
            pl.BlockSpec((1, LANES), lambda b, g, c: (0, 0)),
            pl.BlockSpec((1, gd), lambda b, g, c: (0, g)),
            pl.BlockSpec((1, gd), lambda b, g, c: (0, g)),
        ],
        out_specs=pl.BlockSpec((L, gd), lambda b, g, c: (rows(b, g, c), g)),
        out_shape=jax.ShapeDtypeStruct((t_tokens, SSD_D_INNER), BF16),
        scratch_shapes=[
            pltpu.VMEM((SUBLANES + L, n_conv), F32),
            pltpu.VMEM((ns, gd), F32),
            pltpu.VMEM((LANES, L), F32),
            pltpu.VMEM((LANES, L), F32),
        ],
        compiler_params=_params("arbitrary", "arbitrary", "arbitrary"),
        name="ssd",
    )(h_zx, h_zx, h_zx, h_zx, h_dt, cw, cb, dtb, alog, dskip, norm_g.reshape(1, SSD_D_INNER))


def _gelu_tanh(x):
    return 0.5 * x * (1.0 + jnp.tanh(math.sqrt(2.0 / math.pi) * (x + 0.044715 * (x * x * x))))


def _gmlp_kernel(uv_ref, lng_ref, lnb_ref, ws_ref, bs_ref, o_ref):
    L = GMLP_CHUNK
    gdim = GMLP_GROUP_DIM
    row = lax.broadcasted_iota(I32, (L, L), 0)
    col = lax.broadcasted_iota(I32, (L, L), 1)
    causal = row >= col
    for g in range(GMLP_GROUPS):
        cols = slice(g * gdim, (g + 1) * gdim)
        u = _gelu_tanh(uv_ref[:, cols])
        v = _gelu_tanh(uv_ref[:, GMLP_WIDTH + g * gdim:GMLP_WIDTH + (g + 1) * gdim])
        v = _layer_norm_rows(v, lng_ref[:, cols], lnb_ref[:, cols])
        ws = jnp.where(causal, ws_ref[g], 0.0).astype(BF16)
        sv = jnp.dot(ws, v.astype(BF16), preferred_element_type=F32) + bs_ref[:, cols]
        o_ref[:, cols] = (u * sv).astype(o_ref.dtype)


def _gmlp(h_uv, ln_g, ln_b, ws, bs):
    t_tokens = h_uv.shape[0]
    L = GMLP_CHUNK
    bs_cols = jnp.repeat(bs.T, GMLP_GROUP_DIM, axis=1)
    return pl.pallas_call(
        _gmlp_kernel,
        grid=(t_tokens // L,),
        in_specs=[
            pl.BlockSpec((L, 2 * GMLP_WIDTH), lambda i: (i, 0)),
            pl.BlockSpec((1, GMLP_WIDTH), lambda i: (0, 0)),
            pl.BlockSpec((1, GMLP_WIDTH), lambda i: (0, 0)),
            pl.BlockSpec((GMLP_GROUPS, L, L), lambda i: (0, 0, 0)),
            pl.BlockSpec((L, GMLP_WIDTH), lambda i: (0, 0)),
        ],
        out_specs=pl.BlockSpec((L, GMLP_WIDTH), lambda i: (i, 0)),
        out_shape=jax.ShapeDtypeStruct((t_tokens, GMLP_WIDTH), BF16),
        compiler_params=_params("arbitrary"),
        name="gmlp",
    )(h_uv, ln_g.reshape(1, GMLP_WIDTH), ln_b.reshape(1, GMLP_WIDTH), ws, bs_cols)


def _router_kernel(x_ref, rw_ref, rb_ref, eidx_ref, gate_ref, rank_ref, cnt_ref, carry_sc):
    tm = x_ref.shape[0]
    G, K = N_EXPERT_GROUPS, EXPERTS_PER_GROUP

    @pl.when(pl.program_id(0) == 0)
    def _():
        carry_sc[...] = jnp.zeros_like(carry_sc)

    logits = lax.dot_general(rw_ref[...], x_ref[...], (((1,), (1,)), ((), ())),
                             preferred_element_type=F32, precision=lax.Precision.HIGHEST)
    scores = jax.nn.sigmoid(logits)
    biased = scores + rb_ref[...]
    a = [biased[k * G:(k + 1) * G] for k in range(K)]
    sc = [scores[k * G:(k + 1) * G] for k in range(K)]
    hi01, lo01 = jnp.maximum(a[0], a[1]), jnp.minimum(a[0], a[1])
    hi23, lo23 = jnp.maximum(a[2], a[3]), jnp.minimum(a[2], a[3])
    grp_score = jnp.maximum(hi01, hi23) + jnp.maximum(jnp.minimum(hi01, hi23), jnp.maximum(lo01, lo23))
    gidx = lax.broadcasted_iota(I32, (G, tm), 0)
    g_best = jnp.max(grp_score, axis=0, keepdims=True)
    g_sel = jnp.min(jnp.where(grp_score == g_best, gidx, G), axis=0, keepdims=True)
    in_grp = gidx == g_sel
    cand = [jnp.sum(jnp.where(in_grp, a[k], 0.0), axis=0, keepdims=True) for k in range(K)]
    cand_sc = [jnp.sum(jnp.where(in_grp, sc[k], 0.0), axis=0, keepdims=True) for k in range(K)]

    def first_argmax(vals):
        best, idx = vals[0], jnp.zeros((1, tm), I32)
        for k in range(1, K):
            gt = vals[k] > best
            best = jnp.where(gt, vals[k], best)
            idx = jnp.where(gt, k, idx)
        return idx

    i1 = first_argmax(cand)
    i2 = first_argmax([jnp.where(i1 == k, -jnp.inf, cand[k]) for k in range(K)])
    v1 = sum(jnp.where(i1 == k, cand_sc[k], 0.0) for k in range(K))
    v2 = sum(jnp.where(i2 == k, cand_sc[k], 0.0) for k in range(K))
    den = v1 + v2
    eidx_ref[0:1, :] = g_sel * K + i1
    eidx_ref[1:2, :] = g_sel * K + i2
    gate_ref[0:1, :] = v1 / den
    gate_ref[1:2, :] = v2 / den

    member = jnp.concatenate(
        [jnp.where(in_grp & ((i1 == k) | (i2 == k)), 1.0, 0.0) for k in range(K)], axis=0)
    trow = lax.broadcasted_iota(I32, (tm, tm), 0)
    tcol = lax.broadcasted_iota(I32, (tm, tm), 1)
    before = (trow < tcol).astype(BF16)
    prefix = jnp.dot(member.astype(BF16), before, preferred_element_type=F32) + carry_sc[...]
    r1 = sum(jnp.sum(jnp.where(in_grp & (i1 == k), prefix[k * G:(k + 1) * G], 0.0), axis=0, keepdims=True)
             for k in range(K))
    r2 = sum(jnp.sum(jnp.where(in_grp & (i2 == k), prefix[k * G:(k + 1) * G], 0.0), axis=0, keepdims=True)
             for k in range(K))
    rank_ref[0:1, :] = r1.astype(I32)
    rank_ref[1:2, :] = r2.astype(I32)
    carry_sc[...] = carry_sc[...] + jnp.sum(member, axis=1, keepdims=True)
    cnt_ref[...] = jnp.broadcast_to(carry_sc[...], cnt_ref.shape).astype(I32)


def _slot_of_expert(e):
    return (e % EXPERTS_PER_GROUP) * N_EXPERT_GROUPS + e // EXPERTS_PER_GROUP


def _expert_of_slot(r):
    return (r % N_EXPERT_GROUPS) * EXPERTS_PER_GROUP + r // N_EXPERT_GROUPS


def _route(x, router_w, router_bias):
    t_tokens, d = x.shape
    tm = ROUTER_ROWS
    slot_expert = _expert_of_slot(jnp.arange(N_EXPERTS))
    rw = router_w.T[slot_expert]
    rb = router_bias[slot_expert].reshape(N_EXPERTS, 1)
    tok = pl.BlockSpec((TOP_K, tm), lambda i: (0, i))
    return pl.pallas_call(
        _router_kernel,
        grid=(t_tokens // tm,),
        in_specs=[
            pl.BlockSpec((tm, d), lambda i: (i, 0)),
            pl.BlockSpec((N_EXPERTS, d), lambda i: (0, 0)),
            pl.BlockSpec((N_EXPERTS, 1), lambda i: (0, 0)),
        ],
        out_specs=[tok, tok, tok, pl.BlockSpec((N_EXPERTS, LANES), lambda i: (0, 0))],
        out_shape=[
            jax.ShapeDtypeStruct((TOP_K, t_tokens), I32),
            jax.ShapeDtypeStruct((TOP_K, t_tokens), F32),
            jax.ShapeDtypeStruct((TOP_K, t_tokens), I32),
            jax.ShapeDtypeStruct((N_EXPERTS, LANES), I32),
        ],
        scratch_shapes=[pltpu.VMEM((N_EXPERTS, 1), F32)],
        compiler_params=_params("arbitrary"),
        name="router",
    )(x, rw, rb)


def _row_copy(src_hbm, row, dst, dst_row, sem):
    return pltpu.make_async_copy(src_hbm.at[pl.ds(row, 1), :], dst.at[pl.ds(dst_row, 1), :], sem)


def _dispatch_kernel(nused_ref, src_ref, x_hbm, zero_hbm, o_ref, buf, sem):
    blk = pl.program_id(0)
    rows = buf.shape[0]

    @pl.when(blk < nused_ref[0])
    def _():
        def start(i, carry):
            tok = src_ref[0, i]

            @pl.when(tok >= 0)
            def _():
                _row_copy(x_hbm, tok, buf, i, sem).start()

            @pl.when(tok < 0)
            def _():
                _row_copy(zero_hbm, 0, buf, i, sem).start()

            return carry

        lax.fori_loop(0, rows, start, 0)

        def wait(i, carry):
            _row_copy(zero_hbm, 0, buf, i, sem).wait()
            return carry

        lax.fori_loop(0, rows, wait, 0)
        o_ref[...] = buf[...].astype(o_ref.dtype)


def _dispatch(x, src, n_used, n_blocks):
    t_tokens, d = x.shape
    rows = MOE_ROWS
    zero = jnp.zeros((SUBLANES, d), F32)
    return pl.pallas_call(
        _dispatch_kernel,
        grid_spec=pltpu.PrefetchScalarGridSpec(
            num_scalar_prefetch=1,
            grid=(n_blocks,),
            in_specs=[
                pl.BlockSpec((None, 1, rows), lambda i, nu: (jnp.minimum(i, nu[0] - 1), 0, 0),
                             memory_space=pltpu.SMEM),
                pl.BlockSpec(memory_space=pl.ANY),
                pl.BlockSpec(memory_space=pl.ANY),
            ],
            out_specs=pl.BlockSpec((rows, d), lambda i, nu: (jnp.minimum(i, nu[0] - 1), 0)),
            scratch_shapes=[pltpu.VMEM((rows, d), F32), pltpu.SemaphoreType.DMA(())],
        ),
        out_shape=jax.ShapeDtypeStruct((n_blocks * rows, d), BF16),
        compiler_params=_params("arbitrary"),
        name="moe_dispatch",
    )(n_used, src.reshape(n_blocks, 1, rows), x, zero)


def _ffn_kernel(nused_ref, bexp_ref, xs_ref, wg_ref, wu_ref, wd_ref, o_ref, hg_sc, hu_sc, hid_sc):
    blk = pl.program_id(0)
    step = pl.program_id(1)

    @pl.when(blk < nused_ref[0])
    def _():
        @pl.when(step < MOE_K_STEPS)
        def _():
            xk = xs_ref[...]
            g = jnp.dot(xk, wg_ref[...].astype(BF16), preferred_element_type=F32)
            u = jnp.dot(xk, wu_ref[...].astype(BF16), preferred_element_type=F32)

            @pl.when(step == 0)
            def _():
                hg_sc[...] = g
                hu_sc[...] = u

            @pl.when(step > 0)
            def _():
                hg_sc[...] = hg_sc[...] + g
                hu_sc[...] = hu_sc[...] + u

            @pl.when(step == MOE_K_STEPS - 1)
            def _():
                hid_sc[...] = (_silu(hg_sc[...]) * hu_sc[...]).astype(BF16)

        @pl.when(step >= MOE_K_STEPS)
        def _():
            o_ref[...] = jnp.dot(hid_sc[...], wd_ref[...].astype(BF16), preferred_element_type=F32)


def _expert_ffn(xs, blk_expert, n_used, w_gate, w_up, w_down, layer):
    n_rows, d = xs.shape
    rows = MOE_ROWS
    n_blocks = n_rows // rows
    kc = d // MOE_K_STEPS
    oc = d // MOE_O_STEPS
    last = MOE_K_STEPS + MOE_O_STEPS - 1

    def pos(i, s, nu):
        used = i < nu[0]
        return jnp.where(used, i, nu[0] - 1), jnp.where(used, s, last)

    def k_idx(s):
        return jnp.minimum(s, MOE_K_STEPS - 1)

    def o_idx(s):
        return jnp.maximum(s - MOE_K_STEPS, 0)

    def xs_map(i, s, nu, be):
        ii, ss = pos(i, s, nu)
        return ii, k_idx(ss)

    def w_in_map(i, s, nu, be):
        ii, ss = pos(i, s, nu)
        return layer, be[ii], k_idx(ss), 0

    def w_down_map(i, s, nu, be):
        ii, ss = pos(i, s, nu)
        return layer, be[ii], 0, o_idx(ss)

    def out_map(i, s, nu, be):
        ii, ss = pos(i, s, nu)
        return ii, o_idx(ss)

    return pl.pallas_call(
        _ffn_kernel,
        grid_spec=pltpu.PrefetchScalarGridSpec(
            num_scalar_prefetch=2,
            grid=(n_blocks, MOE_K_STEPS + MOE_O_STEPS),
            in_specs=[
                pl.BlockSpec((rows, kc), xs_map),
                pl.BlockSpec((None, None, kc, D_FF), w_in_map),
                pl.BlockSpec((None, None, kc, D_FF), w_in_map),
                pl.BlockSpec((None, None, D_FF, oc), w_down_map),
            ],
            out_specs=pl.BlockSpec((rows, oc), out_map),
            scratch_shapes=[
                pltpu.VMEM((rows, D_FF), F32),
                pltpu.VMEM((rows, D_FF), F32),
                pltpu.VMEM((rows, D_FF), BF16),
            ],
        ),
        out_shape=jax.ShapeDtypeStruct((n_rows, d), F32),
        compiler_params=_params("arbitrary", "arbitrary"),
        name="moe_ffn",
    )(n_used, blk_expert, xs, w_gate, w_up, w_down)


def _combine_kernel(dest_ref, ys_hbm, gate_ref, x_ref, g_ref, b_ref, o_ref, obf_ref, buf, sem):
    tm = x_ref.shape[0]

    def start(i, carry):
        for k in range(TOP_K):
            _row_copy(ys_hbm, dest_ref[0, k, i], buf.at[k], i, sem).start()
        return carry

    lax.fori_loop(0, tm, start, 0)

    def wait(i, carry):
        for k in range(TOP_K):
            _row_copy(ys_hbm, 0, buf.at[k], i, sem).wait()
        return carry

    lax.fori_loop(0, tm, wait, 0)
    gates = gate_ref[...]
    ffn = gates[:, 0:1] * buf[0] + gates[:, 1:2] * buf[1]
    y = _layer_norm_rows(ALPHA * x_ref[...] + ffn, g_ref[...], b_ref[...])
    o_ref[...] = y
    obf_ref[...] = y.astype(BF16)


def _combine_layer_norm(x, ys, dest, gate, g, b):
    t_tokens, d = x.shape
    tm = LN_ROWS
    n_steps = t_tokens // tm
    dest_blocks = dest.reshape(TOP_K, n_steps, tm).transpose(1, 0, 2)
    row = pl.BlockSpec((tm, d), lambda i: (i, 0))
    vec = pl.BlockSpec((1, d), lambda i: (0, 0))
    return pl.pallas_call(
        _combine_kernel,
        grid=(n_steps,),
        in_specs=[
            pl.BlockSpec((1, TOP_K, tm), lambda i: (i, 0, 0), memory_space=pltpu.SMEM),
            pl.BlockSpec(memory_space=pl.ANY),
            pl.BlockSpec((tm, TOP_K), lambda i: (i, 0)),
            row, vec, vec,
        ],
        out_specs=[row, row],
        out_shape=[jax.ShapeDtypeStruct((t_tokens, d), F32), jax.ShapeDtypeStruct((t_tokens, d), BF16)],
        scratch_shapes=[pltpu.VMEM((TOP_K, tm, d), F32), pltpu.SemaphoreType.DMA(())],
        compiler_params=_params("arbitrary"),
        name="moe_combine",
    )(dest_blocks, ys, gate.T, x, g.reshape(1, d), b.reshape(1, d))


def _moe_layer_norm(x, layer, router_w, router_bias, w_gate, w_up, w_down, ln_g, ln_b):
    t_tokens, _ = x.shape
    rows = MOE_ROWS
    eidx, gate, rank, counts = _route(x, router_w, router_bias)
    cnt = counts[:, 0]
    padded = (cnt + rows - 1) // rows * rows
    pad_end = jnp.cumsum(padded)
    pad_start = pad_end - padded
    dest = pad_start[_slot_of_expert(eidx)] + rank
    n_blocks = (t_tokens * TOP_K) // rows + N_EXPERTS
    n_used = (pad_end[-1] // rows).astype(I32).reshape(1)
    tok = jnp.broadcast_to(jnp.arange(t_tokens, dtype=I32), (TOP_K, t_tokens))
    src = jnp.full((n_blocks * rows,), -1, I32).at[dest.reshape(-1)].set(tok.reshape(-1))
    blk_slot = jnp.minimum(jnp.sum(jnp.arange(n_blocks)[:, None] * rows >= pad_end[None, :], axis=1),
                           N_EXPERTS - 1)
    blk_expert = _expert_of_slot(blk_slot).astype(I32)

    xs = _dispatch(x, src, n_used, n_blocks)
    ys = _expert_ffn(xs, blk_expert, n_used, w_gate, w_up, w_down, layer)
    return _combine_layer_norm(x, ys, dest, gate, ln_g, ln_b)


def kernel(x, rel_bias, even_w_in, even_w_out, diff_lambda, diff_subln_g, odd_w_in, odd_w_out, ssd_conv_w, ssd_conv_b, ssd_dt_bias, ssd_a_log, ssd_d, ssd_norm_g, gmlp_ln_g, gmlp_ln_b, gmlp_ws, gmlp_bs, router_w, router_bias, moe_w_gate, moe_w_up, moe_w_down, ln_mix_g, ln_mix_b, ln_ffn_g, ln_ffn_b):
    bsz, s, d = x.shape
    t_tokens = bsz * s
    xf = x.reshape(t_tokens, d)
    xb = xf.astype(BF16)
    moe_args = (router_w, router_bias, moe_w_gate, moe_w_up, moe_w_down)

    lambda_init = 0.8 - 0.6 * math.exp(-0.3 * 0)
    h = _project([xb], even_w_in[0], [0], 0, EVEN_IN, BF16, tm=1024, tn=512)
    attn = _even_attention(h.reshape(bsz, s, EVEN_IN), _bias_tiles(rel_bias), diff_lambda[0],
                           diff_subln_g[0], lambda_init)
    mix = _project([attn.reshape(t_tokens, d)], even_w_out[0], [0], 0, d, F32, tm=1024, tn=512)
    xf, xb = _residual_layer_norm(xf, mix, ln_mix_g[0], ln_mix_b[0])
    xf, xb = _moe_layer_norm(xf, 0, *moe_args, ln_ffn_g[0], ln_ffn_b[0])

    zx_cols = SSD_D_INNER + SSD_CONV_DIM
    h_zx = _project([xb], odd_w_in[0], [0], 0, zx_cols, F32, tm=1024, tn=512)
    h_dt = _project([xb], odd_w_in[0], [0], zx_cols // LANES, LANES, F32, tm=1024, tn=LANES)
    h_uv = _project([xb], odd_w_in[0][:, C_IN:], [0], 0, 2 * GMLP_WIDTH, F32, tm=1024, tn=512)
    y_ssd = _ssd(h_zx, h_dt, bsz, ssd_conv_w[0], ssd_conv_b[0], ssd_dt_bias[0], ssd_a_log[0], ssd_d[0],
                 ssd_norm_g[0])
    y_gmlp = _gmlp(h_uv, gmlp_ln_g[0], gmlp_ln_b[0], gmlp_ws[0], gmlp_bs[0])
    mix = _project([y_ssd, y_gmlp], odd_w_out[0], [0, SSD_D_INNER // GMLP_WIDTH], 0, d, F32, tm=512, tn=512)
    xf, xb = _residual_layer_norm(xf, mix, ln_mix_g[1], ln_mix_b[1])
    xf, xb = _moe_layer_norm(xf, 1, *moe_args, ln_ffn_g[1], ln_ffn_b[1])
    return xf.reshape(bsz, s, d)
```

```python
import functools
import math

import jax
import jax.numpy as jnp
from jax import lax
from jax.experimental import pallas as pl
from jax.experimental.pallas import tpu as pltpu

F32 = jnp.float32
BF16 = jnp.bfloat16
I32 = jnp.int32

D_MODEL = 4096
DEPTH = 2
HEAD_DIM = 128
N_HEADS_A = 16
N_HEADS_B = 16
DIFF_HALF = HEAD_DIM // 2
N_BUCKETS = 32
MAX_DISTANCE = 128
A_IN = 3 * N_HEADS_A * HEAD_DIM
EVEN_IN = A_IN + 3 * N_HEADS_B * HEAD_DIM
DIFF_SCALE = DIFF_HALF ** -0.5
SB_SCALE = HEAD_DIM ** -0.5

SSD_HEAD_DIM = 64
SSD_D_INNER = D_MODEL
SSD_N_HEADS = SSD_D_INNER // SSD_HEAD_DIM
SSD_N_GROUPS = 8
SSD_HEADS_PER_GROUP = SSD_N_HEADS // SSD_N_GROUPS
SSD_D_STATE = 128
SSD_CONV = 4
SSD_CHUNK = 128
SSD_GN = SSD_N_GROUPS * SSD_D_STATE
SSD_CONV_DIM = SSD_D_INNER + 2 * SSD_GN
SSD_GROUP_DIM = SSD_D_INNER // SSD_N_GROUPS
C_IN = SSD_D_INNER + SSD_CONV_DIM + SSD_N_HEADS
GMLP_WIDTH = D_MODEL // 2
GMLP_GROUPS = 8
GMLP_GROUP_DIM = GMLP_WIDTH // GMLP_GROUPS
GMLP_CHUNK = 128

N_EXPERTS = 32
N_EXPERT_GROUPS = 8
EXPERTS_PER_GROUP = N_EXPERTS // N_EXPERT_GROUPS
TOP_K = 2
D_FF = 768

ALPHA = (2 * DEPTH) ** 0.25
LN_EPS = 1e-5

LANES = 128
SUBLANES = 8
VMEM_LIMIT_BYTES = 58 * 1024 * 1024

ATTN_TILE = 256
ATTN_HEADS_PER_STEP = 4
ATTN_ONES_ROWS = 16
MOE_ROWS = 512
MOE_K_STEPS = 4
MOE_O_STEPS = 2
ROUTER_ROWS = 512
LN_ROWS = 256


def _params(*sem):
    return pltpu.CompilerParams(dimension_semantics=sem, vmem_limit_bytes=VMEM_LIMIT_BYTES)


def _proj_kernel(*refs, n_in):
    x_refs = refs[:n_in]
    w_refs = refs[n_in:2 * n_in]
    o_ref = refs[2 * n_in]
    wbf_refs = refs[2 * n_in + 1:]

    @pl.when(pl.program_id(1) == 0)
    def _():
        for w_ref, wbf_ref in zip(w_refs, wbf_refs):
            wbf_ref[...] = w_ref[...].astype(BF16)

    acc = jnp.dot(x_refs[0][...], wbf_refs[0][...], preferred_element_type=F32)
    for x_ref, wbf_ref in zip(x_refs[1:], wbf_refs[1:]):
        acc = acc + jnp.dot(x_ref[...], wbf_ref[...], preferred_element_type=F32)
    o_ref[...] = acc.astype(o_ref.dtype)


def _project(xs, w, row_blocks, col_block0, n_cols, out_dtype, tm, tn):
    m = xs[0].shape[0]
    n_in = len(xs)
    in_specs = [pl.BlockSpec((tm, x.shape[1]), lambda j, i: (i, 0)) for x in xs]
    for x, rb in zip(xs, row_blocks):
        in_specs.append(pl.BlockSpec((x.shape[1], tn), lambda j, i, rb=rb: (rb, j + col_block0)))
    return pl.pallas_call(
        functools.partial(_proj_kernel, n_in=n_in),
        grid=(n_cols // tn, m // tm),
        in_specs=in_specs,
        out_specs=pl.BlockSpec((tm, tn), lambda j, i: (i, j)),
        out_shape=jax.ShapeDtypeStruct((m, n_cols), out_dtype),
        scratch_shapes=[pltpu.VMEM((x.shape[1], tn), BF16) for x in xs],
        compiler_params=_params("arbitrary", "arbitrary"),
        name="proj",
    )(*xs, *([w] * n_in))


def _layer_norm_rows(y, g, b):
    mu = jnp.mean(y, axis=-1, keepdims=True)
    yc = y - mu
    var = jnp.mean(yc * yc, axis=-1, keepdims=True)
    return yc * lax.rsqrt(var + LN_EPS) * g + b


def _res_ln_kernel(x_ref, mix_ref, g_ref, b_ref, o_ref, obf_ref):
    y = _layer_norm_rows(ALPHA * x_ref[...] + mix_ref[...], g_ref[...], b_ref[...])
    o_ref[...] = y
    obf_ref[...] = y.astype(BF16)


def _residual_layer_norm(x, mix, g, b):
    m, d = x.shape
    tm = LN_ROWS
    row = pl.BlockSpec((tm, d), lambda i: (i, 0))
    vec = pl.BlockSpec((1, d), lambda i: (0, 0))
    return pl.pallas_call(
        _res_ln_kernel,
        grid=(m // tm,),
        in_specs=[row, row, vec, vec],
        out_specs=[row, row],
        out_shape=[jax.ShapeDtypeStruct((m, d), F32), jax.ShapeDtypeStruct((m, d), BF16)],
        compiler_params=_params("arbitrary"),
        name="res_ln",
    )(x, mix, g.reshape(1, d), b.reshape(1, d))


def _bias_tile_kernel(rb_ref, o_ref):
    head = pl.program_id(0)
    t = ATTN_TILE
    key = lax.broadcasted_iota(I32, (t, t), 0)
    qry = lax.broadcasted_iota(I32, (t, t), 1)
    max_exact = N_BUCKETS // 2
    for d in range(2):
        n = jnp.maximum(d * t + qry - key, 0)
        nf = jnp.maximum(n, 1).astype(F32)
        large = max_exact + (jnp.log(nf / max_exact) / math.log(MAX_DISTANCE / max_exact)
                             * (N_BUCKETS - max_exact)).astype(I32)
        large = jnp.minimum(large, N_BUCKETS - 1)
        bucket = jnp.where(n < max_exact, n, large)
        acc = jnp.zeros((t, t), F32)
        for bkt in range(N_BUCKETS):
            acc = jnp.where(bucket == bkt, rb_ref[bkt, head], acc)
        o_ref[d] = acc


def _bias_tiles(rel_bias):
    t = ATTN_TILE
    return pl.pallas_call(
        _bias_tile_kernel,
        grid=(N_HEADS_A,),
        in_specs=[pl.BlockSpec(memory_space=pltpu.SMEM)],
        out_specs=pl.BlockSpec((None, 2, t, t), lambda h: (h, 0, 0, 0)),
        out_shape=jax.ShapeDtypeStruct((N_HEADS_A, 2, t, t), F32),
        compiler_params=_params("arbitrary"),
        name="bias_tiles",
    )(rel_bias)


def _dot_nt(a, b):
    return lax.dot_general(a, b, (((1,), (1,)), ((), ())), preferred_element_type=F32)


def _head_cols(h):
    return slice(h * HEAD_DIM, (h + 1) * HEAD_DIM)


def _value_transposes(v_ref, vt_sc):
    t = ATTN_TILE
    ones = jnp.ones((ATTN_ONES_ROWS, t), BF16)
    for h in range(ATTN_HEADS_PER_STEP):
        for j in range(v_ref.shape[0] // t):
            vb = v_ref[j * t:(j + 1) * t, _head_cols(h)]
            vt_sc[h, j, 0:HEAD_DIM, :] = vb.astype(F32).T.astype(BF16)
            vt_sc[h, j, HEAD_DIM:, :] = ones


def _diff_attention(dl_ref, g_ref, q_ref, k_ref, vt_sc, bias_ref, o_ref, m_sc, acc_sc, lambda_init):
    t = ATTN_TILE
    qi = pl.program_id(2)
    lane = lax.broadcasted_iota(I32, (t, HEAD_DIM), 1)
    q_maps = []
    for h in range(ATTN_HEADS_PER_STEP):
        q = q_ref[:, _head_cols(h)] * DIFF_SCALE
        zero = jnp.zeros_like(q)
        q_maps.append((jnp.where(lane < DIFF_HALF, q, zero), jnp.where(lane >= DIFF_HALF, q, zero)))

    chains = [(h, mp) for h in range(ATTN_HEADS_PER_STEP) for mp in range(2)]

    def block(j, bias_of_head, mask, first):
        kv_start = pl.multiple_of(j * t, t)
        kbs = [k_ref[pl.ds(kv_start, t), _head_cols(h)] for h in range(ATTN_HEADS_PER_STEP)]
        scores = [_dot_nt(kbs[h], q_maps[h][mp]) for h, mp in chains]
        probs, alphas = [], []
        for (h, mp), s in zip(chains, scores):
            s = s + bias_of_head(h)
            if mask is not None:
                s = jnp.where(mask, s, -jnp.inf)
            m_new = jnp.max(s, axis=0, keepdims=True)
            if not first:
                m_old = m_sc[h, mp]
                m_new = jnp.maximum(m_old, m_new)
                alphas.append(jnp.exp(m_old - m_new))
            m_sc[h, mp] = m_new
            probs.append(jnp.exp(s - m_new).astype(BF16))
        for i, (h, mp) in enumerate(chains):
            pv = jnp.dot(vt_sc[h, j], probs[i], preferred_element_type=F32)
            acc_sc[h, mp] = pv if first else alphas[i] * acc_sc[h, mp] + pv

    key = lax.broadcasted_iota(I32, (t, t), 0)
    qry = lax.broadcasted_iota(I32, (t, t), 1)
    block(qi, lambda h: bias_ref[h, 0], qry >= key, True)

    @pl.when(qi >= 1)
    def _():
        block(qi - 1, lambda h: bias_ref[h, 1], None, False)

    def far_body(j, carry):
        block(j, lambda h: bias_ref[h, 1, 0:1, t - 1:t], None, False)
        return carry

    lax.fori_loop(0, qi - 1, far_body, 0)

    dl = dl_ref[...]
    lam = (jnp.exp(jnp.sum(dl[0:1] * dl[1:2], axis=-1, keepdims=True))
           - jnp.exp(jnp.sum(dl[2:3] * dl[3:4], axis=-1, keepdims=True)) + lambda_init)
    for h in range(ATTN_HEADS_PER_STEP):
        a0 = acc_sc[h, 0]
        a1 = acc_sc[h, 1]
        oa = (a0[0:HEAD_DIM] / a0[HEAD_DIM:HEAD_DIM + 1]
              - lam * (a1[0:HEAD_DIM] / a1[HEAD_DIM:HEAD_DIM + 1]))
        ms = jnp.mean(oa * oa, axis=0, keepdims=True)
        oa = (oa * lax.rsqrt(ms + LN_EPS)).T * g_ref[...] * (1.0 - lambda_init)
        o_ref[:, _head_cols(h)] = oa.astype(o_ref.dtype)


def _log_sigmoid(z):
    return jnp.minimum(z, 0.0) - jnp.log(1.0 + jnp.exp(-jnp.abs(z)))


def _split_bf16(x):
    hi = x.astype(BF16)
    lo = (x - hi.astype(F32)).astype(BF16)
    return hi, lo


def _stick_breaking_attention(q_ref, k_ref, vt_sc, o_ref, c_sc, acc_sc):
    t = ATTN_TILE
    qi = pl.program_id(2)
    key = lax.broadcasted_iota(I32, (t, t), 0)
    qry = lax.broadcasted_iota(I32, (t, t), 1)
    later = qry > key
    after = later.astype(BF16)

    heads = range(ATTN_HEADS_PER_STEP)

    def block(j, strict, first):
        kv_start = pl.multiple_of(j * t, t)
        zs = [_dot_nt(k_ref[pl.ds(kv_start, t), _head_cols(h)], q_ref[:, _head_cols(h)]) for h in heads]
        log_betas, splits, col_sums = [], [], []
        for h in heads:
            z = zs[h] * SB_SCALE
            log_beta = _log_sigmoid(z)
            log_1mb = log_beta - z
            if strict is not None:
                log_1mb = jnp.where(strict, log_1mb, 0.0)
            log_betas.append(log_beta)
            splits.append(_split_bf16(log_1mb))
            col_sums.append(jnp.sum(log_1mb, axis=0, keepdims=True))
        tails = [jnp.dot(after, hi, preferred_element_type=F32) + jnp.dot(after, lo, preferred_element_type=F32)
                 for hi, lo in splits]
        weights = []
        for h in heads:
            tail = tails[h] if first else tails[h] + c_sc[h]
            w = jnp.exp(log_betas[h] + tail)
            if strict is not None:
                w = jnp.where(strict, w, 0.0)
            weights.append(w.astype(BF16))
            c_sc[h] = col_sums[h] if first else c_sc[h] + col_sums[h]
        for h in heads:
            pv = jnp.dot(vt_sc[h, j][0:HEAD_DIM], weights[h], preferred_element_type=F32)
            acc_sc[h, 0, 0:HEAD_DIM] = pv if first else acc_sc[h, 0, 0:HEAD_DIM] + pv

    block(qi, later, True)

    def body(step, carry):
        block(qi - 1 - step, None, False)
        return carry

    lax.fori_loop(0, qi, body, 0)
    for h in range(ATTN_HEADS_PER_STEP):
        o_ref[:, _head_cols(h)] = acc_sc[h, 0, 0:HEAD_DIM].T.astype(o_ref.dtype)


def _attn_kernel(dl_ref, g_ref, q_ref, k_ref, v_ref, bias_ref, o_ref, vt_sc, m_sc, c_sc, acc_sc, *, lambda_init):
    grp = pl.program_id(1)

    @pl.when(pl.program_id(2) == 0)
    def _():
        _value_transposes(v_ref, vt_sc)

    @pl.when(grp < N_HEADS_A // ATTN_HEADS_PER_STEP)
    def _():
        _diff_attention(dl_ref, g_ref, q_ref, k_ref, vt_sc, bias_ref, o_ref, m_sc, acc_sc, lambda_init)

    @pl.when(grp >= N_HEADS_A // ATTN_HEADS_PER_STEP)
    def _():
        _stick_breaking_attention(q_ref, k_ref, vt_sc, o_ref, c_sc, acc_sc)


def _even_attention(h, bias_tiles, diff_lambda, subln_g, lambda_init):
    bsz, s, _ = h.shape
    t = ATTN_TILE
    nh = ATTN_HEADS_PER_STEP
    width = nh * HEAD_DIM
    groups_a = N_HEADS_A // nh
    n_groups = (N_HEADS_A + N_HEADS_B) // nh

    def q_col(g):
        return g + jnp.where(g >= groups_a, 2 * groups_a, 0)

    return pl.pallas_call(
        functools.partial(_attn_kernel, lambda_init=lambda_init),
        grid=(bsz, n_groups, s // t),
        in_specs=[
            pl.BlockSpec((4, DIFF_HALF), lambda b, g, qi: (0, 0)),
            pl.BlockSpec((1, HEAD_DIM), lambda b, g, qi: (0, 0)),
            pl.BlockSpec((None, t, width), lambda b, g, qi: (b, qi, q_col(g))),
            pl.BlockSpec((None, s, width), lambda b, g, qi: (b, 0, q_col(g) + groups_a)),
            pl.BlockSpec((None, s, width), lambda b, g, qi: (b, 0, q_col(g) + 2 * groups_a)),
            pl.BlockSpec((nh, 2, t, t), lambda b, g, qi: (jnp.minimum(g, groups_a - 1), 0, 0, 0)),
        ],
        out_specs=pl.BlockSpec((None, t, width), lambda b, g, qi: (b, qi, g)),
        out_shape=jax.ShapeDtypeStruct((bsz, s, n_groups * width), BF16),
        scratch_shapes=[
            pltpu.VMEM((nh, s // t, HEAD_DIM + ATTN_ONES_ROWS, t), BF16),
            pltpu.VMEM((nh, 2, 1, t), F32),
            pltpu.VMEM((nh, 1, t), F32),
            pltpu.VMEM((nh, 2, HEAD_DIM + ATTN_ONES_ROWS, t), F32),
        ],
        compiler_params=_params("arbitrary", "arbitrary", "arbitrary"),
        name="even_attention",
    )(diff_lambda, subln_g.reshape(1, HEAD_DIM), h, h, h, bias_tiles)


def _silu(x):
    return x * jax.nn.sigmoid(x)


def _softplus(x):
    return jnp.maximum(x, 0.0) + jnp.log(1.0 + jnp.exp(-jnp.abs(x)))


def _ssd_kernel(z_ref, x_ref, b_ref, c_ref, dt_ref, cw_ref, cb_ref, dtb_ref, alog_ref, dskip_ref, ng_ref,
                o_ref, pad_sc, state_sc, acst_sc, dtt_sc):
    grp = pl.program_id(1)
    chunk = pl.program_id(2)
    L = SSD_CHUNK
    P2 = 2 * SSD_HEAD_DIM
    n_conv = SSD_GROUP_DIM + 2 * SSD_D_STATE
    halo = SUBLANES

    @pl.when(chunk == 0)
    def _():
        pad_sc[0:halo, :] = jnp.zeros((halo, n_conv), F32)
        state_sc[...] = jnp.zeros_like(state_sc)

    pad_sc[halo:halo + L, 0:SSD_GROUP_DIM] = x_ref[...]
    pad_sc[halo:halo + L, SSD_GROUP_DIM:SSD_GROUP_DIM + SSD_D_STATE] = b_ref[...]
    pad_sc[halo:halo + L, SSD_GROUP_DIM + SSD_D_STATE:n_conv] = c_ref[...]
    conv = cb_ref[...]
    for j in range(SSD_CONV):
        start = halo - (SSD_CONV - 1) + j
        conv = conv + cw_ref[j:j + 1, :] * pad_sc[start:start + L, :]
    pad_sc[0:halo, :] = pad_sc[L:L + halo, :]
    xbc = _silu(conv)
    xg = xbc[:, 0:SSD_GROUP_DIM]
    bg = xbc[:, SSD_GROUP_DIM:SSD_GROUP_DIM + SSD_D_STATE]
    cg = xbc[:, SSD_GROUP_DIM + SSD_D_STATE:n_conv]
    xg_bf = xg.astype(BF16)

    dt = _softplus(dt_ref[...] + dtb_ref[...])
    adt = dt * (-jnp.exp(alog_ref[...]))
    row = lax.broadcasted_iota(I32, (L, L), 0)
    col = lax.broadcasted_iota(I32, (L, L), 1)
    causal = row >= col
    acs = jnp.dot(causal.astype(F32), adt, preferred_element_type=F32, precision=lax.Precision.HIGHEST)
    acst_sc[...] = acs.T
    dtt_sc[...] = dt.T

    cb = _dot_nt(cg.astype(BF16), bg.astype(BF16))
    bgt = bg.T
    cg_bf = cg.astype(BF16)
    prev = state_sc[...]
    y_off = jnp.dot(cg_bf, prev.astype(BF16), preferred_element_type=F32)

    lane = lax.broadcasted_iota(I32, (L, P2), 1)
    first_head = lane < SSD_HEAD_DIM
    lane_row = lax.broadcasted_iota(I32, (1, P2), 1) < SSD_HEAD_DIM
    y_pairs = []
    for pair in range(SSD_HEADS_PER_GROUP // 2):
        x_pair = xg_bf[:, pair * P2:(pair + 1) * P2]
        y_diag, st, e_acs, decay = [], [], [], []
        for sub in range(2):
            head = grp * SSD_HEADS_PER_GROUP + 2 * pair + sub
            acs_row = acst_sc[pl.ds(head, 1), :]
            dt_row = dtt_sc[pl.ds(head, 1), :]
            row_b = jnp.broadcast_to(acs_row, (L, L))
            col_b = row_b.T
            decay_m = jnp.exp(jnp.where(causal, col_b - row_b, -jnp.inf))
            m = (cb * decay_m * dt_row).astype(BF16)
            y_diag.append(jnp.dot(m, x_pair, preferred_element_type=F32))
            acs_last = col_b[L - 1:L, :]
            w_row = jnp.exp(acs_last - acs_row) * dt_row
            st.append(jnp.dot((bgt * w_row).astype(BF16), x_pair, preferred_element_type=F32))
            e_acs.append(jnp.exp(col_b))
            decay.append(jnp.exp(acs_last))
        cols = slice(pair * P2, (pair + 1) * P2)
        y_pair = (jnp.where(first_head, y_diag[0], y_diag[1])
                  + jnp.where(first_head, e_acs[0], e_acs[1]) * y_off[:, cols])
        state_sc[:, cols] = (prev[:, cols] * jnp.where(lane_row, decay[0], decay[1])
                             + jnp.where(first_head, st[0], st[1]))
        y_pair = y_pair + dskip_ref[:, cols] * xg[:, cols]
        y_pairs.append(y_pair * _silu(z_ref[:, cols]))

    y = jnp.concatenate(y_pairs, axis=-1)
    ms = jnp.mean(y * y, axis=-1, keepdims=True)
    o_ref[...] = (y * lax.rsqrt(ms + LN_EPS) * ng_ref[...]).astype(o_ref.dtype)


def _ssd(h_zx, h_dt, bsz, conv_w, conv_b, dt_bias, a_log, d_skip, norm_g):
    t_tokens = h_zx.shape[0]
    n_chunks = t_tokens // bsz // SSD_CHUNK
    L = SSD_CHUNK
    gd, ns, G = SSD_GROUP_DIM, SSD_D_STATE, SSD_N_GROUPS
    n_conv = gd + 2 * ns

    def pack(p):
        xs = p[..., :SSD_D_INNER].reshape(p.shape[:-1] + (G, gd))
        bs_ = p[..., SSD_D_INNER:SSD_D_INNER + SSD_GN].reshape(p.shape[:-1] + (G, ns))
        cs = p[..., SSD_D_INNER + SSD_GN:].reshape(p.shape[:-1] + (G, ns))
        return jnp.moveaxis(jnp.concatenate([xs, bs_, cs], axis=-1), -2, 0)

    cw = pack(conv_w)
    cb = pack(conv_b.reshape(1, -1))
    pad_heads = LANES - SSD_N_HEADS
    dtb = jnp.pad(dt_bias, (0, pad_heads)).reshape(1, LANES)
    alog = jnp.pad(a_log, (0, pad_heads)).reshape(1, LANES)
    dskip = jnp.repeat(d_skip, SSD_HEAD_DIM).reshape(1, SSD_D_INNER)

    def rows(b, g, c):
        return b * n_chunks + c

    x_blk0 = SSD_D_INNER // gd
    b_blk0 = (2 * SSD_D_INNER) // ns
    c_blk0 = (2 * SSD_D_INNER + SSD_GN) // ns
    return pl.pallas_call(
        _ssd_kernel,
        grid=(bsz, G, n_chunks),
        in_specs=[
            pl.BlockSpec((L, gd), lambda b, g, c: (rows(b, g, c), g)),
            pl.BlockSpec((L, gd), lambda b, g, c: (rows(b, g, c), x_blk0 + g)),
            pl.BlockSpec((L, ns), lambda b, g, c: (rows(b, g, c), b_blk0 + g)),
            pl.BlockSpec((L, ns), lambda b, g, c: (rows(b, g, c), c_blk0 + g)),
            pl.BlockSpec((L, LANES), lambda b, g, c: (rows(b, g, c), 0)),
            pl.BlockSpec((None, SSD_CONV, n_conv), lambda b, g, c: (g, 0, 0)),
            pl.BlockSpec((None, 1, n_conv), lambda b, g, c: (g, 0, 0)),
            pl.BlockSpec((1, LANES), lambda b, g, c: (0, 0)),
            pl.BlockSpec((1, LANES), lambda b, g, c: (0, 0)),
            pl.BlockSpec((1, gd), lambda b, g, c: (0, g)),
            pl.BlockSpec((1, gd), lambda b, g, c: (0, g)),
        ],
        out_specs=pl.BlockSpec((L, gd), lambda b, g, c: (rows(b, g, c), g)),
        out_shape=jax.ShapeDtypeStruct((t_tokens, SSD_D_INNER), BF16),
        scratch_shapes=[
            pltpu.VMEM((SUBLANES + L, n_conv), F32),
            pltpu.VMEM((ns, gd), F32),
            pltpu.VMEM((LANES, L), F32),
            pltpu.VMEM((LANES, L), F32),
        ],
        compiler_params=_params("arbitrary", "arbitrary", "arbitrary"),
        name="ssd",
    )(h_zx, h_zx, h_zx, h_zx, h_dt, cw, cb, dtb, alog, dskip, norm_g.reshape(1, SSD_D_INNER))


def _gelu_tanh(x):
    return 0.5 * x * (1.0 + jnp.tanh(math.sqrt(2.0 / math.pi) * (x + 0.044715 * (x * x * x))))


def _gmlp_kernel(uv_ref, lng_ref, lnb_ref, ws_ref, bs_ref, o_ref):
    L = GMLP_CHUNK
    gdim = GMLP_GROUP_DIM
    row = lax.broadcasted_iota(I32, (L, L), 0)
    col = lax.broadcasted_iota(I32, (L, L), 1)
    causal = row >= col
    for g in range(GMLP_GROUPS):
        cols = slice(g * gdim, (g + 1) * gdim)
        u = _gelu_tanh(uv_ref[:, cols])
        v = _gelu_tanh(uv_ref[:, GMLP_WIDTH + g * gdim:GMLP_WIDTH + (g + 1) * gdim])
        v = _layer_norm_rows(v, lng_ref[:, cols], lnb_ref[:, cols])
        ws = jnp.where(causal, ws_ref[g], 0.0).astype(BF16)
        sv = jnp.dot(ws, v.astype(BF16), preferred_element_type=F32) + bs_ref[:, cols]
        o_ref[:, cols] = (u * sv).astype(o_ref.dtype)


def _gmlp(h_uv, ln_g, ln_b, ws, bs):
    t_tokens = h_uv.shape[0]
    L = GMLP_CHUNK
    bs_cols = jnp.repeat(bs.T, GMLP_GROUP_DIM, axis=1)
    return pl.pallas_call(
        _gmlp_kernel,
        grid=(t_tokens // L,),
        in_specs=[
            pl.BlockSpec((L, 2 * GMLP_WIDTH), lambda i: (i, 0)),
            pl.BlockSpec((1, GMLP_WIDTH), lambda i: (0, 0)),
            pl.BlockSpec((1, GMLP_WIDTH), lambda i: (0, 0)),
            pl.BlockSpec((GMLP_GROUPS, L, L), lambda i: (0, 0, 0)),
            pl.BlockSpec((L, GMLP_WIDTH), lambda i: (0, 0)),
        ],
        out_specs=pl.BlockSpec((L, GMLP_WIDTH), lambda i: (i, 0)),
        out_shape=jax.ShapeDtypeStruct((t_tokens, GMLP_WIDTH), BF16),
        compiler_params=_params("arbitrary"),
        name="gmlp",
    )(h_uv, ln_g.reshape(1, GMLP_WIDTH), ln_b.reshape(1, GMLP_WIDTH), ws, bs_cols)


def _router_kernel(x_ref, rw_ref, rb_ref, eidx_ref, gate_ref, rank_ref, cnt_ref, carry_sc):
    tm = x_ref.shape[0]
    G, K = N_EXPERT_GROUPS, EXPERTS_PER_GROUP

    @pl.when(pl.program_id(0) == 0)
    def _():
        carry_sc[...] = jnp.zeros_like(carry_sc)

    logits = lax.dot_general(rw_ref[...], x_ref[...], (((1,), (1,)), ((), ())),
                             preferred_element_type=F32, precision=lax.Precision.HIGHEST)
    scores = jax.nn.sigmoid(logits)
    biased = scores + rb_ref[...]
    a = [biased[k * G:(k + 1) * G] for k in range(K)]
    sc = [scores[k * G:(k + 1) * G] for k in range(K)]
    hi01, lo01 = jnp.maximum(a[0], a[1]), jnp.minimum(a[0], a[1])
    hi23, lo23 = jnp.maximum(a[2], a[3]), jnp.minimum(a[2], a[3])
    grp_score = jnp.maximum(hi01, hi23) + jnp.maximum(jnp.minimum(hi01, hi23), jnp.maximum(lo01, lo23))
    gidx = lax.broadcasted_iota(I32, (G, tm), 0)
    g_best = jnp.max(grp_score, axis=0, keepdims=True)
    g_sel = jnp.min(jnp.where(grp_score == g_best, gidx, G), axis=0, keepdims=True)
    in_grp = gidx == g_sel
    cand = [jnp.sum(jnp.where(in_grp, a[k], 0.0), axis=0, keepdims=True) for k in range(K)]
    cand_sc = [jnp.sum(jnp.where(in_grp, sc[k], 0.0), axis=0, keepdims=True) for k in range(K)]

    def first_argmax(vals):
        best, idx = vals[0], jnp.zeros((1, tm), I32)
        for k in range(1, K):
            gt = vals[k] > best
            best = jnp.where(gt, vals[k], best)
            idx = jnp.where(gt, k, idx)
        return idx

    i1 = first_argmax(cand)
    i2 = first_argmax([jnp.where(i1 == k, -jnp.inf, cand[k]) for k in range(K)])
    v1 = sum(jnp.where(i1 == k, cand_sc[k], 0.0) for k in range(K))
    v2 = sum(jnp.where(i2 == k, cand_sc[k], 0.0) for k in range(K))
    den = v1 + v2
    eidx_ref[0:1, :] = g_sel * K + i1
    eidx_ref[1:2, :] = g_sel * K + i2
    gate_ref[0:1, :] = v1 / den
    gate_ref[1:2, :] = v2 / den

    member = jnp.concatenate(
        [jnp.where(in_grp & ((i1 == k) | (i2 == k)), 1.0, 0.0) for k in range(K)], axis=0)
    trow = lax.broadcasted_iota(I32, (tm, tm), 0)
    tcol = lax.broadcasted_iota(I32, (tm, tm), 1)
    before = (trow < tcol).astype(BF16)
    prefix = jnp.dot(member.astype(BF16), before, preferred_element_type=F32) + carry_sc[...]
    r1 = sum(jnp.sum(jnp.where(in_grp & (i1 == k), prefix[k * G:(k + 1) * G], 0.0), axis=0, keepdims=True)
             for k in range(K))
    r2 = sum(jnp.sum(jnp.where(in_grp & (i2 == k), prefix[k * G:(k + 1) * G], 0.0), axis=0, keepdims=True)
             for k in range(K))
    rank_ref[0:1, :] = r1.astype(I32)
    rank_ref[1:2, :] = r2.astype(I32)
    carry_sc[...] = carry_sc[...] + jnp.sum(member, axis=1, keepdims=True)
    cnt_ref[...] = jnp.broadcast_to(carry_sc[...], cnt_ref.shape).astype(I32)


def _slot_of_expert(e):
    return (e % EXPERTS_PER_GROUP) * N_EXPERT_GROUPS + e // EXPERTS_PER_GROUP


def _expert_of_slot(r):
    return (r % N_EXPERT_GROUPS) * EXPERTS_PER_GROUP + r // N_EXPERT_GROUPS


def _route(x, router_w, router_bias):
    t_tokens, d = x.shape
    tm = ROUTER_ROWS
    slot_expert = _expert_of_slot(jnp.arange(N_EXPERTS))
    rw = router_w.T[slot_expert]
    rb = router_bias[slot_expert].reshape(N_EXPERTS, 1)
    tok = pl.BlockSpec((TOP_K, tm), lambda i: (0, i))
    return pl.pallas_call(
        _router_kernel,
        grid=(t_tokens // tm,),
        in_specs=[
            pl.BlockSpec((tm, d), lambda i: (i, 0)),
            pl.BlockSpec((N_EXPERTS, d), lambda i: (0, 0)),
            pl.BlockSpec((N_EXPERTS, 1), lambda i: (0, 0)),
        ],
        out_specs=[tok, tok, tok, pl.BlockSpec((N_EXPERTS, LANES), lambda i: (0, 0))],
        out_shape=[
            jax.ShapeDtypeStruct((TOP_K, t_tokens), I32),
            jax.ShapeDtypeStruct((TOP_K, t_tokens), F32),
            jax.ShapeDtypeStruct((TOP_K, t_tokens), I32),
            jax.ShapeDtypeStruct((N_EXPERTS, LANES), I32),
        ],
        scratch_shapes=[pltpu.VMEM((N_EXPERTS, 1), F32)],
        compiler_params=_params("arbitrary"),
        name="router",
    )(x, rw, rb)


def _row_copy(src_hbm, row, dst, dst_row, sem):
    return pltpu.make_async_copy(src_hbm.at[pl.ds(row, 1), :], dst.at[pl.ds(dst_row, 1), :], sem)


def _dispatch_kernel(nused_ref, src_ref, x_hbm, zero_hbm, o_ref, buf, sem):
    blk = pl.program_id(0)
    rows = buf.shape[0]

    @pl.when(blk < nused_ref[0])
    def _():
        def start(i, carry):
            tok = src_ref[0, i]

            @pl.when(tok >= 0)
            def _():
                _row_copy(x_hbm, tok, buf, i, sem).start()

            @pl.when(tok < 0)
            def _():
                _row_copy(zero_hbm, 0, buf, i, sem).start()

            return carry

        lax.fori_loop(0, rows, start, 0)

        def wait(i, carry):
            _row_copy(zero_hbm, 0, buf, i, sem).wait()
            return carry

        lax.fori_loop(0, rows, wait, 0)
        o_ref[...] = buf[...].astype(o_ref.dtype)


def _dispatch(x, src, n_used, n_blocks):
    t_tokens, d = x.shape
    rows = MOE_ROWS
    zero = jnp.zeros((SUBLANES, d), F32)
    return pl.pallas_call(
        _dispatch_kernel,
        grid_spec=pltpu.PrefetchScalarGridSpec(
            num_scalar_prefetch=1,
            grid=(n_blocks,),
            in_specs=[
                pl.BlockSpec((None, 1, rows), lambda i, nu: (jnp.minimum(i, nu[0] - 1), 0, 0),
                             memory_space=pltpu.SMEM),
                pl.BlockSpec(memory_space=pl.ANY),
                pl.BlockSpec(memory_space=pl.ANY),
            ],
            out_specs=pl.BlockSpec((rows, d), lambda i, nu: (jnp.minimum(i, nu[0] - 1), 0)),
            scratch_shapes=[pltpu.VMEM((rows, d), F32), pltpu.SemaphoreType.DMA(())],
        ),
        out_shape=jax.ShapeDtypeStruct((n_blocks * rows, d), BF16),
        compiler_params=_params("arbitrary"),
        name="moe_dispatch",
    )(n_used, src.reshape(n_blocks, 1, rows), x, zero)


def _ffn_kernel(nused_ref, bexp_ref, xs_ref, wg_ref, wu_ref, wd_ref, o_ref, hg_sc, hu_sc, hid_sc):
    blk = pl.program_id(0)
    step = pl.program_id(1)

    @pl.when(blk < nused_ref[0])
    def _():
        @pl.when(step < MOE_K_STEPS)
        def _():
            xk = xs_ref[...]
            g = jnp.dot(xk, wg_ref[...].astype(BF16), preferred_element_type=F32)
            u = jnp.dot(xk, wu_ref[...].astype(BF16), preferred_element_type=F32)

            @pl.when(step == 0)
            def _():
                hg_sc[...] = g
                hu_sc[...] = u

            @pl.when(step > 0)
            def _():
                hg_sc[...] = hg_sc[...] + g
                hu_sc[...] = hu_sc[...] + u

            @pl.when(step == MOE_K_STEPS - 1)
            def _():
                hid_sc[...] = (_silu(hg_sc[...]) * hu_sc[...]).astype(BF16)

        @pl.when(step >= MOE_K_STEPS)
        def _():
            o_ref[...] = jnp.dot(hid_sc[...], wd_ref[...].astype(BF16), preferred_element_type=F32)


def _expert_ffn(xs, blk_expert, n_used, w_gate, w_up, w_down, layer):
    n_rows, d = xs.shape
    rows = MOE_ROWS
    n_blocks = n_rows // rows
    kc = d // MOE_K_STEPS
    oc = d // MOE_O_STEPS
    last = MOE_K_STEPS + MOE_O_STEPS - 1

    def pos(i, s, nu):
        used = i < nu[0]
        return jnp.where(used, i, nu[0] - 1), jnp.where(used, s, last)

    def k_idx(s):
        return jnp.minimum(s, MOE_K_STEPS - 1)

    def o_idx(s):
        return jnp.maximum(s - MOE_K_STEPS, 0)

    def xs_map(i, s, nu, be):
        ii, ss = pos(i, s, nu)
        return ii, k_idx(ss)

    def w_in_map(i, s, nu, be):
        ii, ss = pos(i, s, nu)
        return layer, be[ii], k_idx(ss), 0

    def w_down_map(i, s, nu, be):
        ii, ss = pos(i, s, nu)
        return layer, be[ii], 0, o_idx(ss)

    def out_map(i, s, nu, be):
        ii, ss = pos(i, s, nu)
        return ii, o_idx(ss)

    return pl.pallas_call(
        _ffn_kernel,
        grid_spec=pltpu.PrefetchScalarGridSpec(
            num_scalar_prefetch=2,
            grid=(n_blocks, MOE_K_STEPS + MOE_O_STEPS),
            in_specs=[
                pl.BlockSpec((rows, kc), xs_map),
                pl.BlockSpec((None, None, kc, D_FF), w_in_map),
                pl.BlockSpec((None, None, kc, D_FF), w_in_map),
                pl.BlockSpec((None, None, D_FF, oc), w_down_map),
            ],
            out_specs=pl.BlockSpec((rows, oc), out_map),
            scratch_shapes=[
                pltpu.VMEM((rows, D_FF), F32),
                pltpu.VMEM((rows, D_FF), F32),
                pltpu.VMEM((rows, D_FF), BF16),
            ],
        ),
        out_shape=jax.ShapeDtypeStruct((n_rows, d), F32),
        compiler_params=_params("arbitrary", "arbitrary"),
        name="moe_ffn",
    )(n_used, blk_expert, xs, w_gate, w_up, w_down)


def _combine_kernel(dest_ref, ys_hbm, gate_ref, x_ref, g_ref, b_ref, o_ref, obf_ref, buf, sem):
    tm = x_ref.shape[0]

    def start(i, carry):
        for k in range(TOP_K):
            _row_copy(ys_hbm, dest_ref[0, k, i], buf.at[k], i, sem).start()
        return carry

    lax.fori_loop(0, tm, start, 0)

    def wait(i, carry):
        for k in range(TOP_K):
            _row_copy(ys_hbm, 0, buf.at[k], i, sem).wait()
        return carry

    lax.fori_loop(0, tm, wait, 0)
    gates = gate_ref[...]
    ffn = gates[:, 0:1] * buf[0] + gates[:, 1:2] * buf[1]
    y = _layer_norm_rows(ALPHA * x_ref[...] + ffn, g_ref[...], b_ref[...])
    o_ref[...] = y
    obf_ref[...] = y.astype(BF16)


def _combine_layer_norm(x, ys, dest, gate, g, b):
    t_tokens, d = x.shape
    tm = LN_ROWS
    n_steps = t_tokens // tm
    dest_blocks = dest.reshape(TOP_K, n_steps, tm).transpose(1, 0, 2)
    row = pl.BlockSpec((tm, d), lambda i: (i, 0))
    vec = pl.BlockSpec((1, d), lambda i: (0, 0))
    return pl.pallas_call(
        _combine_kernel,
        grid=(n_steps,),
        in_specs=[
            pl.BlockSpec((1, TOP_K, tm), lambda i: (i, 0, 0), memory_space=pltpu.SMEM),
            pl.BlockSpec(memory_space=pl.ANY),
            pl.BlockSpec((tm, TOP_K), lambda i: (i, 0)),
            row, vec, vec,
        ],
        out_specs=[row, row],
        out_shape=[jax.ShapeDtypeStruct((t_tokens, d), F32), jax.ShapeDtypeStruct((t_tokens, d), BF16)],
        scratch_shapes=[pltpu.VMEM((TOP_K, tm, d), F32), pltpu.SemaphoreType.DMA(())],
        compiler_params=_params("arbitrary"),
        name="moe_combine",
    )(dest_blocks, ys, gate.T, x, g.reshape(1, d), b.reshape(1, d))


def _moe_layer_norm(x, layer, router_w, router_bias, w_gate, w_up, w_down, ln_g, ln_b):
    t_tokens, _ = x.shape
    rows = MOE_ROWS
    eidx, gate, rank, counts = _route(x, router_w, router_bias)
    cnt = counts[:, 0]
    padded = (cnt + rows - 1) // rows * rows
    pad_end = jnp.cumsum(padded)
    pad_start = pad_end - padded
    dest = pad_start[_slot_of_expert(eidx)] + rank
    n_blocks = (t_tokens * TOP_K) // rows + N_EXPERTS
    n_used = (pad_end[-1] // rows).astype(I32).reshape(1)
    tok = jnp.broadcast_to(jnp.arange(t_tokens, dtype=I32), (TOP_K, t_tokens))
    src = jnp.full((n_blocks * rows,), -1, I32).at[dest.reshape(-1)].set(tok.reshape(-1))
    blk_slot = jnp.minimum(jnp.sum(jnp.arange(n_blocks)[:, None] * rows >= pad_end[None, :], axis=1),
                           N_EXPERTS - 1)
    blk_expert = _expert_of_slot(blk_slot).astype(I32)

    xs = _dispatch(x, src, n_used, n_blocks)
    ys = _expert_ffn(xs, blk_expert, n_used, w_gate, w_up, w_down, layer)
    return _combine_layer_norm(x, ys, dest, gate, ln_g, ln_b)


def kernel(x, rel_bias, even_w_in, even_w_out, diff_lambda, diff_subln_g, odd_w_in, odd_w_out, ssd_conv_w, ssd_conv_b, ssd_dt_bias, ssd_a_log, ssd_d, ssd_norm_g, gmlp_ln_g, gmlp_ln_b, gmlp_ws, gmlp_bs, router_w, router_bias, moe_w_gate, moe_w_up, moe_w_down, ln_mix_g, ln_mix_b, ln_ffn_g, ln_ffn_b):
    bsz, s, d = x.shape
    t_tokens = bsz * s
    xf = x.reshape(t_tokens, d)
    xb = xf.astype(BF16)
    moe_args = (router_w, router_bias, moe_w_gate, moe_w_up, moe_w_down)

    lambda_init = 0.8 - 0.6 * math.exp(-0.3 * 0)
    h = _project([xb], even_w_in[0], [0], 0, EVEN_IN, BF16, tm=1024, tn=512)
    attn = _even_attention(h.reshape(bsz, s, EVEN_IN), _bias_tiles(rel_bias), diff_lambda[0],
                           diff_subln_g[0], lambda_init)
    mix = _project([attn.reshape(t_tokens, d)], even_w_out[0], [0], 0, d, F32, tm=1024, tn=512)
    xf, xb = _residual_layer_norm(xf, mix, ln_mix_g[0], ln_mix_b[0])
    xf, xb = _moe_layer_norm(xf, 0, *moe_args, ln_ffn_g[0], ln_ffn_b[0])

    zx_cols = SSD_D_INNER + SSD_CONV_DIM
    h_zx = _project([xb], odd_w_in[0], [0], 0, zx_cols, F32, tm=1024, tn=512)
    h_dt = _project([xb], odd_w_in[0], [0], zx_cols // LANES, LANES, F32, tm=1024, tn=LANES)
    h_uv = _project([xb], odd_w_in[0][:, C_IN:], [0], 0, 2 * GMLP_WIDTH, F32, tm=1024, tn=512)
    y_ssd = _ssd(h_zx, h_dt, bsz, ssd_conv_w[0], ssd_conv_b[0], ssd_dt_bias[0], ssd_a_log[0], ssd_d[0],
                 ssd_norm_g[0])
    y_gmlp = _gmlp(h_uv, gmlp_ln_g[0], gmlp_ln_b[0], gmlp_ws[0], gmlp_bs[0])
    mix = _project([y_ssd, y_gmlp], odd_w_out[0], [0, SSD_D_INNER // GMLP_WIDTH], 0, d, F32, tm=512, tn=512)
    xf, xb = _residual_layer_norm(xf, mix, ln_mix_g[1], ln_mix_b[1])
    xf, xb = _moe_layer_norm(xf, 1, *moe_args, ln_ffn_g[1], ln_ffn_b[1])
    return xf.reshape(bsz, s, d)
```

```python
import functools
import math

import numpy as np
import jax
import jax.numpy as jnp
from jax import lax
from jax.experimental import pallas as pl
from jax.experimental.pallas import tpu as pltpu

F32 = jnp.float32
BF16 = jnp.bfloat16
I32 = jnp.int32

D_MODEL = 4096
DEPTH = 2
HEAD_DIM = 128
N_HEADS_A = 16
N_HEADS_B = 16
DIFF_HALF = HEAD_DIM // 2
N_BUCKETS = 32
MAX_DISTANCE = 128
A_IN = 3 * N_HEADS_A * HEAD_DIM
EVEN_IN = A_IN + 3 * N_HEADS_B * HEAD_DIM
DIFF_SCALE = DIFF_HALF ** -0.5
SB_SCALE = HEAD_DIM ** -0.5

SSD_HEAD_DIM = 64
SSD_D_INNER = D_MODEL
SSD_N_HEADS = SSD_D_INNER // SSD_HEAD_DIM
SSD_N_GROUPS = 8
SSD_HEADS_PER_GROUP = SSD_N_HEADS // SSD_N_GROUPS
SSD_D_STATE = 128
SSD_CONV = 4
SSD_CHUNK = 128
SSD_GN = SSD_N_GROUPS * SSD_D_STATE
SSD_CONV_DIM = SSD_D_INNER + 2 * SSD_GN
SSD_GROUP_DIM = SSD_D_INNER // SSD_N_GROUPS
C_IN = SSD_D_INNER + SSD_CONV_DIM + SSD_N_HEADS
GMLP_WIDTH = D_MODEL // 2
GMLP_GROUPS = 8
GMLP_GROUP_DIM = GMLP_WIDTH // GMLP_GROUPS
GMLP_CHUNK = 128

N_EXPERTS = 32
N_EXPERT_GROUPS = 8
EXPERTS_PER_GROUP = N_EXPERTS // N_EXPERT_GROUPS
TOP_K = 2
D_FF = 768

ALPHA = (2 * DEPTH) ** 0.25
LN_EPS = 1e-5

LANES = 128
SUBLANES = 8
VMEM_LIMIT_BYTES = 58 * 1024 * 1024

ATTN_TILE = 256
ATTN_HEADS_PER_STEP = 4
ATTN_ONES_ROWS = 16
MOE_ROWS = 768
MOE_SUB_ROWS = 256
MOE_K_STEPS = 4
MOE_K_CHUNK = D_MODEL // MOE_K_STEPS
MOE_O_STEPS = 2
U32 = jnp.uint32
HIGH_HALF = np.uint32(0xFFFF0000)
ROUTER_ROWS = 512
LN_ROWS = 256


def _params(*sem):
    return pltpu.CompilerParams(dimension_semantics=sem, vmem_limit_bytes=VMEM_LIMIT_BYTES)


def _proj_kernel(*refs, n_in):
    x_refs = refs[:n_in]
    w_refs = refs[n_in:2 * n_in]
    o_ref = refs[2 * n_in]
    wbf_refs = refs[2 * n_in + 1:]

    @pl.when(pl.program_id(1) == 0)
    def _():
        for w_ref, wbf_ref in zip(w_refs, wbf_refs):
            wbf_ref[...] = w_ref[...].astype(BF16)

    acc = jnp.dot(x_refs[0][...], wbf_refs[0][...], preferred_element_type=F32)
    for x_ref, wbf_ref in zip(x_refs[1:], wbf_refs[1:]):
        acc = acc + jnp.dot(x_ref[...], wbf_ref[...], preferred_element_type=F32)
    o_ref[...] = acc.astype(o_ref.dtype)


def _project(xs, w, row_blocks, col_block0, n_cols, out_dtype, tm, tn):
    m = xs[0].shape[0]
    n_in = len(xs)
    in_specs = [pl.BlockSpec((tm, x.shape[1]), lambda j, i: (i, 0)) for x in xs]
    for x, rb in zip(xs, row_blocks):
        in_specs.append(pl.BlockSpec((x.shape[1], tn), lambda j, i, rb=rb: (rb, j + col_block0)))
    return pl.pallas_call(
        functools.partial(_proj_kernel, n_in=n_in),
        grid=(n_cols // tn, m // tm),
        in_specs=in_specs,
        out_specs=pl.BlockSpec((tm, tn), lambda j, i: (i, j)),
        out_shape=jax.ShapeDtypeStruct((m, n_cols), out_dtype),
        scratch_shapes=[pltpu.VMEM((x.shape[1], tn), BF16) for x in xs],
        compiler_params=_params("arbitrary", "arbitrary"),
        name="proj",
    )(*xs, *([w] * n_in))


def _layer_norm_rows(y, g, b):
    mu = jnp.mean(y, axis=-1, keepdims=True)
    yc = y - mu
    var = jnp.mean(yc * yc, axis=-1, keepdims=True)
    return yc * lax.rsqrt(var + LN_EPS) * g + b


def _pack_bf16_pair(lo, hi):
    lo_bits = lax.bitcast_convert_type(lo.astype(BF16).astype(F32), U32)
    hi_bits = lax.bitcast_convert_type(hi.astype(BF16).astype(F32), U32)
    return (lo_bits >> 16) | (hi_bits & HIGH_HALF)


def _unpack_bf16_pair(p):
    return (lax.bitcast_convert_type(p << 16, F32), lax.bitcast_convert_type(p & HIGH_HALF, F32))


def _pack_rows(y):
    half = MOE_K_CHUNK // 2
    parts = [_pack_bf16_pair(y[:, c * MOE_K_CHUNK:c * MOE_K_CHUNK + half],
                             y[:, c * MOE_K_CHUNK + half:(c + 1) * MOE_K_CHUNK])
             for c in range(y.shape[1] // MOE_K_CHUNK)]
    return jnp.concatenate(parts, axis=1)


def _res_ln_kernel(x_ref, mix_ref, g_ref, b_ref, o_ref, opk_ref):
    y = _layer_norm_rows(ALPHA * x_ref[...] + mix_ref[...], g_ref[...], b_ref[...])
    o_ref[...] = y
    opk_ref[...] = _pack_rows(y)


def _residual_layer_norm(x, mix, g, b):
    m, d = x.shape
    tm = LN_ROWS
    row = pl.BlockSpec((tm, d), lambda i: (i, 0))
    half_row = pl.BlockSpec((tm, d // 2), lambda i: (i, 0))
    vec = pl.BlockSpec((1, d), lambda i: (0, 0))
    return pl.pallas_call(
        _res_ln_kernel,
        grid=(m // tm,),
        in_specs=[row, row, vec, vec],
        out_specs=[row, half_row],
        out_shape=[jax.ShapeDtypeStruct((m, d), F32), jax.ShapeDtypeStruct((m, d // 2), U32)],
        compiler_params=_params("arbitrary"),
        name="res_ln",
    )(x, mix, g.reshape(1, d), b.reshape(1, d))


def _bias_tile_kernel(rb_ref, o_ref):
    head = pl.program_id(0)
    t = ATTN_TILE
    key = lax.broadcasted_iota(I32, (t, t), 0)
    qry = lax.broadcasted_iota(I32, (t, t), 1)
    max_exact = N_BUCKETS // 2
    for d in range(2):
        n = jnp.maximum(d * t + qry - key, 0)
        nf = jnp.maximum(n, 1).astype(F32)
        large = max_exact + (jnp.log(nf / max_exact) / math.log(MAX_DISTANCE / max_exact)
                             * (N_BUCKETS - max_exact)).astype(I32)
        large = jnp.minimum(large, N_BUCKETS - 1)
        bucket = jnp.where(n < max_exact, n, large)
        acc = jnp.zeros((t, t), F32)
        for bkt in range(N_BUCKETS):
            acc = jnp.where(bucket == bkt, rb_ref[bkt, head], acc)
        o_ref[d] = acc


def _bias_tiles(rel_bias):
    t = ATTN_TILE
    return pl.pallas_call(
        _bias_tile_kernel,
        grid=(N_HEADS_A,),
        in_specs=[pl.BlockSpec(memory_space=pltpu.SMEM)],
        out_specs=pl.BlockSpec((None, 2, t, t), lambda h: (h, 0, 0, 0)),
        out_shape=jax.ShapeDtypeStruct((N_HEADS_A, 2, t, t), F32),
        compiler_params=_params("arbitrary"),
        name="bias_tiles",
    )(rel_bias)


def _dot_nt(a, b):
    return lax.dot_general(a, b, (((1,), (1,)), ((), ())), preferred_element_type=F32)


def _head_cols(h):
    return slice(h * HEAD_DIM, (h + 1) * HEAD_DIM)


def _value_transposes(v_ref, vt_sc):
    t = ATTN_TILE
    ones = jnp.ones((ATTN_ONES_ROWS, t), BF16)
    for h in range(ATTN_HEADS_PER_STEP):
        for j in range(v_ref.shape[0] // t):
            vb = v_ref[j * t:(j + 1) * t, _head_cols(h)]
            vt_sc[h, j, 0:HEAD_DIM, :] = vb.astype(F32).T.astype(BF16)
            vt_sc[h, j, HEAD_DIM:, :] = ones


def _diff_attention(dl_ref, g_ref, q_ref, k_ref, vt_sc, bias_ref, o_ref, m_sc, acc_sc, lambda_init):
    t = ATTN_TILE
    qi = pl.program_id(2)
    lane = lax.broadcasted_iota(I32, (t, HEAD_DIM), 1)
    q_maps = []
    for h in range(ATTN_HEADS_PER_STEP):
        q = q_ref[:, _head_cols(h)] * DIFF_SCALE
        zero = jnp.zeros_like(q)
        q_maps.append((jnp.where(lane < DIFF_HALF, q, zero), jnp.where(lane >= DIFF_HALF, q, zero)))

    chains = [(h, mp) for h in range(ATTN_HEADS_PER_STEP) for mp in range(2)]

    def block(j, bias_of_head, mask, first):
        kv_start = pl.multiple_of(j * t, t)
        kbs = [k_ref[pl.ds(kv_start, t), _head_cols(h)] for h in range(ATTN_HEADS_PER_STEP)]
        scores = [_dot_nt(kbs[h], q_maps[h][mp]) for h, mp in chains]
        probs, alphas = [], []
        for (h, mp), s in zip(chains, scores):
            s = s + bias_of_head(h)
            if mask is not None:
                s = jnp.where(mask, s, -jnp.inf)
            m_new = jnp.max(s, axis=0, keepdims=True)
            if not first:
                m_old = m_sc[h, mp]
                m_new = jnp.maximum(m_old, m_new)
                alphas.append(jnp.exp(m_old - m_new))
            m_sc[h, mp] = m_new
            probs.append(jnp.exp(s - m_new).astype(BF16))
        for i, (h, mp) in enumerate(chains):
            pv = jnp.dot(vt_sc[h, j], probs[i], preferred_element_type=F32)
            acc_sc[h, mp] = pv if first else alphas[i] * acc_sc[h, mp] + pv

    key = lax.broadcasted_iota(I32, (t, t), 0)
    qry = lax.broadcasted_iota(I32, (t, t), 1)
    block(qi, lambda h: bias_ref[h, 0], qry >= key, True)

    @pl.when(qi >= 1)
    def _():
        block(qi - 1, lambda h: bias_ref[h, 1], None, False)

    def far_body(j, carry):
        block(j, lambda h: bias_ref[h, 1, 0:1, t - 1:t], None, False)
        return carry

    lax.fori_loop(0, qi - 1, far_body, 0)

    dl = dl_ref[...]
    lam = (jnp.exp(jnp.sum(dl[0:1] * dl[1:2], axis=-1, keepdims=True))
           - jnp.exp(jnp.sum(dl[2:3] * dl[3:4], axis=-1, keepdims=True)) + lambda_init)
    for h in range(ATTN_HEADS_PER_STEP):
        a0 = acc_sc[h, 0]
        a1 = acc_sc[h, 1]
        oa = (a0[0:HEAD_DIM] / a0[HEAD_DIM:HEAD_DIM + 1]
              - lam * (a1[0:HEAD_DIM] / a1[HEAD_DIM:HEAD_DIM + 1]))
        ms = jnp.mean(oa * oa, axis=0, keepdims=True)
        oa = (oa * lax.rsqrt(ms + LN_EPS)).T * g_ref[...] * (1.0 - lambda_init)
        o_ref[:, _head_cols(h)] = oa.astype(o_ref.dtype)


def _log_sigmoid(z):
    return jnp.minimum(z, 0.0) - jnp.log(1.0 + jnp.exp(-jnp.abs(z)))


def _split_bf16(x):
    hi = x.astype(BF16)
    lo = (x - hi.astype(F32)).astype(BF16)
    return hi, lo


def _stick_breaking_attention(q_ref, k_ref, vt_sc, o_ref, c_sc, acc_sc):
    t = ATTN_TILE
    qi = pl.program_id(2)
    key = lax.broadcasted_iota(I32, (t, t), 0)
    qry = lax.broadcasted_iota(I32, (t, t), 1)
    later = qry > key
    after = later.astype(BF16)

    heads = range(ATTN_HEADS_PER_STEP)

    def block(j, strict, first):
        kv_start = pl.multiple_of(j * t, t)
        zs = [_dot_nt(k_ref[pl.ds(kv_start, t), _head_cols(h)], q_ref[:, _head_cols(h)]) for h in heads]
        log_betas, splits, col_sums = [], [], []
        for h in heads:
            z = zs[h] * SB_SCALE
            log_beta = _log_sigmoid(z)
            log_1mb = log_beta - z
            if strict is not None:
                log_1mb = jnp.where(strict, log_1mb, 0.0)
            log_betas.append(log_beta)
            splits.append(_split_bf16(log_1mb))
            col_sums.append(jnp.sum(log_1mb, axis=0, keepdims=True))
        tails = [jnp.dot(after, hi, preferred_element_type=F32) + jnp.dot(after, lo, preferred_element_type=F32)
                 for hi, lo in splits]
        weights = []
        for h in heads:
            tail = tails[h] if first else tails[h] + c_sc[h]
            w = jnp.exp(log_betas[h] + tail)
            if strict is not None:
                w = jnp.where(strict, w, 0.0)
            weights.append(w.astype(BF16))
            c_sc[h] = col_sums[h] if first else c_sc[h] + col_sums[h]
        for h in heads:
            pv = jnp.dot(vt_sc[h, j][0:HEAD_DIM], weights[h], preferred_element_type=F32)
            acc_sc[h, 0, 0:HEAD_DIM] = pv if first else acc_sc[h, 0, 0:HEAD_DIM] + pv

    block(qi, later, True)

    def body(step, carry):
        block(qi - 1 - step, None, False)
        return carry

    lax.fori_loop(0, qi, body, 0)
    for h in range(ATTN_HEADS_PER_STEP):
        o_ref[:, _head_cols(h)] = acc_sc[h, 0, 0:HEAD_DIM].T.astype(o_ref.dtype)


def _attn_kernel(dl_ref, g_ref, q_ref, k_ref, v_ref, bias_ref, o_ref, vt_sc, m_sc, c_sc, acc_sc, *, lambda_init):
    grp = pl.program_id(1)

    @pl.when(pl.program_id(2) == 0)
    def _():
        _value_transposes(v_ref, vt_sc)

    @pl.when(grp < N_HEADS_A // ATTN_HEADS_PER_STEP)
    def _():
        _diff_attention(dl_ref, g_ref, q_ref, k_ref, vt_sc, bias_ref, o_ref, m_sc, acc_sc, lambda_init)

    @pl.when(grp >= N_HEADS_A // ATTN_HEADS_PER_STEP)
    def _():
        _stick_breaking_attention(q_ref, k_ref, vt_sc, o_ref, c_sc, acc_sc)


def _even_attention(h, bias_tiles, diff_lambda, subln_g, lambda_init):
    bsz, s, _ = h.shape
    t = ATTN_TILE
    nh = ATTN_HEADS_PER_STEP
    width = nh * HEAD_DIM
    groups_a = N_HEADS_A // nh
    n_groups = (N_HEADS_A + N_HEADS_B) // nh

    def q_col(g):
        return g + jnp.where(g >= groups_a, 2 * groups_a, 0)

    return pl.pallas_call(
        functools.partial(_attn_kernel, lambda_init=lambda_init),
        grid=(bsz, n_groups, s // t),
        in_specs=[
            pl.BlockSpec((4, DIFF_HALF), lambda b, g, qi: (0, 0)),
            pl.BlockSpec((1, HEAD_DIM), lambda b, g, qi: (0, 0)),
            pl.BlockSpec((None, t, width), lambda b, g, qi: (b, qi, q_col(g))),
            pl.BlockSpec((None, s, width), lambda b, g, qi: (b, 0, q_col(g) + groups_a)),
            pl.BlockSpec((None, s, width), lambda b, g, qi: (b, 0, q_col(g) + 2 * groups_a)),
            pl.BlockSpec((nh, 2, t, t), lambda b, g, qi: (jnp.minimum(g, groups_a - 1), 0, 0, 0)),
        ],
        out_specs=pl.BlockSpec((None, t, width), lambda b, g, qi: (b, qi, g)),
        out_shape=jax.ShapeDtypeStruct((bsz, s, n_groups * width), BF16),
        scratch_shapes=[
            pltpu.VMEM((nh, s // t, HEAD_DIM + ATTN_ONES_ROWS, t), BF16),
            pltpu.VMEM((nh, 2, 1, t), F32),
            pltpu.VMEM((nh, 1, t), F32),
            pltpu.VMEM((nh, 2, HEAD_DIM + ATTN_ONES_ROWS, t), F32),
        ],
        compiler_params=_params("arbitrary", "arbitrary", "arbitrary"),
        name="even_attention",
    )(diff_lambda, subln_g.reshape(1, HEAD_DIM), h, h, h, bias_tiles)


def _silu(x):
    return x * jax.nn.sigmoid(x)


def _softplus(x):
    return jnp.maximum(x, 0.0) + jnp.log(1.0 + jnp.exp(-jnp.abs(x)))


def _ssd_kernel(z_ref, x_ref, b_ref, c_ref, dt_ref, cw_ref, cb_ref, dtb_ref, alog_ref, dskip_ref, ng_ref,
                o_ref, pad_sc, state_sc, acst_sc, dtt_sc):
    grp = pl.program_id(1)
    chunk = pl.program_id(2)
    L = SSD_CHUNK
    P2 = 2 * SSD_HEAD_DIM
    n_conv = SSD_GROUP_DIM + 2 * SSD_D_STATE
    halo = SUBLANES

    @pl.when(chunk == 0)
    def _():
        pad_sc[0:halo, :] = jnp.zeros((halo, n_conv), F32)
        state_sc[...] = jnp.zeros_like(state_sc)

    pad_sc[halo:halo + L, 0:SSD_GROUP_DIM] = x_ref[...]
    pad_sc[halo:halo + L, SSD_GROUP_DIM:SSD_GROUP_DIM + SSD_D_STATE] = b_ref[...]
    pad_sc[halo:halo + L, SSD_GROUP_DIM + SSD_D_STATE:n_conv] = c_ref[...]
    conv = cb_ref[...]
    for j in range(SSD_CONV):
        start = halo - (SSD_CONV - 1) + j
        conv = conv + cw_ref[j:j + 1, :] * pad_sc[start:start + L, :]
    pad_sc[0:halo, :] = pad_sc[L:L + halo, :]
    xbc = _silu(conv)
    xg = xbc[:, 0:SSD_GROUP_DIM]
    bg = xbc[:, SSD_GROUP_DIM:SSD_GROUP_DIM + SSD_D_STATE]
    cg = xbc[:, SSD_GROUP_DIM + SSD_D_STATE:n_conv]
    xg_bf = xg.astype(BF16)

    dt = _softplus(dt_ref[...] + dtb_ref[...])
    adt = dt * (-jnp.exp(alog_ref[...]))
    row = lax.broadcasted_iota(I32, (L, L), 0)
    col = lax.broadcasted_iota(I32, (L, L), 1)
    causal = row >= col
    acs = jnp.dot(causal.astype(F32), adt, preferred_element_type=F32, precision=lax.Precision.HIGHEST)
    acst_sc[...] = acs.T
    dtt_sc[...] = dt.T

    cb = _dot_nt(cg.astype(BF16), bg.astype(BF16))
    bgt = bg.T
    cg_bf = cg.astype(BF16)
    prev = state_sc[...]
    y_off = jnp.dot(cg_bf, prev.astype(BF16), preferred_element_type=F32)

    lane = lax.broadcasted_iota(I32, (L, P2), 1)
    first_head = lane < SSD_HEAD_DIM
    lane_row = lax.broadcasted_iota(I32, (1, P2), 1) < SSD_HEAD_DIM
    y_pairs = []
    for pair in range(SSD_HEADS_PER_GROUP // 2):
        x_pair = xg_bf[:, pair * P2:(pair + 1) * P2]
        y_diag, st, e_acs, decay = [], [], [], []
        for sub in range(2):
            head = grp * SSD_HEADS_PER_GROUP + 2 * pair + sub
            acs_row = acst_sc[pl.ds(head, 1), :]
            dt_row = dtt_sc[pl.ds(head, 1), :]
            row_b = jnp.broadcast_to(acs_row, (L, L))
            col_b = row_b.T
            decay_m = jnp.exp(jnp.where(causal, col_b - row_b, -jnp.inf))
            m = (cb * decay_m * dt_row).astype(BF16)
            y_diag.append(jnp.dot(m, x_pair, preferred_element_type=F32))
            acs_last = col_b[L - 1:L, :]
            w_row = jnp.exp(acs_last - acs_row) * dt_row
            st.append(jnp.dot((bgt * w_row).astype(BF16), x_pair, preferred_element_type=F32))
            e_acs.append(jnp.exp(col_b))
            decay.append(jnp.exp(acs_last))
        cols = slice(pair * P2, (pair + 1) * P2)
        y_pair = (jnp.where(first_head, y_diag[0], y_diag[1])
                  + jnp.where(first_head, e_acs[0], e_acs[1]) * y_off[:, cols])
        state_sc[:, cols] = (prev[:, cols] * jnp.where(lane_row, decay[0], decay[1])
                             + jnp.where(first_head, st[0], st[1]))
        y_pair = y_pair + dskip_ref[:, cols] * xg[:, cols]
        y_pairs.append(y_pair * _silu(z_ref[:, cols]))

    y = jnp.concatenate(y_pairs, axis=-1)
    ms = jnp.mean(y * y, axis=-1, keepdims=True)
    o_ref[...] = (y * lax.rsqrt(ms + LN_EPS) * ng_ref[...]).astype(o_ref.dtype)


def _ssd(h_zx, h_dt, bsz, conv_w, conv_b, dt_bias, a_log, d_skip, norm_g):
    t_tokens = h_zx.shape[0]
    n_chunks = t_tokens // bsz // SSD_CHUNK
    L = SSD_CHUNK
    gd, ns, G = SSD_GROUP_DIM, SSD_D_STATE, SSD_N_GROUPS
    n_conv = gd + 2 * ns

    def pack(p):
        xs = p[..., :SSD_D_INNER].reshape(p.shape[:-1] + (G, gd))
        bs_ = p[..., SSD_D_INNER:SSD_D_INNER + SSD_GN].reshape(p.shape[:-1] + (G, ns))
        cs = p[..., SSD_D_INNER + SSD_GN:].reshape(p.shape[:-1] + (G, ns))
        return jnp.moveaxis(jnp.concatenate([xs, bs_, cs], axis=-1), -2, 0)

    cw = pack(conv_w)
    cb = pack(conv_b.reshape(1, -1))
    pad_heads = LANES - SSD_N_HEADS
    dtb = jnp.pad(dt_bias, (0, pad_heads)).reshape(1, LANES)
    alog = jnp.pad(a_log, (0, pad_heads)).reshape(1, LANES)
    dskip = jnp.repeat(d_skip, SSD_HEAD_DIM).reshape(1, SSD_D_INNER)

    def rows(b, g, c):
        return b * n_chunks + c

    x_blk0 = SSD_D_INNER // gd
    b_blk0 = (2 * SSD_D_INNER) // ns
    c_blk0 = (2 * SSD_D_INNER + SSD_GN) // ns
    return pl.pallas_call(
        _ssd_kernel,
        grid=(bsz, G, n_chunks),
        in_specs=[
            pl.BlockSpec((L, gd), lambda b, g, c: (rows(b, g, c), g)),
            pl.BlockSpec((L, gd), lambda b, g, c: (rows(b, g, c), x_blk0 + g)),
            pl.BlockSpec((L, ns), lambda b, g, c: (rows(b, g, c), b_blk0 + g)),
            pl.BlockSpec((L, ns), lambda b, g, c: (rows(b, g, c), c_blk0 + g)),
            pl.BlockSpec((L, LANES), lambda b, g, c: (rows(b, g, c), 0)),
            pl.BlockSpec((None, SSD_CONV, n_conv), lambda b, g, c: (g, 0, 0)),
            pl.BlockSpec((None, 1, n_conv), lambda b, g, c: (g, 0, 0)),
            pl.BlockSpec((1, LANES), lambda b, g, c: (0, 0)),
            pl.BlockSpec((1, LANES), lambda b, g, c: (0, 0)),
            pl.BlockSpec((1, gd), lambda b, g, c: (0, g)),
            pl.BlockSpec((1, gd), lambda b, g, c: (0, g)),
        ],
        out_specs=pl.BlockSpec((L, gd), lambda b, g, c: (rows(b, g, c), g)),
        out_shape=jax.ShapeDtypeStruct((t_tokens, SSD_D_INNER), BF16),
        scratch_shapes=[
            pltpu.VMEM((SUBLANES + L, n_conv), F32),
            pltpu.VMEM((ns, gd), F32),
            pltpu.VMEM((LANES, L), F32),
            pltpu.VMEM((LANES, L), F32),
        ],
        compiler_params=_params("arbitrary", "arbitrary", "arbitrary"),
        name="ssd",
    )(h_zx, h_zx, h_zx, h_zx, h_dt, cw, cb, dtb, alog, dskip, norm_g.reshape(1, SSD_D_INNER))


def _gelu_tanh(x):
    return 0.5 * x * (1.0 + jnp.tanh(math.sqrt(2.0 / math.pi) * (x + 0.044715 * (x * x * x))))


def _gmlp_kernel(uv_ref, lng_ref, lnb_ref, ws_ref, bs_ref, o_ref):
    L = GMLP_CHUNK
    gdim = GMLP_GROUP_DIM
    row = lax.broadcasted_iota(I32, (L, L), 0)
    col = lax.broadcasted_iota(I32, (L, L), 1)
    causal = row >= col
    for g in range(GMLP_GROUPS):
        cols = slice(g * gdim, (g + 1) * gdim)
        u = _gelu_tanh(uv_ref[:, cols])
        v = _gelu_tanh(uv_ref[:, GMLP_WIDTH + g * gdim:GMLP_WIDTH + (g + 1) * gdim])
        v = _layer_norm_rows(v, lng_ref[:, cols], lnb_ref[:, cols])
        ws = jnp.where(causal, ws_ref[g], 0.0).astype(BF16)
        sv = jnp.dot(ws, v.astype(BF16), preferred_element_type=F32) + bs_ref[:, cols]
        o_ref[:, cols] = (u * sv).astype(o_ref.dtype)


def _gmlp(h_uv, ln_g, ln_b, ws, bs):
    t_tokens = h_uv.shape[0]
    L = GMLP_CHUNK
    bs_cols = jnp.repeat(bs.T, GMLP_GROUP_DIM, axis=1)
    return pl.pallas_call(
        _gmlp_kernel,
        grid=(t_tokens // L,),
        in_specs=[
            pl.BlockSpec((L, 2 * GMLP_WIDTH), lambda i: (i, 0)),
            pl.BlockSpec((1, GMLP_WIDTH), lambda i: (0, 0)),
            pl.BlockSpec((1, GMLP_WIDTH), lambda i: (0, 0)),
            pl.BlockSpec((GMLP_GROUPS, L, L), lambda i: (0, 0, 0)),
            pl.BlockSpec((L, GMLP_WIDTH), lambda i: (0, 0)),
        ],
        out_specs=pl.BlockSpec((L, GMLP_WIDTH), lambda i: (i, 0)),
        out_shape=jax.ShapeDtypeStruct((t_tokens, GMLP_WIDTH), BF16),
        compiler_params=_params("arbitrary"),
        name="gmlp",
    )(h_uv, ln_g.reshape(1, GMLP_WIDTH), ln_b.reshape(1, GMLP_WIDTH), ws, bs_cols)


def _router_kernel(x_ref, rw_ref, rb_ref, eidx_ref, gate_ref, rank_ref, cnt_ref, carry_sc):
    tm = x_ref.shape[0]
    G, K = N_EXPERT_GROUPS, EXPERTS_PER_GROUP

    @pl.when(pl.program_id(0) == 0)
    def _():
        carry_sc[...] = jnp.zeros_like(carry_sc)

    logits = lax.dot_general(rw_ref[...], x_ref[...], (((1,), (1,)), ((), ())),
                             preferred_element_type=F32, precision=lax.Precision.HIGHEST)
    scores = jax.nn.sigmoid(logits)
    biased = scores + rb_ref[...]
    a = [biased[k * G:(k + 1) * G] for k in range(K)]
    sc = [scores[k * G:(k + 1) * G] for k in range(K)]
    hi01, lo01 = jnp.maximum(a[0], a[1]), jnp.minimum(a[0], a[1])
    hi23, lo23 = jnp.maximum(a[2], a[3]), jnp.minimum(a[2], a[3])
    grp_score = jnp.maximum(hi01, hi23) + jnp.maximum(jnp.minimum(hi01, hi23), jnp.maximum(lo01, lo23))
    gidx = lax.broadcasted_iota(I32, (G, tm), 0)
    g_best = jnp.max(grp_score, axis=0, keepdims=True)
    g_sel = jnp.min(jnp.where(grp_score == g_best, gidx, G), axis=0, keepdims=True)
    in_grp = gidx == g_sel
    cand = [jnp.sum(jnp.where(in_grp, a[k], 0.0), axis=0, keepdims=True) for k in range(K)]
    cand_sc = [jnp.sum(jnp.where(in_grp, sc[k], 0.0), axis=0, keepdims=True) for k in range(K)]

    def first_argmax(vals):
        best, idx = vals[0], jnp.zeros((1, tm), I32)
        for k in range(1, K):
            gt = vals[k] > best
            best = jnp.where(gt, vals[k], best)
            idx = jnp.where(gt, k, idx)
        return idx

    i1 = first_argmax(cand)
    i2 = first_argmax([jnp.where(i1 == k, -jnp.inf, cand[k]) for k in range(K)])
    v1 = sum(jnp.where(i1 == k, cand_sc[k], 0.0) for k in range(K))
    v2 = sum(jnp.where(i2 == k, cand_sc[k], 0.0) for k in range(K))
    den = v1 + v2
    eidx_ref[0:1, :] = g_sel * K + i1
    eidx_ref[1:2, :] = g_sel * K + i2
    gate_ref[0:1, :] = v1 / den
    gate_ref[1:2, :] = v2 / den

    member = jnp.concatenate(
        [jnp.where(in_grp & ((i1 == k) | (i2 == k)), 1.0, 0.0) for k in range(K)], axis=0)
    trow = lax.broadcasted_iota(I32, (tm, tm), 0)
    tcol = lax.broadcasted_iota(I32, (tm, tm), 1)
    before = (trow < tcol).astype(BF16)
    prefix = jnp.dot(member.astype(BF16), before, preferred_element_type=F32) + carry_sc[...]
    r1 = sum(jnp.sum(jnp.where(in_grp & (i1 == k), prefix[k * G:(k + 1) * G], 0.0), axis=0, keepdims=True)
             for k in range(K))
    r2 = sum(jnp.sum(jnp.where(in_grp & (i2 == k), prefix[k * G:(k + 1) * G], 0.0), axis=0, keepdims=True)
             for k in range(K))
    rank_ref[0:1, :] = r1.astype(I32)
    rank_ref[1:2, :] = r2.astype(I32)
    carry_sc[...] = carry_sc[...] + jnp.sum(member, axis=1, keepdims=True)
    cnt_ref[...] = jnp.broadcast_to(carry_sc[...], cnt_ref.shape).astype(I32)


def _slot_of_expert(e):
    return (e % EXPERTS_PER_GROUP) * N_EXPERT_GROUPS + e // EXPERTS_PER_GROUP


def _expert_of_slot(r):
    return (r % N_EXPERT_GROUPS) * EXPERTS_PER_GROUP + r // N_EXPERT_GROUPS


def _route(x, router_w, router_bias):
    t_tokens, d = x.shape
    tm = ROUTER_ROWS
    slot_expert = _expert_of_slot(jnp.arange(N_EXPERTS))
    rw = router_w.T[slot_expert]
    rb = router_bias[slot_expert].reshape(N_EXPERTS, 1)
    tok = pl.BlockSpec((TOP_K, tm), lambda i: (0, i))
    return pl.pallas_call(
        _router_kernel,
        grid=(t_tokens // tm,),
        in_specs=[
            pl.BlockSpec((tm, d), lambda i: (i, 0)),
            pl.BlockSpec((N_EXPERTS, d), lambda i: (0, 0)),
            pl.BlockSpec((N_EXPERTS, 1), lambda i: (0, 0)),
        ],
        out_specs=[tok, tok, tok, pl.BlockSpec((N_EXPERTS, LANES), lambda i: (0, 0))],
        out_shape=[
            jax.ShapeDtypeStruct((TOP_K, t_tokens), I32),
            jax.ShapeDtypeStruct((TOP_K, t_tokens), F32),
            jax.ShapeDtypeStruct((TOP_K, t_tokens), I32),
            jax.ShapeDtypeStruct((N_EXPERTS, LANES), I32),
        ],
        scratch_shapes=[pltpu.VMEM((N_EXPERTS, 1), F32)],
        compiler_params=_params("arbitrary"),
        name="router",
    )(x, rw, rb)


def _ffn_kernel(nused_ref, iexp_ref, nval_ref, src_ref, src_next_ref, dst_ref, xp_hbm, wg_ref, wu_ref, wd_ref,
                yk_hbm, xbuf, hg_sc, hu_sc, hid_sc, obuf, gsem, ssem):
    item = pl.program_id(0)
    step = pl.program_id(1)
    n_used = nused_ref[0]
    slot = lax.rem(item, 2)
    sub = MOE_SUB_ROWS
    half = MOE_K_CHUNK // 2

    def gather_copy(src_rows, i, to_slot):
        return pltpu.make_async_copy(xp_hbm.at[pl.ds(src_rows[0, i], 1), :],
                                     xbuf.at[to_slot, pl.ds(i, 1), :], gsem.at[to_slot])

    def scatter_copy(i):
        return pltpu.make_async_copy(obuf.at[pl.ds(i, 1), :], yk_hbm.at[pl.ds(dst_ref[0, i], 1), :], ssem)

    def for_rows(count, fn):
        def body(i, carry):
            fn(i)
            return carry
        lax.fori_loop(0, count, body, 0)

    @pl.when(item < n_used)
    def _():
        n_rows = nval_ref[item]
        n_sub = (n_rows + sub - 1) // sub

        @pl.when(step == 0)
        def _():
            @pl.when(item == 0)
            def _():
                xbuf[...] = jnp.zeros_like(xbuf)
                for_rows(n_rows, lambda i: gather_copy(src_ref, i, 0).start())

            @pl.when(item + 1 < n_used)
            def _():
                for_rows(nval_ref[item + 1], lambda i: gather_copy(src_next_ref, i, 1 - slot).start())

            for_rows(n_rows, lambda i: gather_copy(src_ref, i, slot).wait())

        for c in range(MOE_K_STEPS):
            @pl.when(step == c)
            def _(c=c):
                wg = wg_ref[...].astype(BF16)
                wu = wu_ref[...].astype(BF16)

                def sub_block(r):
                    rows = pl.ds(pl.multiple_of(r * sub, sub), sub)
                    lo, hi = _unpack_bf16_pair(xbuf[slot, rows, c * half:(c + 1) * half])
                    xk = jnp.concatenate([lo, hi], axis=1).astype(BF16)
                    g = jnp.dot(xk, wg, preferred_element_type=F32)
                    u = jnp.dot(xk, wu, preferred_element_type=F32)
                    if c > 0:
                        g = g + hg_sc[rows, :]
                        u = u + hu_sc[rows, :]
                    if c < MOE_K_STEPS - 1:
                        hg_sc[rows, :] = g
                        hu_sc[rows, :] = u
                    else:
                        hid_sc[rows, :] = (_silu(g) * u).astype(BF16)

                for_rows(n_sub, sub_block)

        @pl.when(step == MOE_K_STEPS)
        def _():
            @pl.when(item > 0)
            def _():
                for_rows(nval_ref[item - 1], lambda i: scatter_copy(i).wait())

            wd = wd_ref[...].astype(BF16)

            def sub_block(r):
                rows = pl.ds(pl.multiple_of(r * sub, sub), sub)
                o = jnp.dot(hid_sc[rows, :], wd, preferred_element_type=F32)
                obuf[rows, :] = lax.bitcast_convert_type(o.astype(BF16).astype(F32), U32) >> 16

            for_rows(n_sub, sub_block)

        @pl.when(step == MOE_K_STEPS + 1)
        def _():
            wd = wd_ref[...].astype(BF16)

            def sub_block(r):
                rows = pl.ds(pl.multiple_of(r * sub, sub), sub)
                o = jnp.dot(hid_sc[rows, :], wd, preferred_element_type=F32)
                obuf[rows, :] = obuf[rows, :] | (lax.bitcast_convert_type(o.astype(BF16).astype(F32), U32)
                                                 & HIGH_HALF)

            for_rows(n_sub, sub_block)
            for_rows(n_rows, lambda i: scatter_copy(i).start())

            @pl.when(item == n_used - 1)
            def _():
                for_rows(n_rows, lambda i: scatter_copy(i).wait())


def _expert_ffn(xp, item_expert, item_rows, n_used, src, dst, w_gate, w_up, w_down, layer):
    t_tokens, dp = xp.shape
    d = 2 * dp
    rows = MOE_ROWS
    n_items = item_expert.shape[0]
    oc = d // MOE_O_STEPS
    last = MOE_K_STEPS + MOE_O_STEPS - 1

    def pos(i, s, nu):
        used = i < nu[0]
        return jnp.where(used, i, nu[0] - 1), jnp.where(used, s, last)

    def w_in_map(i, s, nu, ie, nv):
        ii, ss = pos(i, s, nu)
        return layer, ie[ii], jnp.minimum(ss, MOE_K_STEPS - 1), 0

    def w_down_map(i, s, nu, ie, nv):
        ii, ss = pos(i, s, nu)
        early = ss < MOE_K_STEPS
        return (layer, ie[jnp.where(early, jnp.maximum(ii - 1, 0), ii)], 0,
                jnp.where(early, MOE_O_STEPS - 1, ss - MOE_K_STEPS))

    def rows_map(i, s, nu, ie, nv):
        return pos(i, s, nu)[0], 0, 0

    def next_rows_map(i, s, nu, ie, nv):
        return jnp.minimum(pos(i, s, nu)[0] + 1, n_items - 1), 0, 0

    idx_block = (None, 1, rows)
    return pl.pallas_call(
        _ffn_kernel,
        grid_spec=pltpu.PrefetchScalarGridSpec(
            num_scalar_prefetch=3,
            grid=(n_items, MOE_K_STEPS + MOE_O_STEPS),
            in_specs=[
                pl.BlockSpec(idx_block, rows_map, memory_space=pltpu.SMEM),
                pl.BlockSpec(idx_block, next_rows_map, memory_space=pltpu.SMEM),
                pl.BlockSpec(idx_block, rows_map, memory_space=pltpu.SMEM),
                pl.BlockSpec(memory_space=pl.ANY),
                pl.BlockSpec((None, None, MOE_K_CHUNK, D_FF), w_in_map),
                pl.BlockSpec((None, None, MOE_K_CHUNK, D_FF), w_in_map),
                pl.BlockSpec((None, None, D_FF, oc), w_down_map),
            ],
            out_specs=pl.BlockSpec(memory_space=pl.ANY),
            scratch_shapes=[
                pltpu.VMEM((2, rows, dp), U32),
                pltpu.VMEM((rows, D_FF), F32),
                pltpu.VMEM((rows, D_FF), F32),
                pltpu.VMEM((rows, D_FF), BF16),
                pltpu.VMEM((rows, dp), U32),
                pltpu.SemaphoreType.DMA((2,)),
                pltpu.SemaphoreType.DMA(()),
            ],
        ),
        out_shape=jax.ShapeDtypeStruct((TOP_K * t_tokens, dp), U32),
        compiler_params=_params("arbitrary", "arbitrary"),
        name="moe_ffn",
    )(n_used, item_expert, item_rows, src.reshape(n_items, 1, rows), src.reshape(n_items, 1, rows),
      dst.reshape(n_items, 1, rows), xp, w_gate, w_up, w_down)


def _combine_kernel(y0_ref, y1_ref, gate_ref, x_ref, g_ref, b_ref, o_ref, obf_ref):
    gates = gate_ref[...]
    ffn = (gates[:, 0:1] * jnp.concatenate(_unpack_bf16_pair(y0_ref[...]), axis=1)
           + gates[:, 1:2] * jnp.concatenate(_unpack_bf16_pair(y1_ref[...]), axis=1))
    y = _layer_norm_rows(ALPHA * x_ref[...] + ffn, g_ref[...], b_ref[...])
    o_ref[...] = y
    obf_ref[...] = y.astype(BF16)


def _combine_layer_norm(x, yk, gate, g, b):
    t_tokens, d = x.shape
    tm = LN_ROWS
    n_steps = t_tokens // tm
    row = pl.BlockSpec((tm, d), lambda i: (i, 0))
    vec = pl.BlockSpec((1, d), lambda i: (0, 0))
    return pl.pallas_call(
        _combine_kernel,
        grid=(n_steps,),
        in_specs=[
            pl.BlockSpec((tm, d // 2), lambda i: (i, 0)),
            pl.BlockSpec((tm, d // 2), lambda i: (i + n_steps, 0)),
            pl.BlockSpec((tm, TOP_K), lambda i: (i, 0)),
            row, vec, vec,
        ],
        out_specs=[row, row],
        out_shape=[jax.ShapeDtypeStruct((t_tokens, d), F32), jax.ShapeDtypeStruct((t_tokens, d), BF16)],
        compiler_params=_params("arbitrary"),
        name="moe_combine",
    )(yk, yk, gate.T, x, g.reshape(1, d), b.reshape(1, d))


def _moe_layer_norm(x, xp, layer, router_w, router_bias, w_gate, w_up, w_down, ln_g, ln_b):
    t_tokens, _ = x.shape
    rows = MOE_ROWS
    eidx, gate, rank, counts = _route(x, router_w, router_bias)
    cnt = counts[:, 0]
    items_per_slot = (cnt + rows - 1) // rows
    item_end = jnp.cumsum(items_per_slot)
    row_start = (item_end - items_per_slot) * rows
    dest = (row_start[_slot_of_expert(eidx)] + rank).reshape(-1)
    n_items = (t_tokens * TOP_K) // rows + N_EXPERTS
    n_used = item_end[-1].astype(I32).reshape(1)
    item_id = jnp.arange(n_items, dtype=I32)
    item_slot = jnp.minimum(jnp.sum(item_id[:, None] >= item_end[None, :], axis=1), N_EXPERTS - 1)
    item_expert = _expert_of_slot(item_slot).astype(I32)
    item_rows = jnp.clip(row_start[item_slot] + cnt[item_slot] - item_id * rows, 0, rows)
    item_rows = jnp.where(item_id < n_used[0], item_rows, 0).astype(I32)
    tok = jnp.broadcast_to(jnp.arange(t_tokens, dtype=I32), (TOP_K, t_tokens)).reshape(-1)
    out_row = jnp.arange(TOP_K * t_tokens, dtype=I32)
    src = jnp.zeros((n_items * rows,), I32).at[dest].set(tok)
    dst = jnp.zeros((n_items * rows,), I32).at[dest].set(out_row)

    yk = _expert_ffn(xp, item_expert, item_rows, n_used, src, dst, w_gate, w_up, w_down, layer)
    return _combine_layer_norm(x, yk, gate, ln_g, ln_b)


def kernel(x, rel_bias, even_w_in, even_w_out, diff_lambda, diff_subln_g, odd_w_in, odd_w_out, ssd_conv_w, ssd_conv_b, ssd_dt_bias, ssd_a_log, ssd_d, ssd_norm_g, gmlp_ln_g, gmlp_ln_b, gmlp_ws, gmlp_bs, router_w, router_bias, moe_w_gate, moe_w_up, moe_w_down, ln_mix_g, ln_mix_b, ln_ffn_g, ln_ffn_b):
    bsz, s, d = x.shape
    t_tokens = bsz * s
    xf = x.reshape(t_tokens, d)
    xb = xf.astype(BF16)
    moe_args = (router_w, router_bias, moe_w_gate, moe_w_up, moe_w_down)

    lambda_init = 0.8 - 0.6 * math.exp(-0.3 * 0)
    h = _project([xb], even_w_in[0], [0], 0, EVEN_IN, BF16, tm=1024, tn=512)
    attn = _even_attention(h.reshape(bsz, s, EVEN_IN), _bias_tiles(rel_bias), diff_lambda[0],
                           diff_subln_g[0], lambda_init)
    mix = _project([attn.reshape(t_tokens, d)], even_w_out[0], [0], 0, d, F32, tm=1024, tn=512)
    xf, xp = _residual_layer_norm(xf, mix, ln_mix_g[0], ln_mix_b[0])
    xf, xb = _moe_layer_norm(xf, xp, 0, *moe_args, ln_ffn_g[0], ln_ffn_b[0])

    zx_cols = SSD_D_INNER + SSD_CONV_DIM
    h_zx = _project([xb], odd_w_in[0], [0], 0, zx_cols, F32, tm=1024, tn=512)
    h_dt = _project([xb], odd_w_in[0], [0], zx_cols // LANES, LANES, F32, tm=1024, tn=LANES)
    h_uv = _project([xb], odd_w_in[0][:, C_IN:], [0], 0, 2 * GMLP_WIDTH, F32, tm=1024, tn=512)
    y_ssd = _ssd(h_zx, h_dt, bsz, ssd_conv_w[0], ssd_conv_b[0], ssd_dt_bias[0], ssd_a_log[0], ssd_d[0],
                 ssd_norm_g[0])
    y_gmlp = _gmlp(h_uv, gmlp_ln_g[0], gmlp_ln_b[0], gmlp_ws[0], gmlp_bs[0])
    mix = _project([y_ssd, y_gmlp], odd_w_out[0], [0, SSD_D_INNER // GMLP_WIDTH], 0, d, F32, tm=512, tn=512)
    xf, xp = _residual_layer_norm(xf, mix, ln_mix_g[1], ln_mix_b[1])
    xf, xb = _moe_layer_norm(xf, xp, 1, *moe_args, ln_ffn_g[1], ln_ffn_b[1])
    return xf.reshape(bsz, s, d)
```

```python
import functools
import math

import numpy as np
import jax
import jax.numpy as jnp
from jax import lax
from jax.experimental import pallas as pl
from jax.experimental.pallas import tpu as pltpu

F32 = jnp.float32
BF16 = jnp.bfloat16
I32 = jnp.int32

D_MODEL = 4096
DEPTH = 2
HEAD_DIM = 128
N_HEADS_A = 16
N_HEADS_B = 16
DIFF_HALF = HEAD_DIM // 2
N_BUCKETS = 32
MAX_DISTANCE = 128
A_IN = 3 * N_HEADS_A * HEAD_DIM
EVEN_IN = A_IN + 3 * N_HEADS_B * HEAD_DIM
DIFF_SCALE = DIFF_HALF ** -0.5
SB_SCALE = HEAD_DIM ** -0.5

SSD_HEAD_DIM = 64
SSD_D_INNER = D_MODEL
SSD_N_HEADS = SSD_D_INNER // SSD_HEAD_DIM
SSD_N_GROUPS = 8
SSD_HEADS_PER_GROUP = SSD_N_HEADS // SSD_N_GROUPS
SSD_D_STATE = 128
SSD_CONV = 4
SSD_CHUNK = 128
SSD_GN = SSD_N_GROUPS * SSD_D_STATE
SSD_CONV_DIM = SSD_D_INNER + 2 * SSD_GN
SSD_GROUP_DIM = SSD_D_INNER // SSD_N_GROUPS
C_IN = SSD_D_INNER + SSD_CONV_DIM + SSD_N_HEADS
GMLP_WIDTH = D_MODEL // 2
GMLP_GROUPS = 8
GMLP_GROUP_DIM = GMLP_WIDTH // GMLP_GROUPS
GMLP_CHUNK = 128

N_EXPERTS = 32
N_EXPERT_GROUPS = 8
EXPERTS_PER_GROUP = N_EXPERTS // N_EXPERT_GROUPS
TOP_K = 2
D_FF = 768

ALPHA = (2 * DEPTH) ** 0.25
LN_EPS = 1e-5

LANES = 128
SUBLANES = 8
VMEM_LIMIT_BYTES = 58 * 1024 * 1024

ATTN_TILE = 256
ATTN_HEADS_PER_STEP = 4
ATTN_ONES_ROWS = 16
MOE_ROWS = 768
MOE_SUB_ROWS = 256
MOE_K_STEPS = 4
MOE_K_CHUNK = D_MODEL // MOE_K_STEPS
MOE_O_STEPS = 2
MOE_DMA_UNROLL = 8
U32 = jnp.uint32
HIGH_HALF = np.uint32(0xFFFF0000)
ROUTER_ROWS = 512
LN_ROWS = 256


def _params(*sem):
    return pltpu.CompilerParams(dimension_semantics=sem, vmem_limit_bytes=VMEM_LIMIT_BYTES)


def _proj_kernel(*refs, n_in, w_transposed):
    x_refs = refs[:n_in]
    w_refs = refs[n_in:2 * n_in]
    o_ref = refs[2 * n_in]
    wbf_refs = refs[2 * n_in + 1:]

    @pl.when(pl.program_id(1) == 0)
    def _():
        for w_ref, wbf_ref in zip(w_refs, wbf_refs):
            wbf_ref[...] = w_ref[...].astype(BF16)

    mm = _dot_nt if w_transposed else functools.partial(jnp.dot, preferred_element_type=F32)
    acc = mm(x_refs[0][...], wbf_refs[0][...])
    for x_ref, wbf_ref in zip(x_refs[1:], wbf_refs[1:]):
        acc = acc + mm(x_ref[...], wbf_ref[...])
    o_ref[...] = acc.astype(o_ref.dtype)


def _project(xs, w, row_blocks, col0, n_cols, out_dtype, tm, tn, w_transposed=False):
    m = xs[0].shape[0]
    n_in = len(xs)
    in_specs = [pl.BlockSpec((tm, x.shape[1]), lambda j, i: (i, 0)) for x in xs]
    for x, rb in zip(xs, row_blocks):
        k = x.shape[1]
        if not w_transposed:
            in_specs.append(pl.BlockSpec((k, tn), lambda j, i, rb=rb: (rb, j + col0 // tn)))
        elif col0 % tn == 0:
            in_specs.append(pl.BlockSpec((tn, k), lambda j, i, rb=rb: (j + col0 // tn, rb)))
        else:
            in_specs.append(pl.BlockSpec((pl.Element(tn), pl.Element(k)),
                                         lambda j, i, rb=rb, k=k: (pl.multiple_of(col0 + j * tn, SUBLANES),
                                                                   rb * k)))
    return pl.pallas_call(
        functools.partial(_proj_kernel, n_in=n_in, w_transposed=w_transposed),
        grid=(n_cols // tn, m // tm),
        in_specs=in_specs,
        out_specs=pl.BlockSpec((tm, tn), lambda j, i: (i, j)),
        out_shape=jax.ShapeDtypeStruct((m, n_cols), out_dtype),
        scratch_shapes=[pltpu.VMEM((tn, x.shape[1]) if w_transposed else (x.shape[1], tn), BF16) for x in xs],
        compiler_params=_params("arbitrary", "arbitrary"),
        name="proj",
    )(*xs, *([w] * n_in))


def _layer_norm_rows(y, g, b):
    mu = jnp.mean(y, axis=-1, keepdims=True)
    yc = y - mu
    var = jnp.mean(yc * yc, axis=-1, keepdims=True)
    return yc * lax.rsqrt(var + LN_EPS) * g + b


def _pack_bf16_pair(lo, hi):
    lo_bits = lax.bitcast_convert_type(lo.astype(BF16).astype(F32), U32)
    hi_bits = lax.bitcast_convert_type(hi.astype(BF16).astype(F32), U32)
    return (lo_bits >> 16) | (hi_bits & HIGH_HALF)


def _unpack_bf16_pair(p):
    return (lax.bitcast_convert_type(p << 16, F32), lax.bitcast_convert_type(p & HIGH_HALF, F32))


def _pack_rows(y):
    half = MOE_K_CHUNK // 2
    parts = [_pack_bf16_pair(y[:, c * MOE_K_CHUNK:c * MOE_K_CHUNK + half],
                             y[:, c * MOE_K_CHUNK + half:(c + 1) * MOE_K_CHUNK])
             for c in range(y.shape[1] // MOE_K_CHUNK)]
    return jnp.concatenate(parts, axis=1)


def _res_ln_kernel(x_ref, mix_ref, g_ref, b_ref, o_ref, opk_ref):
    y = _layer_norm_rows(ALPHA * x_ref[...] + mix_ref[...], g_ref[...], b_ref[...])
    o_ref[...] = y
    opk_ref[...] = _pack_rows(y)


def _residual_layer_norm(x, mix, g, b):
    m, d = x.shape
    tm = LN_ROWS
    row = pl.BlockSpec((tm, d), lambda i: (i, 0))
    half_row = pl.BlockSpec((tm, d // 2), lambda i: (i, 0))
    vec = pl.BlockSpec((1, d), lambda i: (0, 0))
    return pl.pallas_call(
        _res_ln_kernel,
        grid=(m // tm,),
        in_specs=[row, row, vec, vec],
        out_specs=[row, half_row],
        out_shape=[jax.ShapeDtypeStruct((m, d), F32), jax.ShapeDtypeStruct((m, d // 2), U32)],
        compiler_params=_params("arbitrary"),
        name="res_ln",
    )(x, mix, g.reshape(1, d), b.reshape(1, d))


def _bias_tile_kernel(rb_ref, o_ref):
    head = pl.program_id(0)
    t = ATTN_TILE
    key = lax.broadcasted_iota(I32, (t, t), 0)
    qry = lax.broadcasted_iota(I32, (t, t), 1)
    max_exact = N_BUCKETS // 2
    for d in range(2):
        n = jnp.maximum(d * t + qry - key, 0)
        nf = jnp.maximum(n, 1).astype(F32)
        large = max_exact + (jnp.log(nf / max_exact) / math.log(MAX_DISTANCE / max_exact)
                             * (N_BUCKETS - max_exact)).astype(I32)
        large = jnp.minimum(large, N_BUCKETS - 1)
        bucket = jnp.where(n < max_exact, n, large)
        acc = jnp.zeros((t, t), F32)
        for bkt in range(N_BUCKETS):
            acc = jnp.where(bucket == bkt, rb_ref[bkt, head], acc)
        o_ref[d] = acc


def _bias_tiles(rel_bias):
    t = ATTN_TILE
    return pl.pallas_call(
        _bias_tile_kernel,
        grid=(N_HEADS_A,),
        in_specs=[pl.BlockSpec(memory_space=pltpu.SMEM)],
        out_specs=pl.BlockSpec((None, 2, t, t), lambda h: (h, 0, 0, 0)),
        out_shape=jax.ShapeDtypeStruct((N_HEADS_A, 2, t, t), F32),
        compiler_params=_params("arbitrary"),
        name="bias_tiles",
    )(rel_bias)


def _dot_nt(a, b):
    return lax.dot_general(a, b, (((1,), (1,)), ((), ())), preferred_element_type=F32)


def _head_cols(h):
    return slice(h * HEAD_DIM, (h + 1) * HEAD_DIM)


def _value_transposes(v_ref, vt_sc):
    t = ATTN_TILE
    ones = jnp.ones((ATTN_ONES_ROWS, t), BF16)
    for h in range(ATTN_HEADS_PER_STEP):
        for j in range(v_ref.shape[0] // t):
            vb = v_ref[j * t:(j + 1) * t, _head_cols(h)]
            vt_sc[h, j, 0:HEAD_DIM, :] = vb.astype(F32).T.astype(BF16)
            vt_sc[h, j, HEAD_DIM:, :] = ones


def _diff_attention(dl_ref, g_ref, q_ref, k_ref, vt_sc, bias_ref, o_ref, m_sc, acc_sc, lambda_init):
    t = ATTN_TILE
    qi = pl.program_id(2)
    lane = lax.broadcasted_iota(I32, (t, HEAD_DIM), 1)
    q_maps = []
    for h in range(ATTN_HEADS_PER_STEP):
        q = q_ref[:, _head_cols(h)] * DIFF_SCALE
        zero = jnp.zeros_like(q)
        q_maps.append((jnp.where(lane < DIFF_HALF, q, zero), jnp.where(lane >= DIFF_HALF, q, zero)))

    chains = [(h, mp) for h in range(ATTN_HEADS_PER_STEP) for mp in range(2)]

    def block(j, bias_of_head, mask, first):
        kv_start = pl.multiple_of(j * t, t)
        kbs = [k_ref[pl.ds(kv_start, t), _head_cols(h)] for h in range(ATTN_HEADS_PER_STEP)]
        scores = [_dot_nt(kbs[h], q_maps[h][mp]) for h, mp in chains]
        probs, alphas = [], []
        for (h, mp), s in zip(chains, scores):
            bias = bias_of_head(h)
            uniform = bias.shape == (1, 1)
            if not uniform:
                s = s + bias
            if mask is not None:
                s = jnp.where(mask, s, -jnp.inf)
            m_new = jnp.max(s, axis=0, keepdims=True)
            if uniform:
                m_new = m_new + bias
            if not first:
                m_old = m_sc[h, mp]
                m_new = jnp.maximum(m_old, m_new)
                alphas.append(jnp.exp(m_old - m_new))
            m_sc[h, mp] = m_new
            probs.append(jnp.exp(s - (m_new - bias if uniform else m_new)).astype(BF16))
        for i, (h, mp) in enumerate(chains):
            pv = jnp.dot(vt_sc[h, j], probs[i], preferred_element_type=F32)
            acc_sc[h, mp] = pv if first else alphas[i] * acc_sc[h, mp] + pv

    key = lax.broadcasted_iota(I32, (t, t), 0)
    qry = lax.broadcasted_iota(I32, (t, t), 1)
    block(qi, lambda h: bias_ref[h, 0], qry >= key, True)

    @pl.when(qi >= 1)
    def _():
        block(qi - 1, lambda h: bias_ref[h, 1], None, False)

    def far_body(j, carry):
        block(j, lambda h: bias_ref[h, 1, 0:1, t - 1:t], None, False)
        return carry

    lax.fori_loop(0, qi - 1, far_body, 0)

    dl = dl_ref[...]
    lam = (jnp.exp(jnp.sum(dl[0:1] * dl[1:2], axis=-1, keepdims=True))
           - jnp.exp(jnp.sum(dl[2:3] * dl[3:4], axis=-1, keepdims=True)) + lambda_init)
    for h in range(ATTN_HEADS_PER_STEP):
        a0 = acc_sc[h, 0]
        a1 = acc_sc[h, 1]
        oa = (a0[0:HEAD_DIM] / a0[HEAD_DIM:HEAD_DIM + 1]
              - lam * (a1[0:HEAD_DIM] / a1[HEAD_DIM:HEAD_DIM + 1]))
        ms = jnp.mean(oa * oa, axis=0, keepdims=True)
        oa = (oa * lax.rsqrt(ms + LN_EPS)).T * g_ref[...] * (1.0 - lambda_init)
        o_ref[:, _head_cols(h)] = oa.astype(o_ref.dtype)


def _log_sigmoid(z):
    return jnp.minimum(z, 0.0) - jnp.log(1.0 + jnp.exp(-jnp.abs(z)))


def _split_bf16(x):
    hi = x.astype(BF16)
    lo = (x - hi.astype(F32)).astype(BF16)
    return hi, lo


def _stick_breaking_attention(q_ref, k_ref, vt_sc, o_ref, c_sc, acc_sc):
    t = ATTN_TILE
    qi = pl.program_id(2)
    key = lax.broadcasted_iota(I32, (t, t), 0)
    qry = lax.broadcasted_iota(I32, (t, t), 1)
    later = qry > key
    after = later.astype(BF16)

    heads = range(ATTN_HEADS_PER_STEP)

    def block(j, strict, first):
        kv_start = pl.multiple_of(j * t, t)
        zs = [_dot_nt(k_ref[pl.ds(kv_start, t), _head_cols(h)], q_ref[:, _head_cols(h)]) for h in heads]
        log_betas, splits, col_sums = [], [], []
        for h in heads:
            z = zs[h] * SB_SCALE
            log_beta = _log_sigmoid(z)
            log_1mb = log_beta - z
            if strict is not None:
                log_1mb = jnp.where(strict, log_1mb, 0.0)
            log_betas.append(log_beta)
            splits.append(_split_bf16(log_1mb))
            col_sums.append(jnp.sum(log_1mb, axis=0, keepdims=True))
        tails = [jnp.dot(after, hi, preferred_element_type=F32) + jnp.dot(after, lo, preferred_element_type=F32)
                 for hi, lo in splits]
        weights = []
        for h in heads:
            tail = tails[h] if first else tails[h] + c_sc[h]
            w = jnp.exp(log_betas[h] + tail)
            if strict is not None:
                w = jnp.where(strict, w, 0.0)
            weights.append(w.astype(BF16))
            c_sc[h] = col_sums[h] if first else c_sc[h] + col_sums[h]
        for h in heads:
            pv = jnp.dot(vt_sc[h, j][0:HEAD_DIM], weights[h], preferred_element_type=F32)
            acc_sc[h, 0, 0:HEAD_DIM] = pv if first else acc_sc[h, 0, 0:HEAD_DIM] + pv

    block(qi, later, True)

    def body(step, carry):
        block(qi - 1 - step, None, False)
        return carry

    lax.fori_loop(0, qi, body, 0)
    for h in range(ATTN_HEADS_PER_STEP):
        o_ref[:, _head_cols(h)] = acc_sc[h, 0, 0:HEAD_DIM].T.astype(o_ref.dtype)


def _attn_kernel(dl_ref, g_ref, q_ref, k_ref, v_ref, bias_ref, o_ref, vt_sc, m_sc, c_sc, acc_sc, *, lambda_init):
    grp = pl.program_id(1)

    @pl.when(pl.program_id(2) == 0)
    def _():
        _value_transposes(v_ref, vt_sc)

    @pl.when(grp < N_HEADS_A // ATTN_HEADS_PER_STEP)
    def _():
        _diff_attention(dl_ref, g_ref, q_ref, k_ref, vt_sc, bias_ref, o_ref, m_sc, acc_sc, lambda_init)

    @pl.when(grp >= N_HEADS_A // ATTN_HEADS_PER_STEP)
    def _():
        _stick_breaking_attention(q_ref, k_ref, vt_sc, o_ref, c_sc, acc_sc)


def _even_attention(h, bias_tiles, diff_lambda, subln_g, lambda_init):
    bsz, s, _ = h.shape
    t = ATTN_TILE
    nh = ATTN_HEADS_PER_STEP
    width = nh * HEAD_DIM
    groups_a = N_HEADS_A // nh
    n_groups = (N_HEADS_A + N_HEADS_B) // nh

    def q_col(g):
        return g + jnp.where(g >= groups_a, 2 * groups_a, 0)

    return pl.pallas_call(
        functools.partial(_attn_kernel, lambda_init=lambda_init),
        grid=(bsz, n_groups, s // t),
        in_specs=[
            pl.BlockSpec((4, DIFF_HALF), lambda b, g, qi: (0, 0)),
            pl.BlockSpec((1, HEAD_DIM), lambda b, g, qi: (0, 0)),
            pl.BlockSpec((None, t, width), lambda b, g, qi: (b, qi, q_col(g))),
            pl.BlockSpec((None, s, width), lambda b, g, qi: (b, 0, q_col(g) + groups_a)),
            pl.BlockSpec((None, s, width), lambda b, g, qi: (b, 0, q_col(g) + 2 * groups_a)),
            pl.BlockSpec((nh, 2, t, t), lambda b, g, qi: (jnp.minimum(g, groups_a - 1), 0, 0, 0)),
        ],
        out_specs=pl.BlockSpec((None, t, width), lambda b, g, qi: (b, qi, g)),
        out_shape=jax.ShapeDtypeStruct((bsz, s, n_groups * width), BF16),
        scratch_shapes=[
            pltpu.VMEM((nh, s // t, HEAD_DIM + ATTN_ONES_ROWS, t), BF16),
            pltpu.VMEM((nh, 2, 1, t), F32),
            pltpu.VMEM((nh, 1, t), F32),
            pltpu.VMEM((nh, 2, HEAD_DIM + ATTN_ONES_ROWS, t), F32),
        ],
        compiler_params=_params("arbitrary", "arbitrary", "arbitrary"),
        name="even_attention",
    )(diff_lambda, subln_g.reshape(1, HEAD_DIM), h, h, h, bias_tiles)


def _silu(x):
    return x * jax.nn.sigmoid(x)


def _softplus(x):
    return jnp.maximum(x, 0.0) + jnp.log(1.0 + jnp.exp(-jnp.abs(x)))


def _ssd_kernel(z_ref, x_ref, b_ref, c_ref, dt_ref, cw_ref, cb_ref, dtb_ref, alog_ref, dskip_ref, ng_ref,
                o_ref, pad_sc, state_sc, acst_sc, dtt_sc):
    grp = pl.program_id(1)
    chunk = pl.program_id(2)
    L = SSD_CHUNK
    P2 = 2 * SSD_HEAD_DIM
    n_conv = SSD_GROUP_DIM + 2 * SSD_D_STATE
    halo = SUBLANES

    @pl.when(chunk == 0)
    def _():
        pad_sc[0:halo, :] = jnp.zeros((halo, n_conv), F32)
        state_sc[...] = jnp.zeros_like(state_sc)

    pad_sc[halo:halo + L, 0:SSD_GROUP_DIM] = x_ref[...]
    pad_sc[halo:halo + L, SSD_GROUP_DIM:SSD_GROUP_DIM + SSD_D_STATE] = b_ref[...]
    pad_sc[halo:halo + L, SSD_GROUP_DIM + SSD_D_STATE:n_conv] = c_ref[...]
    conv = cb_ref[...]
    for j in range(SSD_CONV):
        start = halo - (SSD_CONV - 1) + j
        conv = conv + cw_ref[j:j + 1, :] * pad_sc[start:start + L, :]
    pad_sc[0:halo, :] = pad_sc[L:L + halo, :]
    xbc = _silu(conv)
    xg = xbc[:, 0:SSD_GROUP_DIM]
    bg = xbc[:, SSD_GROUP_DIM:SSD_GROUP_DIM + SSD_D_STATE]
    cg = xbc[:, SSD_GROUP_DIM + SSD_D_STATE:n_conv]
    xg_bf = xg.astype(BF16)

    dt = _softplus(dt_ref[...] + dtb_ref[...])
    adt = dt * (-jnp.exp(alog_ref[...]))
    row = lax.broadcasted_iota(I32, (L, L), 0)
    col = lax.broadcasted_iota(I32, (L, L), 1)
    causal = row >= col
    acs = jnp.dot(causal.astype(F32), adt, preferred_element_type=F32, precision=lax.Precision.HIGHEST)
    acst_sc[...] = acs.T
    dtt_sc[...] = dt.T

    cb = _dot_nt(cg.astype(BF16), bg.astype(BF16))
    bgt = bg.T
    cg_bf = cg.astype(BF16)
    prev = state_sc[...]
    y_off = jnp.dot(cg_bf, prev.astype(BF16), preferred_element_type=F32)

    lane = lax.broadcasted_iota(I32, (L, P2), 1)
    first_head = lane < SSD_HEAD_DIM
    lane_row = lax.broadcasted_iota(I32, (1, P2), 1) < SSD_HEAD_DIM
    y_pairs = []
    for pair in range(SSD_HEADS_PER_GROUP // 2):
        x_pair = xg_bf[:, pair * P2:(pair + 1) * P2]
        y_diag, st, e_acs, decay = [], [], [], []
        for sub in range(2):
            head = grp * SSD_HEADS_PER_GROUP + 2 * pair + sub
            acs_row = acst_sc[pl.ds(head, 1), :]
            dt_row = dtt_sc[pl.ds(head, 1), :]
            row_b = jnp.broadcast_to(acs_row, (L, L))
            col_b = row_b.T
            decay_m = jnp.exp(jnp.where(causal, col_b - row_b, -jnp.inf))
            m = (cb * decay_m * dt_row).astype(BF16)
            y_diag.append(jnp.dot(m, x_pair, preferred_element_type=F32))
            acs_last = col_b[L - 1:L, :]
            w_row = jnp.exp(acs_last - acs_row) * dt_row
            st.append(jnp.dot((bgt * w_row).astype(BF16), x_pair, preferred_element_type=F32))
            e_acs.append(jnp.exp(col_b))
            decay.append(jnp.exp(acs_last))
        cols = slice(pair * P2, (pair + 1) * P2)
        y_pair = (jnp.where(first_head, y_diag[0], y_diag[1])
                  + jnp.where(first_head, e_acs[0], e_acs[1]) * y_off[:, cols])
        state_sc[:, cols] = (prev[:, cols] * jnp.where(lane_row, decay[0], decay[1])
                             + jnp.where(first_head, st[0], st[1]))
        y_pair = y_pair + dskip_ref[:, cols] * xg[:, cols]
        y_pairs.append(y_pair * _silu(z_ref[:, cols]))

    y = jnp.concatenate(y_pairs, axis=-1)
    ms = jnp.mean(y * y, axis=-1, keepdims=True)
    o_ref[...] = (y * lax.rsqrt(ms + LN_EPS) * ng_ref[...]).astype(o_ref.dtype)


def _ssd(h_zx, h_dt, bsz, conv_w, conv_b, dt_bias, a_log, d_skip, norm_g):
    t_tokens = h_zx.shape[0]
    n_chunks = t_tokens // bsz // SSD_CHUNK
    L = SSD_CHUNK
    gd, ns, G = SSD_GROUP_DIM, SSD_D_STATE, SSD_N_GROUPS
    n_conv = gd + 2 * ns

    def pack(p):
        xs = p[..., :SSD_D_INNER].reshape(p.shape[:-1] + (G, gd))
        bs_ = p[..., SSD_D_INNER:SSD_D_INNER + SSD_GN].reshape(p.shape[:-1] + (G, ns))
        cs = p[..., SSD_D_INNER + SSD_GN:].reshape(p.shape[:-1] + (G, ns))
        return jnp.moveaxis(jnp.concatenate([xs, bs_, cs], axis=-1), -2, 0)

    cw = pack(conv_w)
    cb = pack(conv_b.reshape(1, -1))
    pad_heads = LANES - SSD_N_HEADS
    dtb = jnp.pad(dt_bias, (0, pad_heads)).reshape(1, LANES)
    alog = jnp.pad(a_log, (0, pad_heads)).reshape(1, LANES)
    dskip = jnp.repeat(d_skip, SSD_HEAD_DIM).reshape(1, SSD_D_INNER)

    def rows(b, g, c):
        return b * n_chunks + c

    x_blk0 = SSD_D_INNER // gd
    b_blk0 = (2 * SSD_D_INNER) // ns
    c_blk0 = (2 * SSD_D_INNER + SSD_GN) // ns
    return pl.pallas_call(
        _ssd_kernel,
        grid=(bsz, G, n_chunks),
        in_specs=[
            pl.BlockSpec((L, gd), lambda b, g, c: (rows(b, g, c), g)),
            pl.BlockSpec((L, gd), lambda b, g, c: (rows(b, g, c), x_blk0 + g)),
            pl.BlockSpec((L, ns), lambda b, g, c: (rows(b, g, c), b_blk0 + g)),
            pl.BlockSpec((L, ns), lambda b, g, c: (rows(b, g, c), c_blk0 + g)),
            pl.BlockSpec((L, LANES), lambda b, g, c: (rows(b, g, c), 0)),
            pl.BlockSpec((None, SSD_CONV, n_conv), lambda b, g, c: (g, 0, 0)),
            pl.BlockSpec((None, 1, n_conv), lambda b, g, c: (g, 0, 0)),
            pl.BlockSpec((1, LANES), lambda b, g, c: (0, 0)),
            pl.BlockSpec((1, LANES), lambda b, g, c: (0, 0)),
            pl.BlockSpec((1, gd), lambda b, g, c: (0, g)),
            pl.BlockSpec((1, gd), lambda b, g, c: (0, g)),
        ],
        out_specs=pl.BlockSpec((L, gd), lambda b, g, c: (rows(b, g, c), g)),
        out_shape=jax.ShapeDtypeStruct((t_tokens, SSD_D_INNER), BF16),
        scratch_shapes=[
            pltpu.VMEM((SUBLANES + L, n_conv), F32),
            pltpu.VMEM((ns, gd), F32),
            pltpu.VMEM((LANES, L), F32),
            pltpu.VMEM((LANES, L), F32),
        ],
        compiler_params=_params("arbitrary", "arbitrary", "arbitrary"),
        name="ssd",
    )(h_zx, h_zx, h_zx, h_zx, h_dt, cw, cb, dtb, alog, dskip, norm_g.reshape(1, SSD_D_INNER))


def _gelu_tanh(x):
    return 0.5 * x * (1.0 + jnp.tanh(math.sqrt(2.0 / math.pi) * (x + 0.044715 * (x * x * x))))


def _gmlp_kernel(uv_ref, lng_ref, lnb_ref, ws_ref, bs_ref, o_ref):
    L = GMLP_CHUNK
    gdim = GMLP_GROUP_DIM
    row = lax.broadcasted_iota(I32, (L, L), 0)
    col = lax.broadcasted_iota(I32, (L, L), 1)
    causal = row >= col
    for g in range(GMLP_GROUPS):
        cols = slice(g * gdim, (g + 1) * gdim)
        u = _gelu_tanh(uv_ref[:, cols])
        v = _gelu_tanh(uv_ref[:, GMLP_WIDTH + g * gdim:GMLP_WIDTH + (g + 1) * gdim])
        v = _layer_norm_rows(v, lng_ref[:, cols], lnb_ref[:, cols])
        ws = jnp.where(causal, ws_ref[g], 0.0).astype(BF16)
        sv = jnp.dot(ws, v.astype(BF16), preferred_element_type=F32) + bs_ref[:, cols]
        o_ref[:, cols] = (u * sv).astype(o_ref.dtype)


def _gmlp(h_uv, ln_g, ln_b, ws, bs):
    t_tokens = h_uv.shape[0]
    L = GMLP_CHUNK
    bs_cols = jnp.repeat(bs.T, GMLP_GROUP_DIM, axis=1)
    return pl.pallas_call(
        _gmlp_kernel,
        grid=(t_tokens // L,),
        in_specs=[
            pl.BlockSpec((L, 2 * GMLP_WIDTH), lambda i: (i, 0)),
            pl.BlockSpec((1, GMLP_WIDTH), lambda i: (0, 0)),
            pl.BlockSpec((1, GMLP_WIDTH), lambda i: (0, 0)),
            pl.BlockSpec((GMLP_GROUPS, L, L), lambda i: (0, 0, 0)),
            pl.BlockSpec((L, GMLP_WIDTH), lambda i: (0, 0)),
        ],
        out_specs=pl.BlockSpec((L, GMLP_WIDTH), lambda i: (i, 0)),
        out_shape=jax.ShapeDtypeStruct((t_tokens, GMLP_WIDTH), BF16),
        compiler_params=_params("arbitrary"),
        name="gmlp",
    )(h_uv, ln_g.reshape(1, GMLP_WIDTH), ln_b.reshape(1, GMLP_WIDTH), ws, bs_cols)


def _router_kernel(x_ref, rw_ref, rb_ref, eidx_ref, gate_ref, rank_ref, cnt_ref, carry_sc):
    tm = x_ref.shape[0]
    G, K = N_EXPERT_GROUPS, EXPERTS_PER_GROUP

    @pl.when(pl.program_id(0) == 0)
    def _():
        carry_sc[...] = jnp.zeros_like(carry_sc)

    logits = lax.dot_general(rw_ref[...], x_ref[...], (((1,), (1,)), ((), ())),
                             preferred_element_type=F32, precision=lax.Precision.HIGHEST)
    scores = jax.nn.sigmoid(logits)
    biased = scores + rb_ref[...]
    a = [biased[k * G:(k + 1) * G] for k in range(K)]
    sc = [scores[k * G:(k + 1) * G] for k in range(K)]
    hi01, lo01 = jnp.maximum(a[0], a[1]), jnp.minimum(a[0], a[1])
    hi23, lo23 = jnp.maximum(a[2], a[3]), jnp.minimum(a[2], a[3])
    grp_score = jnp.maximum(hi01, hi23) + jnp.maximum(jnp.minimum(hi01, hi23), jnp.maximum(lo01, lo23))
    gidx = lax.broadcasted_iota(I32, (G, tm), 0)
    g_best = jnp.max(grp_score, axis=0, keepdims=True)
    g_sel = jnp.min(jnp.where(grp_score == g_best, gidx, G), axis=0, keepdims=True)
    in_grp = gidx == g_sel
    cand = [jnp.sum(jnp.where(in_grp, a[k], 0.0), axis=0, keepdims=True) for k in range(K)]
    cand_sc = [jnp.sum(jnp.where(in_grp, sc[k], 0.0), axis=0, keepdims=True) for k in range(K)]

    def first_argmax(vals):
        best, idx = vals[0], jnp.zeros((1, tm), I32)
        for k in range(1, K):
            gt = vals[k] > best
            best = jnp.where(gt, vals[k], best)
            idx = jnp.where(gt, k, idx)
        return idx

    i1 = first_argmax(cand)
    i2 = first_argmax([jnp.where(i1 == k, -jnp.inf, cand[k]) for k in range(K)])
    v1 = sum(jnp.where(i1 == k, cand_sc[k], 0.0) for k in range(K))
    v2 = sum(jnp.where(i2 == k, cand_sc[k], 0.0) for k in range(K))
    den = v1 + v2
    eidx_ref[0:1, :] = g_sel * K + i1
    eidx_ref[1:2, :] = g_sel * K + i2
    gate_ref[0:1, :] = v1 / den
    gate_ref[1:2, :] = v2 / den

    member = jnp.concatenate(
        [jnp.where(in_grp & ((i1 == k) | (i2 == k)), 1.0, 0.0) for k in range(K)], axis=0)
    trow = lax.broadcasted_iota(I32, (tm, tm), 0)
    tcol = lax.broadcasted_iota(I32, (tm, tm), 1)
    before = (trow < tcol).astype(BF16)
    prefix = jnp.dot(member.astype(BF16), before, preferred_element_type=F32) + carry_sc[...]
    r1 = sum(jnp.sum(jnp.where(in_grp & (i1 == k), prefix[k * G:(k + 1) * G], 0.0), axis=0, keepdims=True)
             for k in range(K))
    r2 = sum(jnp.sum(jnp.where(in_grp & (i2 == k), prefix[k * G:(k + 1) * G], 0.0), axis=0, keepdims=True)
             for k in range(K))
    rank_ref[0:1, :] = r1.astype(I32)
    rank_ref[1:2, :] = r2.astype(I32)
    carry_sc[...] = carry_sc[...] + jnp.sum(member, axis=1, keepdims=True)
    cnt_ref[...] = jnp.broadcast_to(carry_sc[...], cnt_ref.shape).astype(I32)


def _slot_of_expert(e):
    return (e % EXPERTS_PER_GROUP) * N_EXPERT_GROUPS + e // EXPERTS_PER_GROUP


def _expert_of_slot(r):
    return (r % N_EXPERT_GROUPS) * EXPERTS_PER_GROUP + r // N_EXPERT_GROUPS


def _route(x, router_w, router_bias):
    t_tokens, d = x.shape
    tm = ROUTER_ROWS
    slot_expert = _expert_of_slot(jnp.arange(N_EXPERTS))
    rw = router_w.T[slot_expert]
    rb = router_bias[slot_expert].reshape(N_EXPERTS, 1)
    tok = pl.BlockSpec((TOP_K, tm), lambda i: (0, i))
    return pl.pallas_call(
        _router_kernel,
        grid=(t_tokens // tm,),
        in_specs=[
            pl.BlockSpec((tm, d), lambda i: (i, 0)),
            pl.BlockSpec((N_EXPERTS, d), lambda i: (0, 0)),
            pl.BlockSpec((N_EXPERTS, 1), lambda i: (0, 0)),
        ],
        out_specs=[tok, tok, tok, pl.BlockSpec((N_EXPERTS, LANES), lambda i: (0, 0))],
        out_shape=[
            jax.ShapeDtypeStruct((TOP_K, t_tokens), I32),
            jax.ShapeDtypeStruct((TOP_K, t_tokens), F32),
            jax.ShapeDtypeStruct((TOP_K, t_tokens), I32),
            jax.ShapeDtypeStruct((N_EXPERTS, LANES), I32),
        ],
        scratch_shapes=[pltpu.VMEM((N_EXPERTS, 1), F32)],
        compiler_params=_params("arbitrary"),
        name="router",
    )(x, rw, rb)


def _ffn_kernel(nused_ref, iexp_ref, nval_ref, src_ref, src_next_ref, xp_hbm, wg_ref, wu_ref, wd_ref,
                yk_hbm, xbuf, hg_sc, hu_sc, hid_sc, obuf, gsem, ssem):
    item = pl.program_id(0)
    step = pl.program_id(1)
    n_used = nused_ref[0]
    slot = lax.rem(item, 2)
    sub = MOE_SUB_ROWS
    half = MOE_K_CHUNK // 2

    n_tokens = xp_hbm.shape[0]
    unroll = MOE_DMA_UNROLL

    def for_rows(count, fn):
        def body(i, carry):
            fn(i)
            return carry
        lax.fori_loop(0, count, body, 0)

    def for_rows_unrolled(count, fn):
        def body(b, carry):
            for u in range(unroll):
                fn(b * unroll + u)
            return carry
        n_full = count // unroll
        lax.fori_loop(0, n_full, body, 0)
        lax.fori_loop(n_full * unroll, count, lambda i, c: (fn(i), c)[1], 0)

    def gather_rows(slot_rows, count, to_slot):
        def start(i):
            slot_id = slot_rows[0, i]
            if n_tokens & (n_tokens - 1) == 0:
                tok = slot_id & (n_tokens - 1)
            else:
                tok = lax.rem(slot_id, n_tokens)
            pltpu.make_async_copy(xp_hbm.at[pl.ds(tok, 1), :], xbuf.at[to_slot, pl.ds(i, 1), :],
                                  gsem.at[to_slot]).start(priority=1)
        for_rows_unrolled(count, start)

    def scatter_rows(count):
        def start(i):
            pltpu.make_async_copy(obuf.at[pl.ds(i, 1), :], yk_hbm.at[pl.ds(src_ref[0, i], 1), :],
                                  ssem).start(priority=1)
        for_rows_unrolled(count, start)

    def wait_rows(count, n_rows_copy):
        n_full = count // unroll
        for_rows(n_full, lambda i: n_rows_copy(unroll).wait())
        for_rows(count - n_full * unroll, lambda i: n_rows_copy(1).wait())

    def gathered(to_slot):
        return lambda n: pltpu.make_async_copy(xp_hbm.at[pl.ds(0, n), :], xbuf.at[to_slot, pl.ds(0, n), :],
                                               gsem.at[to_slot])

    def scattered(n):
        return pltpu.make_async_copy(obuf.at[pl.ds(0, n), :], yk_hbm.at[pl.ds(0, n), :], ssem)

    @pl.when(item < n_used)
    def _():
        n_rows = nval_ref[item]
        n_sub = (n_rows + sub - 1) // sub

        @pl.when(step == 0)
        def _():
            @pl.when(item == 0)
            def _():
                xbuf[...] = jnp.zeros_like(xbuf)
                gather_rows(src_ref, n_rows, 0)

            @pl.when(item + 1 < n_used)
            def _():
                gather_rows(src_next_ref, nval_ref[item + 1], 1 - slot)

            wait_rows(n_rows, gathered(slot))

        for c in range(MOE_K_STEPS):
            @pl.when(step == c)
            def _(c=c):
                wg = wg_ref[...].astype(BF16)
                wu = wu_ref[...].astype(BF16)

                def sub_block(r):
                    rows = pl.ds(pl.multiple_of(r * sub, sub), sub)
                    lo, hi = _unpack_bf16_pair(xbuf[slot, rows, c * half:(c + 1) * half])
                    xk = jnp.concatenate([lo, hi], axis=1).astype(BF16)
                    g = jnp.dot(xk, wg, preferred_element_type=F32)
                    u = jnp.dot(xk, wu, preferred_element_type=F32)
                    if c > 0:
                        g = g + hg_sc[rows, :]
                        u = u + hu_sc[rows, :]
                    if c < MOE_K_STEPS - 1:
                        hg_sc[rows, :] = g
                        hu_sc[rows, :] = u
                    else:
                        hid_sc[rows, :] = (_silu(g) * u).astype(BF16)

                for_rows(n_sub, sub_block)

        @pl.when(step == MOE_K_STEPS)
        def _():
            @pl.when(item > 0)
            def _():
                wait_rows(nval_ref[item - 1], scattered)

            wd = wd_ref[...].astype(BF16)

            def sub_block(r):
                rows = pl.ds(pl.multiple_of(r * sub, sub), sub)
                o = jnp.dot(hid_sc[rows, :], wd, preferred_element_type=F32)
                obuf[rows, :] = lax.bitcast_convert_type(o.astype(BF16).astype(F32), U32) >> 16

            for_rows(n_sub, sub_block)

        @pl.when(step == MOE_K_STEPS + 1)
        def _():
            wd = wd_ref[...].astype(BF16)

            def sub_block(r):
                rows = pl.ds(pl.multiple_of(r * sub, sub), sub)
                o = jnp.dot(hid_sc[rows, :], wd, preferred_element_type=F32)
                obuf[rows, :] = obuf[rows, :] | (lax.bitcast_convert_type(o.astype(BF16).astype(F32), U32)
                                                 & HIGH_HALF)

            for_rows(n_sub, sub_block)
            scatter_rows(n_rows)

            @pl.when(item == n_used - 1)
            def _():
                wait_rows(n_rows, scattered)


def _expert_ffn(xp, item_expert, item_rows, n_used, row_slot, w_gate, w_up, w_down, layer):
    t_tokens, dp = xp.shape
    d = 2 * dp
    rows = MOE_ROWS
    n_items = item_expert.shape[0]
    oc = d // MOE_O_STEPS
    last = MOE_K_STEPS + MOE_O_STEPS - 1

    def pos(i, s, nu):
        used = i < nu[0]
        return jnp.where(used, i, nu[0] - 1), jnp.where(used, s, last)

    def w_in_map(i, s, nu, ie, nv):
        ii, ss = pos(i, s, nu)
        return layer, ie[ii], jnp.minimum(ss, MOE_K_STEPS - 1), 0

    def w_down_map(i, s, nu, ie, nv):
        ii, ss = pos(i, s, nu)
        early = ss < MOE_K_STEPS
        return (layer, ie[jnp.where(early, jnp.maximum(ii - 1, 0), ii)], 0,
                jnp.where(early, MOE_O_STEPS - 1, ss - MOE_K_STEPS))

    def rows_map(i, s, nu, ie, nv):
        return pos(i, s, nu)[0], 0, 0

    def next_rows_map(i, s, nu, ie, nv):
        return jnp.minimum(pos(i, s, nu)[0] + 1, n_items - 1), 0, 0

    idx_block = (None, 1, rows)
    return pl.pallas_call(
        _ffn_kernel,
        grid_spec=pltpu.PrefetchScalarGridSpec(
            num_scalar_prefetch=3,
            grid=(n_items, MOE_K_STEPS + MOE_O_STEPS),
            in_specs=[
                pl.BlockSpec(idx_block, rows_map, memory_space=pltpu.SMEM),
                pl.BlockSpec(idx_block, next_rows_map, memory_space=pltpu.SMEM),
                pl.BlockSpec(memory_space=pl.ANY),
                pl.BlockSpec((None, None, MOE_K_CHUNK, D_FF), w_in_map),
                pl.BlockSpec((None, None, MOE_K_CHUNK, D_FF), w_in_map),
                pl.BlockSpec((None, None, D_FF, oc), w_down_map),
            ],
            out_specs=pl.BlockSpec(memory_space=pl.ANY),
            scratch_shapes=[
                pltpu.VMEM((2, rows, dp), U32),
                pltpu.VMEM((rows, D_FF), F32),
                pltpu.VMEM((rows, D_FF), F32),
                pltpu.VMEM((rows, D_FF), BF16),
                pltpu.VMEM((rows, dp), U32),
                pltpu.SemaphoreType.DMA((2,)),
                pltpu.SemaphoreType.DMA(()),
            ],
        ),
        out_shape=jax.ShapeDtypeStruct((TOP_K * t_tokens, dp), U32),
        compiler_params=_params("arbitrary", "arbitrary"),
        name="moe_ffn",
    )(n_used, item_expert, item_rows, row_slot.reshape(n_items, 1, rows), row_slot.reshape(n_items, 1, rows),
      xp, w_gate, w_up, w_down)


def _combine_kernel(y0_ref, y1_ref, gate_ref, x_ref, g_ref, b_ref, o_ref, obf_ref):
    gates = gate_ref[...]
    ffn = (gates[:, 0:1] * jnp.concatenate(_unpack_bf16_pair(y0_ref[...]), axis=1)
           + gates[:, 1:2] * jnp.concatenate(_unpack_bf16_pair(y1_ref[...]), axis=1))
    y = _layer_norm_rows(ALPHA * x_ref[...] + ffn, g_ref[...], b_ref[...])
    o_ref[...] = y
    obf_ref[...] = y.astype(BF16)


def _combine_layer_norm(x, yk, gate, g, b):
    t_tokens, d = x.shape
    tm = LN_ROWS
    n_steps = t_tokens // tm
    row = pl.BlockSpec((tm, d), lambda i: (i, 0))
    vec = pl.BlockSpec((1, d), lambda i: (0, 0))
    return pl.pallas_call(
        _combine_kernel,
        grid=(n_steps,),
        in_specs=[
            pl.BlockSpec((tm, d // 2), lambda i: (i, 0)),
            pl.BlockSpec((tm, d // 2), lambda i: (i + n_steps, 0)),
            pl.BlockSpec((tm, TOP_K), lambda i: (i, 0)),
            row, vec, vec,
        ],
        out_specs=[row, row],
        out_shape=[jax.ShapeDtypeStruct((t_tokens, d), F32), jax.ShapeDtypeStruct((t_tokens, d), BF16)],
        compiler_params=_params("arbitrary"),
        name="moe_combine",
    )(yk, yk, gate.T, x, g.reshape(1, d), b.reshape(1, d))


def _moe_layer_norm(x, xp, layer, router_w, router_bias, w_gate, w_up, w_down, ln_g, ln_b):
    t_tokens, _ = x.shape
    rows = MOE_ROWS
    eidx, gate, rank, counts = _route(x, router_w, router_bias)
    cnt = counts[:, 0]
    items_per_slot = (cnt + rows - 1) // rows
    item_end = jnp.cumsum(items_per_slot)
    row_start = (item_end - items_per_slot) * rows
    in_slot = _slot_of_expert(eidx)[..., None] == jnp.arange(N_EXPERTS, dtype=I32)
    dest = (jnp.sum(jnp.where(in_slot, row_start, 0), axis=-1) + rank).reshape(-1)
    n_items = (t_tokens * TOP_K) // rows + N_EXPERTS
    n_used = item_end[-1].astype(I32).reshape(1)
    item_id = jnp.arange(n_items, dtype=I32)
    item_slot = jnp.minimum(jnp.sum(item_id[:, None] >= item_end[None, :], axis=1), N_EXPERTS - 1)
    item_expert = _expert_of_slot(item_slot).astype(I32)
    item_rows = jnp.clip(row_start[item_slot] + cnt[item_slot] - item_id * rows, 0, rows)
    item_rows = jnp.where(item_id < n_used[0], item_rows, 0).astype(I32)
    row_slot = jnp.zeros((n_items * rows,), I32).at[dest].set(jnp.arange(TOP_K * t_tokens, dtype=I32))

    yk = _expert_ffn(xp, item_expert, item_rows, n_used, row_slot, w_gate, w_up, w_down, layer)
    return _combine_layer_norm(x, yk, gate, ln_g, ln_b)


def kernel(x, rel_bias, even_w_in, even_w_out, diff_lambda, diff_subln_g, odd_w_in, odd_w_out, ssd_conv_w, ssd_conv_b, ssd_dt_bias, ssd_a_log, ssd_d, ssd_norm_g, gmlp_ln_g, gmlp_ln_b, gmlp_ws, gmlp_bs, router_w, router_bias, moe_w_gate, moe_w_up, moe_w_down, ln_mix_g, ln_mix_b, ln_ffn_g, ln_ffn_b):
    bsz, s, d = x.shape
    t_tokens = bsz * s
    xf = x.reshape(t_tokens, d)
    xb = xf.astype(BF16)
    moe_args = (router_w, router_bias, moe_w_gate, moe_w_up, moe_w_down)

    lambda_init = 0.8 - 0.6 * math.exp(-0.3 * 0)
    h = _project([xb], even_w_in[0], [0], 0, EVEN_IN, BF16, tm=1024, tn=512)
    attn = _even_attention(h.reshape(bsz, s, EVEN_IN), _bias_tiles(rel_bias), diff_lambda[0],
                           diff_subln_g[0], lambda_init)
    mix = _project([attn.reshape(t_tokens, d)], even_w_out[0], [0], 0, d, F32, tm=1024, tn=512)
    xf, xp = _residual_layer_norm(xf, mix, ln_mix_g[0], ln_mix_b[0])
    xf, xb = _moe_layer_norm(xf, xp, 0, *moe_args, ln_ffn_g[0], ln_ffn_b[0])

    zx_cols = SSD_D_INNER + SSD_CONV_DIM
    w_in_t = jnp.swapaxes(odd_w_in[0], 0, 1)
    h_zx = _project([xb], w_in_t, [0], 0, zx_cols, F32, tm=1024, tn=512, w_transposed=True)
    h_dt = _project([xb], w_in_t, [0], zx_cols, LANES, F32, tm=1024, tn=LANES, w_transposed=True)
    h_uv = _project([xb], w_in_t, [0], C_IN, 2 * GMLP_WIDTH, F32, tm=1024, tn=512, w_transposed=True)
    y_ssd = _ssd(h_zx, h_dt, bsz, ssd_conv_w[0], ssd_conv_b[0], ssd_dt_bias[0], ssd_a_log[0], ssd_d[0],
                 ssd_norm_g[0])
    y_gmlp = _gmlp(h_uv, gmlp_ln_g[0], gmlp_ln_b[0], gmlp_ws[0], gmlp_bs[0])
    mix = _project([y_ssd, y_gmlp], odd_w_out[0], [0, SSD_D_INNER // GMLP_WIDTH], 0, d, F32, tm=512, tn=512)
    xf, xp = _residual_layer_norm(xf, mix, ln_mix_g[1], ln_mix_b[1])
    xf, xb = _moe_layer_norm(xf, xp, 1, *moe_args, ln_ffn_g[1], ln_ffn_b[1])
    return xf.reshape(bsz, s, d)
```

```python
import functools
import math

import numpy as np
import jax
import jax.numpy as jnp
from jax import lax
from jax.experimental import pallas as pl
from jax.experimental.pallas import tpu as pltpu

F32 = jnp.float32
BF16 = jnp.bfloat16
I32 = jnp.int32

D_MODEL = 4096
DEPTH = 2
HEAD_DIM = 128
N_HEADS_A = 16
N_HEADS_B = 16
DIFF_HALF = HEAD_DIM // 2
N_BUCKETS = 32
MAX_DISTANCE = 128
A_IN = 3 * N_HEADS_A * HEAD_DIM
EVEN_IN = A_IN + 3 * N_HEADS_B * HEAD_DIM
DIFF_SCALE = DIFF_HALF ** -0.5
SB_SCALE = HEAD_DIM ** -0.5

SSD_HEAD_DIM = 64
SSD_D_INNER = D_MODEL
SSD_N_HEADS = SSD_D_INNER // SSD_HEAD_DIM
SSD_N_GROUPS = 8
SSD_HEADS_PER_GROUP = SSD_N_HEADS // SSD_N_GROUPS
SSD_D_STATE = 128
SSD_CONV = 4
SSD_CHUNK = 128
SSD_GN = SSD_N_GROUPS * SSD_D_STATE
SSD_CONV_DIM = SSD_D_INNER + 2 * SSD_GN
SSD_GROUP_DIM = SSD_D_INNER // SSD_N_GROUPS
C_IN = SSD_D_INNER + SSD_CONV_DIM + SSD_N_HEADS
GMLP_WIDTH = D_MODEL // 2
GMLP_GROUPS = 8
GMLP_GROUP_DIM = GMLP_WIDTH // GMLP_GROUPS
GMLP_CHUNK = 128

N_EXPERTS = 32
N_EXPERT_GROUPS = 8
EXPERTS_PER_GROUP = N_EXPERTS // N_EXPERT_GROUPS
TOP_K = 2
D_FF = 768

ALPHA = (2 * DEPTH) ** 0.25
LN_EPS = 1e-5

LANES = 128
SUBLANES = 8
VMEM_LIMIT_BYTES = 58 * 1024 * 1024

ATTN_TILE = 256
ATTN_HEADS_PER_STEP = 4
ATTN_ONES_ROWS = 16
MOE_ROWS = 768
MOE_SUB_ROWS = 192
MOE_K_STEPS = 4
MOE_K_CHUNK = D_MODEL // MOE_K_STEPS
MOE_O_STEPS = 2
MOE_DMA_UNROLL = 8
U32 = jnp.uint32
HIGH_HALF = np.uint32(0xFFFF0000)
ROUTER_ROWS = 512
LN_ROWS = 256


def _params(*sem):
    return pltpu.CompilerParams(dimension_semantics=sem, vmem_limit_bytes=VMEM_LIMIT_BYTES)


def _proj_kernel(*refs, n_in, w_transposed):
    x_refs = refs[:n_in]
    w_refs = refs[n_in:2 * n_in]
    o_ref = refs[2 * n_in]
    wbf_refs = refs[2 * n_in + 1:]

    @pl.when(pl.program_id(1) == 0)
    def _():
        for w_ref, wbf_ref in zip(w_refs, wbf_refs):
            wbf_ref[...] = w_ref[...].astype(BF16)

    mm = _dot_nt if w_transposed else functools.partial(jnp.dot, preferred_element_type=F32)
    acc = mm(x_refs[0][...], wbf_refs[0][...])
    for x_ref, wbf_ref in zip(x_refs[1:], wbf_refs[1:]):
        acc = acc + mm(x_ref[...], wbf_ref[...])
    o_ref[...] = acc.astype(o_ref.dtype)


def _project(xs, w, row_blocks, col0, n_cols, out_dtype, tm, tn, w_transposed=False):
    m = xs[0].shape[0]
    n_in = len(xs)
    in_specs = [pl.BlockSpec((tm, x.shape[1]), lambda j, i: (i, 0)) for x in xs]
    for x, rb in zip(xs, row_blocks):
        k = x.shape[1]
        if not w_transposed:
            in_specs.append(pl.BlockSpec((k, tn), lambda j, i, rb=rb: (rb, j + col0 // tn)))
        elif col0 % tn == 0:
            in_specs.append(pl.BlockSpec((tn, k), lambda j, i, rb=rb: (j + col0 // tn, rb)))
        else:
            in_specs.append(pl.BlockSpec((pl.Element(tn), pl.Element(k)),
                                         lambda j, i, rb=rb, k=k: (pl.multiple_of(col0 + j * tn, SUBLANES),
                                                                   rb * k)))
    return pl.pallas_call(
        functools.partial(_proj_kernel, n_in=n_in, w_transposed=w_transposed),
        grid=(n_cols // tn, m // tm),
        in_specs=in_specs,
        out_specs=pl.BlockSpec((tm, tn), lambda j, i: (i, j)),
        out_shape=jax.ShapeDtypeStruct((m, n_cols), out_dtype),
        scratch_shapes=[pltpu.VMEM((tn, x.shape[1]) if w_transposed else (x.shape[1], tn), BF16) for x in xs],
        compiler_params=_params("arbitrary", "arbitrary"),
        name="proj",
    )(*xs, *([w] * n_in))


def _layer_norm_rows(y, g, b):
    mu = jnp.mean(y, axis=-1, keepdims=True)
    yc = y - mu
    var = jnp.mean(yc * yc, axis=-1, keepdims=True)
    return yc * lax.rsqrt(var + LN_EPS) * g + b


def _pack_bf16_pair(lo, hi):
    lo_bits = lax.bitcast_convert_type(lo.astype(BF16).astype(F32), U32)
    hi_bits = lax.bitcast_convert_type(hi.astype(BF16).astype(F32), U32)
    return (lo_bits >> 16) | (hi_bits & HIGH_HALF)


def _unpack_bf16_pair(p):
    return (lax.bitcast_convert_type(p << 16, F32), lax.bitcast_convert_type(p & HIGH_HALF, F32))


def _pack_rows(y):
    half = MOE_K_CHUNK // 2
    parts = [_pack_bf16_pair(y[:, c * MOE_K_CHUNK:c * MOE_K_CHUNK + half],
                             y[:, c * MOE_K_CHUNK + half:(c + 1) * MOE_K_CHUNK])
             for c in range(y.shape[1] // MOE_K_CHUNK)]
    return jnp.concatenate(parts, axis=1)


def _res_ln_kernel(x_ref, mix_ref, g_ref, b_ref, o_ref, opk_ref):
    y = _layer_norm_rows(ALPHA * x_ref[...] + mix_ref[...], g_ref[...], b_ref[...])
    o_ref[...] = y
    opk_ref[...] = _pack_rows(y)


def _residual_layer_norm(x, mix, g, b):
    m, d = x.shape
    tm = LN_ROWS
    row = pl.BlockSpec((tm, d), lambda i: (i, 0))
    half_row = pl.BlockSpec((tm, d // 2), lambda i: (i, 0))
    vec = pl.BlockSpec((1, d), lambda i: (0, 0))
    return pl.pallas_call(
        _res_ln_kernel,
        grid=(m // tm,),
        in_specs=[row, row, vec, vec],
        out_specs=[row, half_row],
        out_shape=[jax.ShapeDtypeStruct((m, d), F32), jax.ShapeDtypeStruct((m, d // 2), U32)],
        compiler_params=_params("arbitrary"),
        name="res_ln",
    )(x, mix, g.reshape(1, d), b.reshape(1, d))


def _bias_tile_kernel(rb_ref, o_ref):
    head = pl.program_id(0)
    t = ATTN_TILE
    key = lax.broadcasted_iota(I32, (t, t), 0)
    qry = lax.broadcasted_iota(I32, (t, t), 1)
    max_exact = N_BUCKETS // 2
    for d in range(2):
        n = jnp.maximum(d * t + qry - key, 0)
        nf = jnp.maximum(n, 1).astype(F32)
        large = max_exact + (jnp.log(nf / max_exact) / math.log(MAX_DISTANCE / max_exact)
                             * (N_BUCKETS - max_exact)).astype(I32)
        large = jnp.minimum(large, N_BUCKETS - 1)
        bucket = jnp.where(n < max_exact, n, large)
        acc = jnp.zeros((t, t), F32)
        for bkt in range(N_BUCKETS):
            acc = jnp.where(bucket == bkt, rb_ref[bkt, head], acc)
        o_ref[d] = acc


def _bias_tiles(rel_bias):
    t = ATTN_TILE
    return pl.pallas_call(
        _bias_tile_kernel,
        grid=(N_HEADS_A,),
        in_specs=[pl.BlockSpec(memory_space=pltpu.SMEM)],
        out_specs=pl.BlockSpec((None, 2, t, t), lambda h: (h, 0, 0, 0)),
        out_shape=jax.ShapeDtypeStruct((N_HEADS_A, 2, t, t), F32),
        compiler_params=_params("arbitrary"),
        name="bias_tiles",
    )(rel_bias)


def _dot_nt(a, b):
    return lax.dot_general(a, b, (((1,), (1,)), ((), ())), preferred_element_type=F32)


def _head_cols(h):
    return slice(h * HEAD_DIM, (h + 1) * HEAD_DIM)


def _value_transposes(v_ref, vt_sc):
    t = ATTN_TILE
    ones = jnp.ones((ATTN_ONES_ROWS, t), BF16)
    for h in range(ATTN_HEADS_PER_STEP):
        for j in range(v_ref.shape[0] // t):
            vb = v_ref[j * t:(j + 1) * t, _head_cols(h)]
            vt_sc[h, j, 0:HEAD_DIM, :] = vb.astype(F32).T.astype(BF16)
            vt_sc[h, j, HEAD_DIM:, :] = ones


def _diff_attention(dl_ref, g_ref, q_ref, k_ref, vt_sc, bias_ref, o_ref, m_sc, acc_sc, lambda_init):
    t = ATTN_TILE
    qi = pl.program_id(2)
    lane = lax.broadcasted_iota(I32, (t, HEAD_DIM), 1)
    q_maps = []
    for h in range(ATTN_HEADS_PER_STEP):
        q = q_ref[:, _head_cols(h)] * DIFF_SCALE
        zero = jnp.zeros_like(q)
        q_maps.append((jnp.where(lane < DIFF_HALF, q, zero), jnp.where(lane >= DIFF_HALF, q, zero)))

    chains = [(h, mp) for h in range(ATTN_HEADS_PER_STEP) for mp in range(2)]

    def block(j, bias_of_head, mask, first):
        kv_start = pl.multiple_of(j * t, t)
        kbs = [k_ref[pl.ds(kv_start, t), _head_cols(h)] for h in range(ATTN_HEADS_PER_STEP)]
        scores = [_dot_nt(kbs[h], q_maps[h][mp]) for h, mp in chains]
        probs, alphas = [], []
        for (h, mp), s in zip(chains, scores):
            bias = bias_of_head(h)
            uniform = bias.shape == (1, 1)
            if not uniform:
                s = s + bias
            if mask is not None:
                s = jnp.where(mask, s, -jnp.inf)
            m_new = jnp.max(s, axis=0, keepdims=True)
            if uniform:
                m_new = m_new + bias
            if not first:
                m_old = m_sc[h, mp]
                m_new = jnp.maximum(m_old, m_new)
                alphas.append(jnp.exp(m_old - m_new))
            m_sc[h, mp] = m_new
            probs.append(jnp.exp(s - (m_new - bias if uniform else m_new)).astype(BF16))
        for i, (h, mp) in enumerate(chains):
            pv = jnp.dot(vt_sc[h, j], probs[i], preferred_element_type=F32)
            acc_sc[h, mp] = pv if first else alphas[i] * acc_sc[h, mp] + pv

    key = lax.broadcasted_iota(I32, (t, t), 0)
    qry = lax.broadcasted_iota(I32, (t, t), 1)
    block(qi, lambda h: bias_ref[h, 0], qry >= key, True)

    @pl.when(qi >= 1)
    def _():
        block(qi - 1, lambda h: bias_ref[h, 1], None, False)

    def far_body(j, carry):
        block(j, lambda h: bias_ref[h, 1, 0:1, t - 1:t], None, False)
        return carry

    lax.fori_loop(0, qi - 1, far_body, 0)

    dl = dl_ref[...]
    lam = (jnp.exp(jnp.sum(dl[0:1] * dl[1:2], axis=-1, keepdims=True))
           - jnp.exp(jnp.sum(dl[2:3] * dl[3:4], axis=-1, keepdims=True)) + lambda_init)
    for h in range(ATTN_HEADS_PER_STEP):
        a0 = acc_sc[h, 0]
        a1 = acc_sc[h, 1]
        oa = (a0[0:HEAD_DIM] / a0[HEAD_DIM:HEAD_DIM + 1]
              - lam * (a1[0:HEAD_DIM] / a1[HEAD_DIM:HEAD_DIM + 1]))
        ms = jnp.mean(oa * oa, axis=0, keepdims=True)
        oa = (oa * lax.rsqrt(ms + LN_EPS)).T * g_ref[...] * (1.0 - lambda_init)
        o_ref[:, _head_cols(h)] = oa.astype(o_ref.dtype)


def _log_sigmoid(z):
    return jnp.minimum(z, 0.0) - jnp.log(1.0 + jnp.exp(-jnp.abs(z)))


def _split_bf16(x):
    hi = x.astype(BF16)
    lo = (x - hi.astype(F32)).astype(BF16)
    return hi, lo


def _stick_breaking_attention(q_ref, k_ref, vt_sc, o_ref, c_sc, acc_sc):
    t = ATTN_TILE
    qi = pl.program_id(2)
    key = lax.broadcasted_iota(I32, (t, t), 0)
    qry = lax.broadcasted_iota(I32, (t, t), 1)
    later = qry > key
    after = later.astype(BF16)

    heads = range(ATTN_HEADS_PER_STEP)

    def block(j, strict, first):
        kv_start = pl.multiple_of(j * t, t)
        zs = [_dot_nt(k_ref[pl.ds(kv_start, t), _head_cols(h)], q_ref[:, _head_cols(h)]) for h in heads]
        log_betas, splits, col_sums = [], [], []
        for h in heads:
            z = zs[h] * SB_SCALE
            log_beta = _log_sigmoid(z)
            log_1mb = log_beta - z
            if strict is not None:
                log_1mb = jnp.where(strict, log_1mb, 0.0)
            log_betas.append(log_beta)
            splits.append(_split_bf16(log_1mb))
            col_sums.append(jnp.sum(log_1mb, axis=0, keepdims=True))
        tails = [jnp.dot(after, hi, preferred_element_type=F32) + jnp.dot(after, lo, preferred_element_type=F32)
                 for hi, lo in splits]
        weights = []
        for h in heads:
            tail = tails[h] if first else tails[h] + c_sc[h]
            w = jnp.exp(log_betas[h] + tail)
            if strict is not None:
                w = jnp.where(strict, w, 0.0)
            weights.append(w.astype(BF16))
            c_sc[h] = col_sums[h] if first else c_sc[h] + col_sums[h]
        for h in heads:
            pv = jnp.dot(vt_sc[h, j][0:HEAD_DIM], weights[h], preferred_element_type=F32)
            acc_sc[h, 0, 0:HEAD_DIM] = pv if first else acc_sc[h, 0, 0:HEAD_DIM] + pv

    block(qi, later, True)

    def body(step, carry):
        block(qi - 1 - step, None, False)
        return carry

    lax.fori_loop(0, qi, body, 0)
    for h in range(ATTN_HEADS_PER_STEP):
        o_ref[:, _head_cols(h)] = acc_sc[h, 0, 0:HEAD_DIM].T.astype(o_ref.dtype)


def _attn_kernel(dl_ref, g_ref, q_ref, k_ref, v_ref, bias_ref, o_ref, vt_sc, m_sc, c_sc, acc_sc, *, lambda_init):
    grp = pl.program_id(1)

    @pl.when(pl.program_id(2) == 0)
    def _():
        _value_transposes(v_ref, vt_sc)

    @pl.when(grp < N_HEADS_A // ATTN_HEADS_PER_STEP)
    def _():
        _diff_attention(dl_ref, g_ref, q_ref, k_ref, vt_sc, bias_ref, o_ref, m_sc, acc_sc, lambda_init)

    @pl.when(grp >= N_HEADS_A // ATTN_HEADS_PER_STEP)
    def _():
        _stick_breaking_attention(q_ref, k_ref, vt_sc, o_ref, c_sc, acc_sc)


def _even_attention(h, bias_tiles, diff_lambda, subln_g, lambda_init):
    bsz, s, _ = h.shape
    t = ATTN_TILE
    nh = ATTN_HEADS_PER_STEP
    width = nh * HEAD_DIM
    groups_a = N_HEADS_A // nh
    n_groups = (N_HEADS_A + N_HEADS_B) // nh

    def q_col(g):
        return g + jnp.where(g >= groups_a, 2 * groups_a, 0)

    return pl.pallas_call(
        functools.partial(_attn_kernel, lambda_init=lambda_init),
        grid=(bsz, n_groups, s // t),
        in_specs=[
            pl.BlockSpec((4, DIFF_HALF), lambda b, g, qi: (0, 0)),
            pl.BlockSpec((1, HEAD_DIM), lambda b, g, qi: (0, 0)),
            pl.BlockSpec((None, t, width), lambda b, g, qi: (b, qi, q_col(g))),
            pl.BlockSpec((None, s, width), lambda b, g, qi: (b, 0, q_col(g) + groups_a)),
            pl.BlockSpec((None, s, width), lambda b, g, qi: (b, 0, q_col(g) + 2 * groups_a)),
            pl.BlockSpec((nh, 2, t, t), lambda b, g, qi: (jnp.minimum(g, groups_a - 1), 0, 0, 0)),
        ],
        out_specs=pl.BlockSpec((None, t, width), lambda b, g, qi: (b, qi, g)),
        out_shape=jax.ShapeDtypeStruct((bsz, s, n_groups * width), BF16),
        scratch_shapes=[
            pltpu.VMEM((nh, s // t, HEAD_DIM + ATTN_ONES_ROWS, t), BF16),
            pltpu.VMEM((nh, 2, 1, t), F32),
            pltpu.VMEM((nh, 1, t), F32),
            pltpu.VMEM((nh, 2, HEAD_DIM + ATTN_ONES_ROWS, t), F32),
        ],
        compiler_params=_params("arbitrary", "arbitrary", "arbitrary"),
        name="even_attention",
    )(diff_lambda, subln_g.reshape(1, HEAD_DIM), h, h, h, bias_tiles)


def _silu(x):
    return x * jax.nn.sigmoid(x)


def _softplus(x):
    return jnp.maximum(x, 0.0) + jnp.log(1.0 + jnp.exp(-jnp.abs(x)))


def _ssd_kernel(z_ref, x_ref, b_ref, c_ref, dt_ref, cw_ref, cb_ref, dtb_ref, alog_ref, dskip_ref, ng_ref,
                o_ref, pad_sc, state_sc, acst_sc, dtt_sc):
    pad_sc[0:SUBLANES, :] = jnp.zeros((SUBLANES, pad_sc.shape[1]), F32)
    state_sc[...] = jnp.zeros_like(state_sc)

    def chunk_body(chunk, carry):
        _ssd_chunk(chunk, z_ref, x_ref, b_ref, c_ref, dt_ref, cw_ref, cb_ref, dtb_ref, alog_ref, dskip_ref,
                   ng_ref, o_ref, pad_sc, state_sc, acst_sc, dtt_sc)
        return carry

    lax.fori_loop(0, x_ref.shape[0] // SSD_CHUNK, chunk_body, 0)


def _ssd_chunk(chunk, z_ref, x_ref, b_ref, c_ref, dt_ref, cw_ref, cb_ref, dtb_ref, alog_ref, dskip_ref, ng_ref,
               o_ref, pad_sc, state_sc, acst_sc, dtt_sc):
    grp = pl.program_id(1)
    L = SSD_CHUNK
    P2 = 2 * SSD_HEAD_DIM
    n_conv = SSD_GROUP_DIM + 2 * SSD_D_STATE
    halo = SUBLANES
    rows = pl.ds(pl.multiple_of(chunk * L, L), L)

    pad_sc[halo:halo + L, 0:SSD_GROUP_DIM] = x_ref[rows, :]
    pad_sc[halo:halo + L, SSD_GROUP_DIM:SSD_GROUP_DIM + SSD_D_STATE] = b_ref[rows, :]
    pad_sc[halo:halo + L, SSD_GROUP_DIM + SSD_D_STATE:n_conv] = c_ref[rows, :]
    conv = cb_ref[...]
    for j in range(SSD_CONV):
        start = halo - (SSD_CONV - 1) + j
        conv = conv + cw_ref[j:j + 1, :] * pad_sc[start:start + L, :]
    pad_sc[0:halo, :] = pad_sc[L:L + halo, :]
    xbc = _silu(conv)
    xg = xbc[:, 0:SSD_GROUP_DIM]
    bg = xbc[:, SSD_GROUP_DIM:SSD_GROUP_DIM + SSD_D_STATE]
    cg = xbc[:, SSD_GROUP_DIM + SSD_D_STATE:n_conv]
    xg_bf = xg.astype(BF16)

    dt = _softplus(dt_ref[rows, :] + dtb_ref[...])
    adt = dt * (-jnp.exp(alog_ref[...]))
    row = lax.broadcasted_iota(I32, (L, L), 0)
    col = lax.broadcasted_iota(I32, (L, L), 1)
    causal = row >= col
    acs = jnp.dot(causal.astype(F32), adt, preferred_element_type=F32, precision=lax.Precision.HIGHEST)
    acst_sc[...] = acs.T
    dtt_sc[...] = dt.T

    cb = _dot_nt(cg.astype(BF16), bg.astype(BF16))
    bgt = bg.T
    cg_bf = cg.astype(BF16)
    prev = state_sc[...]
    y_off = jnp.dot(cg_bf, prev.astype(BF16), preferred_element_type=F32)

    lane = lax.broadcasted_iota(I32, (L, P2), 1)
    first_head = lane < SSD_HEAD_DIM
    lane_row = lax.broadcasted_iota(I32, (1, P2), 1) < SSD_HEAD_DIM
    y_pairs = []
    for pair in range(SSD_HEADS_PER_GROUP // 2):
        x_pair = xg_bf[:, pair * P2:(pair + 1) * P2]
        y_diag, st, e_acs, decay = [], [], [], []
        for sub in range(2):
            head = grp * SSD_HEADS_PER_GROUP + 2 * pair + sub
            acs_row = acst_sc[pl.ds(head, 1), :]
            dt_row = dtt_sc[pl.ds(head, 1), :]
            row_b = jnp.broadcast_to(acs_row, (L, L))
            col_b = row_b.T
            decay_m = jnp.exp(jnp.where(causal, col_b - row_b, -jnp.inf))
            m = (cb * decay_m * dt_row).astype(BF16)
            y_diag.append(jnp.dot(m, x_pair, preferred_element_type=F32))
            acs_last = col_b[L - 1:L, :]
            w_row = jnp.exp(acs_last - acs_row) * dt_row
            st.append(jnp.dot((bgt * w_row).astype(BF16), x_pair, preferred_element_type=F32))
            e_acs.append(jnp.exp(col_b))
            decay.append(jnp.exp(acs_last))
        cols = slice(pair * P2, (pair + 1) * P2)
        y_pair = (jnp.where(first_head, y_diag[0], y_diag[1])
                  + jnp.where(first_head, e_acs[0], e_acs[1]) * y_off[:, cols])
        state_sc[:, cols] = (prev[:, cols] * jnp.where(lane_row, decay[0], decay[1])
                             + jnp.where(first_head, st[0], st[1]))
        y_pair = y_pair + dskip_ref[:, cols] * xg[:, cols]
        y_pairs.append(y_pair * _silu(z_ref[rows, cols]))

    y = jnp.concatenate(y_pairs, axis=-1)
    ms = jnp.mean(y * y, axis=-1, keepdims=True)
    o_ref[rows, :] = (y * lax.rsqrt(ms + LN_EPS) * ng_ref[...]).astype(o_ref.dtype)


def _ssd(h_zx, h_dt, bsz, conv_w, conv_b, dt_bias, a_log, d_skip, norm_g):
    t_tokens = h_zx.shape[0]
    n_chunks = t_tokens // bsz // SSD_CHUNK
    L = SSD_CHUNK
    gd, ns, G = SSD_GROUP_DIM, SSD_D_STATE, SSD_N_GROUPS
    n_conv = gd + 2 * ns

    def pack(p):
        xs = p[..., :SSD_D_INNER].reshape(p.shape[:-1] + (G, gd))
        bs_ = p[..., SSD_D_INNER:SSD_D_INNER + SSD_GN].reshape(p.shape[:-1] + (G, ns))
        cs = p[..., SSD_D_INNER + SSD_GN:].reshape(p.shape[:-1] + (G, ns))
        return jnp.moveaxis(jnp.concatenate([xs, bs_, cs], axis=-1), -2, 0)

    cw = pack(conv_w)
    cb = pack(conv_b.reshape(1, -1))
    pad_heads = LANES - SSD_N_HEADS
    dtb = jnp.pad(dt_bias, (0, pad_heads)).reshape(1, LANES)
    alog = jnp.pad(a_log, (0, pad_heads)).reshape(1, LANES)
    dskip = jnp.repeat(d_skip, SSD_HEAD_DIM).reshape(1, SSD_D_INNER)

    seq = n_chunks * L
    x_blk0 = SSD_D_INNER // gd
    b_blk0 = (2 * SSD_D_INNER) // ns
    c_blk0 = (2 * SSD_D_INNER + SSD_GN) // ns
    return pl.pallas_call(
        _ssd_kernel,
        grid=(bsz, G),
        in_specs=[
            pl.BlockSpec((seq, gd), lambda b, g: (b, g)),
            pl.BlockSpec((seq, gd), lambda b, g: (b, x_blk0 + g)),
            pl.BlockSpec((seq, ns), lambda b, g: (b, b_blk0 + g)),
            pl.BlockSpec((seq, ns), lambda b, g: (b, c_blk0 + g)),
            pl.BlockSpec((seq, LANES), lambda b, g: (b, 0)),
            pl.BlockSpec((None, SSD_CONV, n_conv), lambda b, g: (g, 0, 0)),
            pl.BlockSpec((None, 1, n_conv), lambda b, g: (g, 0, 0)),
            pl.BlockSpec((1, LANES), lambda b, g: (0, 0)),
            pl.BlockSpec((1, LANES), lambda b, g: (0, 0)),
            pl.BlockSpec((1, gd), lambda b, g: (0, g)),
            pl.BlockSpec((1, gd), lambda b, g: (0, g)),
        ],
        out_specs=pl.BlockSpec((seq, gd), lambda b, g: (b, g)),
        out_shape=jax.ShapeDtypeStruct((t_tokens, SSD_D_INNER), BF16),
        scratch_shapes=[
            pltpu.VMEM((SUBLANES + L, n_conv), F32),
            pltpu.VMEM((ns, gd), F32),
            pltpu.VMEM((LANES, L), F32),
            pltpu.VMEM((LANES, L), F32),
        ],
        compiler_params=_params("arbitrary", "arbitrary"),
        name="ssd",
    )(h_zx, h_zx, h_zx, h_zx, h_dt, cw, cb, dtb, alog, dskip, norm_g.reshape(1, SSD_D_INNER))


def _gelu_tanh(x):
    return 0.5 * x * (1.0 + jnp.tanh(math.sqrt(2.0 / math.pi) * (x + 0.044715 * (x * x * x))))


def _gmlp_kernel(uv_ref, lng_ref, lnb_ref, ws_ref, bs_ref, o_ref):
    L = GMLP_CHUNK
    gdim = GMLP_GROUP_DIM
    row = lax.broadcasted_iota(I32, (L, L), 0)
    col = lax.broadcasted_iota(I32, (L, L), 1)
    causal = row >= col
    for g in range(GMLP_GROUPS):
        cols = slice(g * gdim, (g + 1) * gdim)
        u = _gelu_tanh(uv_ref[:, cols])
        v = _gelu_tanh(uv_ref[:, GMLP_WIDTH + g * gdim:GMLP_WIDTH + (g + 1) * gdim])
        v = _layer_norm_rows(v, lng_ref[:, cols], lnb_ref[:, cols])
        ws = jnp.where(causal, ws_ref[g], 0.0).astype(BF16)
        sv = jnp.dot(ws, v.astype(BF16), preferred_element_type=F32) + bs_ref[:, cols]
        o_ref[:, cols] = (u * sv).astype(o_ref.dtype)


def _gmlp(h_uv, ln_g, ln_b, ws, bs):
    t_tokens = h_uv.shape[0]
    L = GMLP_CHUNK
    bs_cols = jnp.repeat(bs.T, GMLP_GROUP_DIM, axis=1)
    return pl.pallas_call(
        _gmlp_kernel,
        grid=(t_tokens // L,),
        in_specs=[
            pl.BlockSpec((L, 2 * GMLP_WIDTH), lambda i: (i, 0)),
            pl.BlockSpec((1, GMLP_WIDTH), lambda i: (0, 0)),
            pl.BlockSpec((1, GMLP_WIDTH), lambda i: (0, 0)),
            pl.BlockSpec((GMLP_GROUPS, L, L), lambda i: (0, 0, 0)),
            pl.BlockSpec((L, GMLP_WIDTH), lambda i: (0, 0)),
        ],
        out_specs=pl.BlockSpec((L, GMLP_WIDTH), lambda i: (i, 0)),
        out_shape=jax.ShapeDtypeStruct((t_tokens, GMLP_WIDTH), BF16),
        compiler_params=_params("arbitrary"),
        name="gmlp",
    )(h_uv, ln_g.reshape(1, GMLP_WIDTH), ln_b.reshape(1, GMLP_WIDTH), ws, bs_cols)


def _router_kernel(x_ref, rw_ref, rb_ref, eidx_ref, gate_ref, rank_ref, cnt_ref, carry_sc):
    tm = x_ref.shape[0]
    G, K = N_EXPERT_GROUPS, EXPERTS_PER_GROUP

    @pl.when(pl.program_id(0) == 0)
    def _():
        carry_sc[...] = jnp.zeros_like(carry_sc)

    x_hi, x_lo = _split_bf16(x_ref[...])
    w_hi, w_lo = _split_bf16(rw_ref[...])
    logits_t = (jnp.dot(x_hi, w_hi, preferred_element_type=F32) + jnp.dot(x_hi, w_lo, preferred_element_type=F32)
                + jnp.dot(x_lo, w_hi, preferred_element_type=F32))
    logits = logits_t.T[0:N_EXPERTS]
    scores = jax.nn.sigmoid(logits)
    biased = scores + rb_ref[...]
    a = [biased[k * G:(k + 1) * G] for k in range(K)]
    sc = [scores[k * G:(k + 1) * G] for k in range(K)]
    hi01, lo01 = jnp.maximum(a[0], a[1]), jnp.minimum(a[0], a[1])
    hi23, lo23 = jnp.maximum(a[2], a[3]), jnp.minimum(a[2], a[3])
    grp_score = jnp.maximum(hi01, hi23) + jnp.maximum(jnp.minimum(hi01, hi23), jnp.maximum(lo01, lo23))
    gidx = lax.broadcasted_iota(I32, (G, tm), 0)
    g_best = jnp.max(grp_score, axis=0, keepdims=True)
    g_sel = jnp.min(jnp.where(grp_score == g_best, gidx, G), axis=0, keepdims=True)
    in_grp = gidx == g_sel
    cand = [jnp.sum(jnp.where(in_grp, a[k], 0.0), axis=0, keepdims=True) for k in range(K)]
    cand_sc = [jnp.sum(jnp.where(in_grp, sc[k], 0.0), axis=0, keepdims=True) for k in range(K)]

    def first_argmax(vals):
        best, idx = vals[0], jnp.zeros((1, tm), I32)
        for k in range(1, K):
            gt = vals[k] > best
            best = jnp.where(gt, vals[k], best)
            idx = jnp.where(gt, k, idx)
        return idx

    i1 = first_argmax(cand)
    i2 = first_argmax([jnp.where(i1 == k, -jnp.inf, cand[k]) for k in range(K)])
    v1 = sum(jnp.where(i1 == k, cand_sc[k], 0.0) for k in range(K))
    v2 = sum(jnp.where(i2 == k, cand_sc[k], 0.0) for k in range(K))
    den = v1 + v2
    eidx_ref[0:1, :] = g_sel * K + i1
    eidx_ref[1:2, :] = g_sel * K + i2
    gate_ref[0:1, :] = v1 / den
    gate_ref[1:2, :] = v2 / den

    member = jnp.concatenate(
        [jnp.where(in_grp & ((i1 == k) | (i2 == k)), 1.0, 0.0) for k in range(K)], axis=0)
    trow = lax.broadcasted_iota(I32, (tm, tm), 0)
    tcol = lax.broadcasted_iota(I32, (tm, tm), 1)
    before = (trow < tcol).astype(BF16)
    prefix = jnp.dot(member.astype(BF16), before, preferred_element_type=F32) + carry_sc[...]
    r1 = sum(jnp.sum(jnp.where(in_grp & (i1 == k), prefix[k * G:(k + 1) * G], 0.0), axis=0, keepdims=True)
             for k in range(K))
    r2 = sum(jnp.sum(jnp.where(in_grp & (i2 == k), prefix[k * G:(k + 1) * G], 0.0), axis=0, keepdims=True)
             for k in range(K))
    rank_ref[0:1, :] = r1.astype(I32)
    rank_ref[1:2, :] = r2.astype(I32)
    carry_sc[...] = carry_sc[...] + jnp.sum(member, axis=1, keepdims=True)
    cnt_ref[...] = jnp.broadcast_to(carry_sc[...], cnt_ref.shape).astype(I32)


def _slot_of_expert(e):
    return (e % EXPERTS_PER_GROUP) * N_EXPERT_GROUPS + e // EXPERTS_PER_GROUP


def _expert_of_slot(r):
    return (r % N_EXPERT_GROUPS) * EXPERTS_PER_GROUP + r // N_EXPERT_GROUPS


def _route(x, router_w, router_bias):
    t_tokens, d = x.shape
    tm = ROUTER_ROWS
    slot_expert = _expert_of_slot(jnp.arange(N_EXPERTS))
    rw = jnp.pad(router_w[:, slot_expert], ((0, 0), (0, LANES - N_EXPERTS)))
    rb = router_bias[slot_expert].reshape(N_EXPERTS, 1)
    tok = pl.BlockSpec((TOP_K, tm), lambda i: (0, i))
    return pl.pallas_call(
        _router_kernel,
        grid=(t_tokens // tm,),
        in_specs=[
            pl.BlockSpec((tm, d), lambda i: (i, 0)),
            pl.BlockSpec((d, LANES), lambda i: (0, 0)),
            pl.BlockSpec((N_EXPERTS, 1), lambda i: (0, 0)),
        ],
        out_specs=[tok, tok, tok, pl.BlockSpec((N_EXPERTS, LANES), lambda i: (0, 0))],
        out_shape=[
            jax.ShapeDtypeStruct((TOP_K, t_tokens), I32),
            jax.ShapeDtypeStruct((TOP_K, t_tokens), F32),
            jax.ShapeDtypeStruct((TOP_K, t_tokens), I32),
            jax.ShapeDtypeStruct((N_EXPERTS, LANES), I32),
        ],
        scratch_shapes=[pltpu.VMEM((N_EXPERTS, 1), F32)],
        compiler_params=_params("arbitrary"),
        name="router",
    )(x, rw, rb)


def _ffn_kernel(nused_ref, iexp_ref, nval_ref, src_ref, src_next_ref, xp_hbm, wg_ref, wu_ref, wd_ref,
                yk_hbm, xbuf, hg_sc, hu_sc, hid_sc, obuf, gsem, ssem):
    item = pl.program_id(0)
    step = pl.program_id(1)
    n_used = nused_ref[0]
    slot = lax.rem(item, 2)
    sub = MOE_SUB_ROWS
    half = MOE_K_CHUNK // 2

    n_tokens = xp_hbm.shape[0]
    unroll = MOE_DMA_UNROLL

    def for_rows(count, fn):
        def body(i, carry):
            fn(i)
            return carry
        lax.fori_loop(0, count, body, 0)

    def for_rows_unrolled(count, fn):
        def body(b, carry):
            for u in range(unroll):
                fn(b * unroll + u)
            return carry
        n_full = count // unroll
        lax.fori_loop(0, n_full, body, 0)
        lax.fori_loop(n_full * unroll, count, lambda i, c: (fn(i), c)[1], 0)

    def gather_rows(slot_rows, count, to_slot):
        def start(i):
            slot_id = slot_rows[0, i]
            if n_tokens & (n_tokens - 1) == 0:
                tok = slot_id & (n_tokens - 1)
            else:
                tok = lax.rem(slot_id, n_tokens)
            pltpu.make_async_copy(xp_hbm.at[pl.ds(tok, 1), :], xbuf.at[to_slot, pl.ds(i, 1), :],
                                  gsem.at[to_slot]).start(priority=1)
        for_rows_unrolled(count, start)

    def scatter_rows(count):
        def start(i):
            pltpu.make_async_copy(obuf.at[pl.ds(i, 1), :], yk_hbm.at[pl.ds(src_ref[0, i], 1), :],
                                  ssem).start(priority=1)
        for_rows_unrolled(count, start)

    def wait_rows(count, n_rows_copy):
        n_full = count // unroll
        for_rows(n_full, lambda i: n_rows_copy(unroll).wait())
        for_rows(count - n_full * unroll, lambda i: n_rows_copy(1).wait())

    def gathered(to_slot):
        return lambda n: pltpu.make_async_copy(xp_hbm.at[pl.ds(0, n), :], xbuf.at[to_slot, pl.ds(0, n), :],
                                               gsem.at[to_slot])

    def scattered(n):
        return pltpu.make_async_copy(obuf.at[pl.ds(0, n), :], yk_hbm.at[pl.ds(0, n), :], ssem)

    @pl.when(item < n_used)
    def _():
        n_rows = nval_ref[item]
        n_sub = (n_rows + sub - 1) // sub

        @pl.when(step == 0)
        def _():
            @pl.when(item == 0)
            def _():
                xbuf[...] = jnp.zeros_like(xbuf)
                gather_rows(src_ref, n_rows, 0)

            @pl.when(item + 1 < n_used)
            def _():
                gather_rows(src_next_ref, nval_ref[item + 1], 1 - slot)

            wait_rows(n_rows, gathered(slot))

        for c in range(MOE_K_STEPS):
            @pl.when(step == c)
            def _(c=c):
                wg = wg_ref[...].astype(BF16)
                wu = wu_ref[...].astype(BF16)

                def sub_block(r):
                    rows = pl.ds(pl.multiple_of(r * sub, sub), sub)
                    lo, hi = _unpack_bf16_pair(xbuf[slot, rows, c * half:(c + 1) * half])
                    xk = jnp.concatenate([lo, hi], axis=1).astype(BF16)
                    g = jnp.dot(xk, wg, preferred_element_type=F32)
                    u = jnp.dot(xk, wu, preferred_element_type=F32)
                    if c > 0:
                        g = g + hg_sc[rows, :]
                        u = u + hu_sc[rows, :]
                    if c < MOE_K_STEPS - 1:
                        hg_sc[rows, :] = g
                        hu_sc[rows, :] = u
                    else:
                        hid_sc[rows, :] = (_silu(g) * u).astype(BF16)

                for_rows(n_sub, sub_block)

        @pl.when(step == MOE_K_STEPS)
        def _():
            @pl.when(item > 0)
            def _():
                wait_rows(nval_ref[item - 1], scattered)

            wd = wd_ref[...].astype(BF16)

            def sub_block(r):
                rows = pl.ds(pl.multiple_of(r * sub, sub), sub)
                o = jnp.dot(hid_sc[rows, :], wd, preferred_element_type=F32)
                obuf[rows, :] = lax.bitcast_convert_type(o.astype(BF16).astype(F32), U32) >> 16

            for_rows(n_sub, sub_block)

        @pl.when(step == MOE_K_STEPS + 1)
        def _():
            wd = wd_ref[...].astype(BF16)

            def sub_block(r):
                rows = pl.ds(pl.multiple_of(r * sub, sub), sub)
                o = jnp.dot(hid_sc[rows, :], wd, preferred_element_type=F32)
                obuf[rows, :] = obuf[rows, :] | (lax.bitcast_convert_type(o.astype(BF16).astype(F32), U32)
                                                 & HIGH_HALF)

            for_rows(n_sub, sub_block)
            scatter_rows(n_rows)

            @pl.when(item == n_used - 1)
            def _():
                wait_rows(n_rows, scattered)


def _expert_ffn(xp, item_expert, item_rows, n_used, row_slot, w_gate, w_up, w_down, layer):
    t_tokens, dp = xp.shape
    d = 2 * dp
    rows = MOE_ROWS
    n_items = item_expert.shape[0]
    oc = d // MOE_O_STEPS
    last = MOE_K_STEPS + MOE_O_STEPS - 1

    def pos(i, s, nu):
        used = i < nu[0]
        return jnp.where(used, i, nu[0] - 1), jnp.where(used, s, last)

    def w_in_map(i, s, nu, ie, nv):
        ii, ss = pos(i, s, nu)
        return layer, ie[ii], jnp.minimum(ss, MOE_K_STEPS - 1), 0

    def w_down_map(i, s, nu, ie, nv):
        ii, ss = pos(i, s, nu)
        early = ss < MOE_K_STEPS
        return (layer, ie[jnp.where(early, jnp.maximum(ii - 1, 0), ii)], 0,
                jnp.where(early, MOE_O_STEPS - 1, ss - MOE_K_STEPS))

    def rows_map(i, s, nu, ie, nv):
        return pos(i, s, nu)[0], 0, 0

    def next_rows_map(i, s, nu, ie, nv):
        return jnp.minimum(pos(i, s, nu)[0] + 1, n_items - 1), 0, 0

    idx_block = (None, 1, rows)
    return pl.pallas_call(
        _ffn_kernel,
        grid_spec=pltpu.PrefetchScalarGridSpec(
            num_scalar_prefetch=3,
            grid=(n_items, MOE_K_STEPS + MOE_O_STEPS),
            in_specs=[
                pl.BlockSpec(idx_block, rows_map, memory_space=pltpu.SMEM),
                pl.BlockSpec(idx_block, next_rows_map, memory_space=pltpu.SMEM),
                pl.BlockSpec(memory_space=pl.ANY),
                pl.BlockSpec((None, None, MOE_K_CHUNK, D_FF), w_in_map),
                pl.BlockSpec((None, None, MOE_K_CHUNK, D_FF), w_in_map),
                pl.BlockSpec((None, None, D_FF, oc), w_down_map),
            ],
            out_specs=pl.BlockSpec(memory_space=pl.ANY),
            scratch_shapes=[
                pltpu.VMEM((2, rows, dp), U32),
                pltpu.VMEM((rows, D_FF), F32),
                pltpu.VMEM((rows, D_FF), F32),
                pltpu.VMEM((rows, D_FF), BF16),
                pltpu.VMEM((rows, dp), U32),
                pltpu.SemaphoreType.DMA((2,)),
                pltpu.SemaphoreType.DMA(()),
            ],
        ),
        out_shape=jax.ShapeDtypeStruct((TOP_K * t_tokens, dp), U32),
        compiler_params=_params("arbitrary", "arbitrary"),
        name="moe_ffn",
    )(n_used, item_expert, item_rows, row_slot.reshape(n_items, 1, rows), row_slot.reshape(n_items, 1, rows),
      xp, w_gate, w_up, w_down)


def _combine_kernel(y0_ref, y1_ref, gate_ref, x_ref, g_ref, b_ref, o_ref, obf_ref):
    gates = gate_ref[...]
    ffn = (gates[:, 0:1] * jnp.concatenate(_unpack_bf16_pair(y0_ref[...]), axis=1)
           + gates[:, 1:2] * jnp.concatenate(_unpack_bf16_pair(y1_ref[...]), axis=1))
    y = _layer_norm_rows(ALPHA * x_ref[...] + ffn, g_ref[...], b_ref[...])
    o_ref[...] = y
    obf_ref[...] = y.astype(BF16)


def _combine_layer_norm(x, yk, gate, g, b):
    t_tokens, d = x.shape
    tm = LN_ROWS
    n_steps = t_tokens // tm
    row = pl.BlockSpec((tm, d), lambda i: (i, 0))
    vec = pl.BlockSpec((1, d), lambda i: (0, 0))
    return pl.pallas_call(
        _combine_kernel,
        grid=(n_steps,),
        in_specs=[
            pl.BlockSpec((tm, d // 2), lambda i: (i, 0)),
            pl.BlockSpec((tm, d // 2), lambda i: (i + n_steps, 0)),
            pl.BlockSpec((tm, TOP_K), lambda i: (i, 0)),
            row, vec, vec,
        ],
        out_specs=[row, row],
        out_shape=[jax.ShapeDtypeStruct((t_tokens, d), F32), jax.ShapeDtypeStruct((t_tokens, d), BF16)],
        compiler_params=_params("arbitrary"),
        name="moe_combine",
    )(yk, yk, gate.T, x, g.reshape(1, d), b.reshape(1, d))


def _moe_layer_norm(x, xp, layer, router_w, router_bias, w_gate, w_up, w_down, ln_g, ln_b):
    t_tokens, _ = x.shape
    rows = MOE_ROWS
    eidx, gate, rank, counts = _route(x, router_w, router_bias)
    cnt = counts[:, 0]
    items_per_slot = (cnt + rows - 1) // rows
    item_end = jnp.cumsum(items_per_slot)
    row_start = (item_end - items_per_slot) * rows
    in_slot = _slot_of_expert(eidx)[..., None] == jnp.arange(N_EXPERTS, dtype=I32)
    dest = (jnp.sum(jnp.where(in_slot, row_start, 0), axis=-1) + rank).reshape(-1)
    n_items = (t_tokens * TOP_K) // rows + N_EXPERTS
    n_used = item_end[-1].astype(I32).reshape(1)
    item_id = jnp.arange(n_items, dtype=I32)
    item_slot = jnp.minimum(jnp.sum(item_id[:, None] >= item_end[None, :], axis=1), N_EXPERTS - 1)
    item_expert = _expert_of_slot(item_slot).astype(I32)
    item_rows = jnp.clip(row_start[item_slot] + cnt[item_slot] - item_id * rows, 0, rows)
    item_rows = jnp.where(item_id < n_used[0], item_rows, 0).astype(I32)
    row_slot = jnp.zeros((n_items * rows,), I32).at[dest].set(jnp.arange(TOP_K * t_tokens, dtype=I32))

    yk = _expert_ffn(xp, item_expert, item_rows, n_used, row_slot, w_gate, w_up, w_down, layer)
    return _combine_layer_norm(x, yk, gate, ln_g, ln_b)


def kernel(x, rel_bias, even_w_in, even_w_out, diff_lambda, diff_subln_g, odd_w_in, odd_w_out, ssd_conv_w, ssd_conv_b, ssd_dt_bias, ssd_a_log, ssd_d, ssd_norm_g, gmlp_ln_g, gmlp_ln_b, gmlp_ws, gmlp_bs, router_w, router_bias, moe_w_gate, moe_w_up, moe_w_down, ln_mix_g, ln_mix_b, ln_ffn_g, ln_ffn_b):
    bsz, s, d = x.shape
    t_tokens = bsz * s
    xf = x.reshape(t_tokens, d)
    xb = xf.astype(BF16)
    moe_args = (router_w, router_bias, moe_w_gate, moe_w_up, moe_w_down)

    lambda_init = 0.8 - 0.6 * math.exp(-0.3 * 0)
    h = _project([xb], even_w_in[0], [0], 0, EVEN_IN, BF16, tm=1024, tn=512)
    attn = _even_attention(h.reshape(bsz, s, EVEN_IN), _bias_tiles(rel_bias), diff_lambda[0],
                           diff_subln_g[0], lambda_init)
    mix = _project([attn.reshape(t_tokens, d)], even_w_out[0], [0], 0, d, F32, tm=1024, tn=512)
    xf, xp = _residual_layer_norm(xf, mix, ln_mix_g[0], ln_mix_b[0])
    xf, xb = _moe_layer_norm(xf, xp, 0, *moe_args, ln_ffn_g[0], ln_ffn_b[0])

    zx_cols = SSD_D_INNER + SSD_CONV_DIM
    w_in_t = jnp.swapaxes(odd_w_in[0], 0, 1)
    h_zx = _project([xb], w_in_t, [0], 0, zx_cols, F32, tm=1024, tn=512, w_transposed=True)
    h_dt = _project([xb], w_in_t, [0], zx_cols, LANES, F32, tm=1024, tn=LANES, w_transposed=True)
    h_uv = _project([xb], w_in_t, [0], C_IN, 2 * GMLP_WIDTH, F32, tm=1024, tn=512, w_transposed=True)
    y_ssd = _ssd(h_zx, h_dt, bsz, ssd_conv_w[0], ssd_conv_b[0], ssd_dt_bias[0], ssd_a_log[0], ssd_d[0],
                 ssd_norm_g[0])
    y_gmlp = _gmlp(h_uv, gmlp_ln_g[0], gmlp_ln_b[0], gmlp_ws[0], gmlp_bs[0])
    mix = _project([y_ssd, y_gmlp], odd_w_out[0], [0, SSD_D_INNER // GMLP_WIDTH], 0, d, F32, tm=512, tn=512)
    xf, xp = _residual_layer_norm(xf, mix, ln_mix_g[1], ln_mix_b[1])
    xf, xb = _moe_layer_norm(xf, xp, 1, *moe_args, ln_ffn_g[1], ln_ffn_b[1])
    return xf.reshape(bsz, s, d)
```

```python
import functools
import math

import numpy as np
import jax
import jax.numpy as jnp
from jax import lax
from jax.experimental import pallas as pl
from jax.experimental.pallas import tpu as pltpu

F32 = jnp.float32
BF16 = jnp.bfloat16
I32 = jnp.int32

D_MODEL = 4096
DEPTH = 2
HEAD_DIM = 128
N_HEADS_A = 16
N_HEADS_B = 16
DIFF_HALF = HEAD_DIM // 2
N_BUCKETS = 32
MAX_DISTANCE = 128
A_IN = 3 * N_HEADS_A * HEAD_DIM
EVEN_IN = A_IN + 3 * N_HEADS_B * HEAD_DIM
DIFF_SCALE = DIFF_HALF ** -0.5
SB_SCALE = HEAD_DIM ** -0.5

SSD_HEAD_DIM = 64
SSD_D_INNER = D_MODEL
SSD_N_HEADS = SSD_D_INNER // SSD_HEAD_DIM
SSD_N_GROUPS = 8
SSD_HEADS_PER_GROUP = SSD_N_HEADS // SSD_N_GROUPS
SSD_D_STATE = 128
SSD_CONV = 4
SSD_CHUNK = 128
SSD_GN = SSD_N_GROUPS * SSD_D_STATE
SSD_CONV_DIM = SSD_D_INNER + 2 * SSD_GN
SSD_GROUP_DIM = SSD_D_INNER // SSD_N_GROUPS
C_IN = SSD_D_INNER + SSD_CONV_DIM + SSD_N_HEADS
GMLP_WIDTH = D_MODEL // 2
GMLP_GROUPS = 8
GMLP_GROUP_DIM = GMLP_WIDTH // GMLP_GROUPS
GMLP_CHUNK = 128

N_EXPERTS = 32
N_EXPERT_GROUPS = 8
EXPERTS_PER_GROUP = N_EXPERTS // N_EXPERT_GROUPS
TOP_K = 2
D_FF = 768

ALPHA = (2 * DEPTH) ** 0.25
LN_EPS = 1e-5

LANES = 128
SUBLANES = 8
VMEM_LIMIT_BYTES = 58 * 1024 * 1024

ATTN_TILE = 256
ATTN_HEADS_PER_STEP = 8
ATTN_ONES_ROWS = 16
MOE_ROWS = 768
MOE_SUB_ROWS = 256
MOE_K_STEPS = 4
MOE_K_CHUNK = D_MODEL // MOE_K_STEPS
MOE_O_STEPS = 2
MOE_DMA_UNROLL = 8
U32 = jnp.uint32
HIGH_HALF = np.uint32(0xFFFF0000)
ROUTER_ROWS = 512
LN_ROWS = 256


def _params(*sem):
    return pltpu.CompilerParams(dimension_semantics=sem, vmem_limit_bytes=VMEM_LIMIT_BYTES)


def _proj_kernel(*refs, n_in, w_transposed):
    x_refs = refs[:n_in]
    w_refs = refs[n_in:2 * n_in]
    o_ref = refs[2 * n_in]
    wbf_refs = refs[2 * n_in + 1:]

    @pl.when(pl.program_id(1) == 0)
    def _():
        for w_ref, wbf_ref in zip(w_refs, wbf_refs):
            wbf_ref[...] = w_ref[...].astype(BF16)

    mm = _dot_nt if w_transposed else functools.partial(jnp.dot, preferred_element_type=F32)
    acc = mm(x_refs[0][...], wbf_refs[0][...])
    for x_ref, wbf_ref in zip(x_refs[1:], wbf_refs[1:]):
        acc = acc + mm(x_ref[...], wbf_ref[...])
    o_ref[...] = acc.astype(o_ref.dtype)


def _project(xs, w, row_blocks, col0, n_cols, out_dtype, tm, tn, w_transposed=False):
    m = xs[0].shape[0]
    n_in = len(xs)
    in_specs = [pl.BlockSpec((tm, x.shape[1]), lambda j, i: (i, 0)) for x in xs]
    for x, rb in zip(xs, row_blocks):
        k = x.shape[1]
        if not w_transposed:
            in_specs.append(pl.BlockSpec((k, tn), lambda j, i, rb=rb: (rb, j + col0 // tn)))
        elif col0 % tn == 0:
            in_specs.append(pl.BlockSpec((tn, k), lambda j, i, rb=rb: (j + col0 // tn, rb)))
        else:
            in_specs.append(pl.BlockSpec((pl.Element(tn), pl.Element(k)),
                                         lambda j, i, rb=rb, k=k: (pl.multiple_of(col0 + j * tn, SUBLANES),
                                                                   rb * k)))
    return pl.pallas_call(
        functools.partial(_proj_kernel, n_in=n_in, w_transposed=w_transposed),
        grid=(n_cols // tn, m // tm),
        in_specs=in_specs,
        out_specs=pl.BlockSpec((tm, tn), lambda j, i: (i, j)),
        out_shape=jax.ShapeDtypeStruct((m, n_cols), out_dtype),
        scratch_shapes=[pltpu.VMEM((tn, x.shape[1]) if w_transposed else (x.shape[1], tn), BF16) for x in xs],
        compiler_params=_params("arbitrary", "arbitrary"),
        name="proj",
    )(*xs, *([w] * n_in))


def _layer_norm_rows(y, g, b):
    mu = jnp.mean(y, axis=-1, keepdims=True)
    yc = y - mu
    var = jnp.mean(yc * yc, axis=-1, keepdims=True)
    return yc * lax.rsqrt(var + LN_EPS) * g + b


def _pack_bf16_pair(lo, hi):
    lo_bits = lax.bitcast_convert_type(lo.astype(BF16).astype(F32), U32)
    hi_bits = lax.bitcast_convert_type(hi.astype(BF16).astype(F32), U32)
    return (lo_bits >> 16) | (hi_bits & HIGH_HALF)


def _unpack_bf16_pair(p):
    return (lax.bitcast_convert_type(p << 16, F32), lax.bitcast_convert_type(p & HIGH_HALF, F32))


def _pack_rows(y):
    half = MOE_K_CHUNK // 2
    parts = [_pack_bf16_pair(y[:, c * MOE_K_CHUNK:c * MOE_K_CHUNK + half],
                             y[:, c * MOE_K_CHUNK + half:(c + 1) * MOE_K_CHUNK])
             for c in range(y.shape[1] // MOE_K_CHUNK)]
    return jnp.concatenate(parts, axis=1)


def _res_ln_kernel(x_ref, mix_ref, g_ref, b_ref, o_ref, opk_ref):
    y = _layer_norm_rows(ALPHA * x_ref[...] + mix_ref[...], g_ref[...], b_ref[...])
    o_ref[...] = y
    opk_ref[...] = _pack_rows(y)


def _residual_layer_norm(x, mix, g, b):
    m, d = x.shape
    tm = LN_ROWS
    row = pl.BlockSpec((tm, d), lambda i: (i, 0))
    half_row = pl.BlockSpec((tm, d // 2), lambda i: (i, 0))
    vec = pl.BlockSpec((1, d), lambda i: (0, 0))
    return pl.pallas_call(
        _res_ln_kernel,
        grid=(m // tm,),
        in_specs=[row, row, vec, vec],
        out_specs=[row, half_row],
        out_shape=[jax.ShapeDtypeStruct((m, d), F32), jax.ShapeDtypeStruct((m, d // 2), U32)],
        compiler_params=_params("arbitrary"),
        name="res_ln",
    )(x, mix, g.reshape(1, d), b.reshape(1, d))


def _bias_tile_kernel(rb_ref, o_ref):
    head = pl.program_id(0)
    t = ATTN_TILE
    key = lax.broadcasted_iota(I32, (t, t), 0)
    qry = lax.broadcasted_iota(I32, (t, t), 1)
    max_exact = N_BUCKETS // 2
    for d in range(2):
        n = jnp.maximum(d * t + qry - key, 0)
        nf = jnp.maximum(n, 1).astype(F32)
        large = max_exact + (jnp.log(nf / max_exact) / math.log(MAX_DISTANCE / max_exact)
                             * (N_BUCKETS - max_exact)).astype(I32)
        large = jnp.minimum(large, N_BUCKETS - 1)
        bucket = jnp.where(n < max_exact, n, large)
        acc = jnp.zeros((t, t), F32)
        for bkt in range(N_BUCKETS):
            acc = jnp.where(bucket == bkt, rb_ref[bkt, head], acc)
        o_ref[d] = acc


def _bias_tiles(rel_bias):
    t = ATTN_TILE
    return pl.pallas_call(
        _bias_tile_kernel,
        grid=(N_HEADS_A,),
        in_specs=[pl.BlockSpec(memory_space=pltpu.SMEM)],
        out_specs=pl.BlockSpec((None, 2, t, t), lambda h: (h, 0, 0, 0)),
        out_shape=jax.ShapeDtypeStruct((N_HEADS_A, 2, t, t), F32),
        compiler_params=_params("arbitrary"),
        name="bias_tiles",
    )(rel_bias)


def _dot_nt(a, b):
    return lax.dot_general(a, b, (((1,), (1,)), ((), ())), preferred_element_type=F32)


def _head_cols(h):
    return slice(h * HEAD_DIM, (h + 1) * HEAD_DIM)


def _value_transposes(v_ref, vt_sc):
    t = ATTN_TILE
    ones = jnp.ones((ATTN_ONES_ROWS, t), BF16)
    for h in range(ATTN_HEADS_PER_STEP):
        for j in range(v_ref.shape[0] // t):
            vb = v_ref[j * t:(j + 1) * t, _head_cols(h)]
            vt_sc[h, j, 0:HEAD_DIM, :] = vb.astype(F32).T.astype(BF16)
            vt_sc[h, j, HEAD_DIM:, :] = ones


def _diff_attention(dl_ref, g_ref, q_ref, k_ref, vt_sc, bias_ref, o_ref, m_sc, acc_sc, lambda_init):
    t = ATTN_TILE
    qi = pl.program_id(2)
    lane = lax.broadcasted_iota(I32, (t, HEAD_DIM), 1)
    q_maps = []
    for h in range(ATTN_HEADS_PER_STEP):
        q = q_ref[:, _head_cols(h)] * DIFF_SCALE
        zero = jnp.zeros_like(q)
        q_maps.append((jnp.where(lane < DIFF_HALF, q, zero), jnp.where(lane >= DIFF_HALF, q, zero)))

    chains = [(h, mp) for h in range(ATTN_HEADS_PER_STEP) for mp in range(2)]

    def block(j, bias_of_head, mask, first):
        kv_start = pl.multiple_of(j * t, t)
        kbs = [k_ref[pl.ds(kv_start, t), _head_cols(h)] for h in range(ATTN_HEADS_PER_STEP)]
        scores = [_dot_nt(kbs[h], q_maps[h][mp]) for h, mp in chains]
        probs, alphas = [], []
        for (h, mp), s in zip(chains, scores):
            bias = bias_of_head(h)
            uniform = bias.shape == (1, 1)
            if not uniform:
                s = s + bias
            if mask is not None:
                s = jnp.where(mask, s, -jnp.inf)
            m_new = jnp.max(s, axis=0, keepdims=True)
            if uniform:
                m_new = m_new + bias
            if not first:
                m_old = m_sc[h, mp]
                m_new = jnp.maximum(m_old, m_new)
                alphas.append(jnp.exp(m_old - m_new))
            m_sc[h, mp] = m_new
            probs.append(jnp.exp(s - (m_new - bias if uniform else m_new)).astype(BF16))
        for i, (h, mp) in enumerate(chains):
            pv = jnp.dot(vt_sc[h, j], probs[i], preferred_element_type=F32)
            acc_sc[h, mp] = pv if first else alphas[i] * acc_sc[h, mp] + pv

    key = lax.broadcasted_iota(I32, (t, t), 0)
    qry = lax.broadcasted_iota(I32, (t, t), 1)
    block(qi, lambda h: bias_ref[h, 0], qry >= key, True)

    @pl.when(qi >= 1)
    def _():
        block(qi - 1, lambda h: bias_ref[h, 1], None, False)

    def far_body(j, carry):
        block(j, lambda h: bias_ref[h, 1, 0:1, t - 1:t], None, False)
        return carry

    lax.fori_loop(0, qi - 1, far_body, 0)

    dl = dl_ref[...]
    lam = (jnp.exp(jnp.sum(dl[0:1] * dl[1:2], axis=-1, keepdims=True))
           - jnp.exp(jnp.sum(dl[2:3] * dl[3:4], axis=-1, keepdims=True)) + lambda_init)
    for h in range(ATTN_HEADS_PER_STEP):
        a0 = acc_sc[h, 0]
        a1 = acc_sc[h, 1]
        oa = (a0[0:HEAD_DIM] / a0[HEAD_DIM:HEAD_DIM + 1]
              - lam * (a1[0:HEAD_DIM] / a1[HEAD_DIM:HEAD_DIM + 1]))
        ms = jnp.mean(oa * oa, axis=0, keepdims=True)
        oa = (oa * lax.rsqrt(ms + LN_EPS)).T * g_ref[...] * (1.0 - lambda_init)
        o_ref[:, _head_cols(h)] = oa.astype(o_ref.dtype)


def _log_sigmoid(z):
    return jnp.minimum(z, 0.0) - jnp.log(1.0 + jnp.exp(-jnp.abs(z)))


def _split_bf16(x):
    hi = x.astype(BF16)
    lo = (x - hi.astype(F32)).astype(BF16)
    return hi, lo


def _stick_breaking_attention(q_ref, k_ref, vt_sc, o_ref, c_sc, acc_sc):
    t = ATTN_TILE
    qi = pl.program_id(2)
    key = lax.broadcasted_iota(I32, (t, t), 0)
    qry = lax.broadcasted_iota(I32, (t, t), 1)
    later = qry > key
    after = later.astype(BF16)

    heads = range(ATTN_HEADS_PER_STEP)

    def block(j, strict, first):
        kv_start = pl.multiple_of(j * t, t)
        zs = [_dot_nt(k_ref[pl.ds(kv_start, t), _head_cols(h)], q_ref[:, _head_cols(h)]) for h in heads]
        log_betas, splits, col_sums = [], [], []
        for h in heads:
            z = zs[h] * SB_SCALE
            log_beta = _log_sigmoid(z)
            log_1mb = log_beta - z
            if strict is not None:
                log_1mb = jnp.where(strict, log_1mb, 0.0)
            log_betas.append(log_beta)
            splits.append(_split_bf16(log_1mb))
            col_sums.append(jnp.sum(log_1mb, axis=0, keepdims=True))
        tails = [jnp.dot(after, hi, preferred_element_type=F32) + jnp.dot(after, lo, preferred_element_type=F32)
                 for hi, lo in splits]
        weights = []
        for h in heads:
            tail = tails[h] if first else tails[h] + c_sc[h]
            w = jnp.exp(log_betas[h] + tail)
            if strict is not None:
                w = jnp.where(strict, w, 0.0)
            weights.append(w.astype(BF16))
            c_sc[h] = col_sums[h] if first else c_sc[h] + col_sums[h]
        for h in heads:
            pv = jnp.dot(vt_sc[h, j][0:HEAD_DIM], weights[h], preferred_element_type=F32)
            acc_sc[h, 0, 0:HEAD_DIM] = pv if first else acc_sc[h, 0, 0:HEAD_DIM] + pv

    block(qi, later, True)

    def body(step, carry):
        block(qi - 1 - step, None, False)
        return carry

    lax.fori_loop(0, qi, body, 0)
    for h in range(ATTN_HEADS_PER_STEP):
        o_ref[:, _head_cols(h)] = acc_sc[h, 0, 0:HEAD_DIM].T.astype(o_ref.dtype)


def _attn_kernel(dl_ref, g_ref, q_ref, k_ref, v_ref, bias_ref, o_ref, vt_sc, m_sc, c_sc, acc_sc, *, lambda_init):
    grp = pl.program_id(1)

    @pl.when(pl.program_id(2) == 0)
    def _():
        _value_transposes(v_ref, vt_sc)

    @pl.when(grp < N_HEADS_A // ATTN_HEADS_PER_STEP)
    def _():
        _diff_attention(dl_ref, g_ref, q_ref, k_ref, vt_sc, bias_ref, o_ref, m_sc, acc_sc, lambda_init)

    @pl.when(grp >= N_HEADS_A // ATTN_HEADS_PER_STEP)
    def _():
        _stick_breaking_attention(q_ref, k_ref, vt_sc, o_ref, c_sc, acc_sc)


def _even_attention(h, bias_tiles, diff_lambda, subln_g, lambda_init):
    bsz, s, _ = h.shape
    t = ATTN_TILE
    nh = ATTN_HEADS_PER_STEP
    width = nh * HEAD_DIM
    groups_a = N_HEADS_A // nh
    n_groups = (N_HEADS_A + N_HEADS_B) // nh

    def q_col(g):
        return g + jnp.where(g >= groups_a, 2 * groups_a, 0)

    return pl.pallas_call(
        functools.partial(_attn_kernel, lambda_init=lambda_init),
        grid=(bsz, n_groups, s // t),
        in_specs=[
            pl.BlockSpec((4, DIFF_HALF), lambda b, g, qi: (0, 0)),
            pl.BlockSpec((1, HEAD_DIM), lambda b, g, qi: (0, 0)),
            pl.BlockSpec((None, t, width), lambda b, g, qi: (b, qi, q_col(g))),
            pl.BlockSpec((None, s, width), lambda b, g, qi: (b, 0, q_col(g) + groups_a)),
            pl.BlockSpec((None, s, width), lambda b, g, qi: (b, 0, q_col(g) + 2 * groups_a)),
            pl.BlockSpec((nh, 2, t, t), lambda b, g, qi: (jnp.minimum(g, groups_a - 1), 0, 0, 0)),
        ],
        out_specs=pl.BlockSpec((None, t, width), lambda b, g, qi: (b, qi, g)),
        out_shape=jax.ShapeDtypeStruct((bsz, s, n_groups * width), BF16),
        scratch_shapes=[
            pltpu.VMEM((nh, s // t, HEAD_DIM + ATTN_ONES_ROWS, t), BF16),
            pltpu.VMEM((nh, 2, 1, t), F32),
            pltpu.VMEM((nh, 1, t), F32),
            pltpu.VMEM((nh, 2, HEAD_DIM + ATTN_ONES_ROWS, t), F32),
        ],
        compiler_params=_params("arbitrary", "arbitrary", "arbitrary"),
        name="even_attention",
    )(diff_lambda, subln_g.reshape(1, HEAD_DIM), h, h, h, bias_tiles)


def _silu(x):
    return x * jax.nn.sigmoid(x)


def _softplus(x):
    return jnp.maximum(x, 0.0) + jnp.log(1.0 + jnp.exp(-jnp.abs(x)))


def _ssd_kernel(z_ref, x_ref, b_ref, c_ref, dt_ref, cw_ref, cb_ref, dtb_ref, alog_ref, dskip_ref, ng_ref,
                o_ref, pad_sc, state_sc, acst_sc, dtt_sc):
    pad_sc[0:SUBLANES, :] = jnp.zeros((SUBLANES, pad_sc.shape[1]), F32)
    state_sc[...] = jnp.zeros_like(state_sc)

    def chunk_body(chunk, carry):
        _ssd_chunk(chunk, z_ref, x_ref, b_ref, c_ref, dt_ref, cw_ref, cb_ref, dtb_ref, alog_ref, dskip_ref,
                   ng_ref, o_ref, pad_sc, state_sc, acst_sc, dtt_sc)
        return carry

    lax.fori_loop(0, x_ref.shape[0] // SSD_CHUNK, chunk_body, 0)


def _ssd_chunk(chunk, z_ref, x_ref, b_ref, c_ref, dt_ref, cw_ref, cb_ref, dtb_ref, alog_ref, dskip_ref, ng_ref,
               o_ref, pad_sc, state_sc, acst_sc, dtt_sc):
    grp = pl.program_id(1)
    L = SSD_CHUNK
    P2 = 2 * SSD_HEAD_DIM
    n_conv = SSD_GROUP_DIM + 2 * SSD_D_STATE
    halo = SUBLANES
    rows = pl.ds(pl.multiple_of(chunk * L, L), L)

    pad_sc[halo:halo + L, 0:SSD_GROUP_DIM] = x_ref[rows, :]
    pad_sc[halo:halo + L, SSD_GROUP_DIM:SSD_GROUP_DIM + SSD_D_STATE] = b_ref[rows, :]
    pad_sc[halo:halo + L, SSD_GROUP_DIM + SSD_D_STATE:n_conv] = c_ref[rows, :]
    conv = cb_ref[...]
    for j in range(SSD_CONV):
        start = halo - (SSD_CONV - 1) + j
        conv = conv + cw_ref[j:j + 1, :] * pad_sc[start:start + L, :]
    pad_sc[0:halo, :] = pad_sc[L:L + halo, :]
    xbc = _silu(conv)
    xg = xbc[:, 0:SSD_GROUP_DIM]
    bg = xbc[:, SSD_GROUP_DIM:SSD_GROUP_DIM + SSD_D_STATE]
    cg = xbc[:, SSD_GROUP_DIM + SSD_D_STATE:n_conv]
    xg_bf = xg.astype(BF16)

    dt = _softplus(dt_ref[rows, :] + dtb_ref[...])
    adt = dt * (-jnp.exp(alog_ref[...]))
    row = lax.broadcasted_iota(I32, (L, L), 0)
    col = lax.broadcasted_iota(I32, (L, L), 1)
    causal = row >= col
    acs = jnp.dot(causal.astype(F32), adt, preferred_element_type=F32, precision=lax.Precision.HIGHEST)
    acst_sc[...] = acs.T
    dtt_sc[...] = dt.T

    cb = _dot_nt(cg.astype(BF16), bg.astype(BF16))
    bgt = bg.T
    cg_bf = cg.astype(BF16)
    prev = state_sc[...]
    y_off = jnp.dot(cg_bf, prev.astype(BF16), preferred_element_type=F32)

    lane = lax.broadcasted_iota(I32, (L, P2), 1)
    first_head = lane < SSD_HEAD_DIM
    lane_row = lax.broadcasted_iota(I32, (1, P2), 1) < SSD_HEAD_DIM
    y_pairs = []
    for pair in range(SSD_HEADS_PER_GROUP // 2):
        x_pair = xg_bf[:, pair * P2:(pair + 1) * P2]
        y_diag, st, e_acs, decay = [], [], [], []
        for sub in range(2):
            head = grp * SSD_HEADS_PER_GROUP + 2 * pair + sub
            acs_row = acst_sc[pl.ds(head, 1), :]
            dt_row = dtt_sc[pl.ds(head, 1), :]
            row_b = jnp.broadcast_to(acs_row, (L, L))
            col_b = row_b.T
            decay_m = jnp.exp(jnp.where(causal, col_b - row_b, -jnp.inf))
            m = (cb * decay_m * dt_row).astype(BF16)
            y_diag.append(jnp.dot(m, x_pair, preferred_element_type=F32))
            acs_last = col_b[L - 1:L, :]
            w_row = jnp.exp(acs_last - acs_row) * dt_row
            st.append(jnp.dot((bgt * w_row).astype(BF16), x_pair, preferred_element_type=F32))
            e_acs.append(jnp.exp(col_b))
            decay.append(jnp.exp(acs_last))
        cols = slice(pair * P2, (pair + 1) * P2)
        y_pair = (jnp.where(first_head, y_diag[0], y_diag[1])
                  + jnp.where(first_head, e_acs[0], e_acs[1]) * y_off[:, cols])
        state_sc[:, cols] = (prev[:, cols] * jnp.where(lane_row, decay[0], decay[1])
                             + jnp.where(first_head, st[0], st[1]))
        y_pair = y_pair + dskip_ref[:, cols] * xg[:, cols]
        y_pairs.append(y_pair * _silu(z_ref[rows, cols]))

    y = jnp.concatenate(y_pairs, axis=-1)
    ms = jnp.mean(y * y, axis=-1, keepdims=True)
    o_ref[rows, :] = (y * lax.rsqrt(ms + LN_EPS) * ng_ref[...]).astype(o_ref.dtype)


def _ssd(h_zx, h_dt, bsz, conv_w, conv_b, dt_bias, a_log, d_skip, norm_g):
    t_tokens = h_zx.shape[0]
    n_chunks = t_tokens // bsz // SSD_CHUNK
    L = SSD_CHUNK
    gd, ns, G = SSD_GROUP_DIM, SSD_D_STATE, SSD_N_GROUPS
    n_conv = gd + 2 * ns

    def pack(p):
        xs = p[..., :SSD_D_INNER].reshape(p.shape[:-1] + (G, gd))
        bs_ = p[..., SSD_D_INNER:SSD_D_INNER + SSD_GN].reshape(p.shape[:-1] + (G, ns))
        cs = p[..., SSD_D_INNER + SSD_GN:].reshape(p.shape[:-1] + (G, ns))
        return jnp.moveaxis(jnp.concatenate([xs, bs_, cs], axis=-1), -2, 0)

    cw = pack(conv_w)
    cb = pack(conv_b.reshape(1, -1))
    pad_heads = LANES - SSD_N_HEADS
    dtb = jnp.pad(dt_bias, (0, pad_heads)).reshape(1, LANES)
    alog = jnp.pad(a_log, (0, pad_heads)).reshape(1, LANES)
    dskip = jnp.repeat(d_skip, SSD_HEAD_DIM).reshape(1, SSD_D_INNER)

    seq = n_chunks * L
    x_blk0 = SSD_D_INNER // gd
    b_blk0 = (2 * SSD_D_INNER) // ns
    c_blk0 = (2 * SSD_D_INNER + SSD_GN) // ns
    return pl.pallas_call(
        _ssd_kernel,
        grid=(bsz, G),
        in_specs=[
            pl.BlockSpec((seq, gd), lambda b, g: (b, g)),
            pl.BlockSpec((seq, gd), lambda b, g: (b, x_blk0 + g)),
            pl.BlockSpec((seq, ns), lambda b, g: (b, b_blk0 + g)),
            pl.BlockSpec((seq, ns), lambda b, g: (b, c_blk0 + g)),
            pl.BlockSpec((seq, LANES), lambda b, g: (b, 0)),
            pl.BlockSpec((None, SSD_CONV, n_conv), lambda b, g: (g, 0, 0)),
            pl.BlockSpec((None, 1, n_conv), lambda b, g: (g, 0, 0)),
            pl.BlockSpec((1, LANES), lambda b, g: (0, 0)),
            pl.BlockSpec((1, LANES), lambda b, g: (0, 0)),
            pl.BlockSpec((1, gd), lambda b, g: (0, g)),
            pl.BlockSpec((1, gd), lambda b, g: (0, g)),
        ],
        out_specs=pl.BlockSpec((seq, gd), lambda b, g: (b, g)),
        out_shape=jax.ShapeDtypeStruct((t_tokens, SSD_D_INNER), BF16),
        scratch_shapes=[
            pltpu.VMEM((SUBLANES + L, n_conv), F32),
            pltpu.VMEM((ns, gd), F32),
            pltpu.VMEM((LANES, L), F32),
            pltpu.VMEM((LANES, L), F32),
        ],
        compiler_params=_params("arbitrary", "arbitrary"),
        name="ssd",
    )(h_zx, h_zx, h_zx, h_zx, h_dt, cw, cb, dtb, alog, dskip, norm_g.reshape(1, SSD_D_INNER))


def _gelu_tanh(x):
    return 0.5 * x * (1.0 + jnp.tanh(math.sqrt(2.0 / math.pi) * (x + 0.044715 * (x * x * x))))


def _gmlp_kernel(uv_ref, lng_ref, lnb_ref, ws_ref, bs_ref, o_ref):
    L = GMLP_CHUNK
    gdim = GMLP_GROUP_DIM
    row = lax.broadcasted_iota(I32, (L, L), 0)
    col = lax.broadcasted_iota(I32, (L, L), 1)
    causal = row >= col
    for g in range(GMLP_GROUPS):
        cols = slice(g * gdim, (g + 1) * gdim)
        u = _gelu_tanh(uv_ref[:, cols])
        v = _gelu_tanh(uv_ref[:, GMLP_WIDTH + g * gdim:GMLP_WIDTH + (g + 1) * gdim])
        v = _layer_norm_rows(v, lng_ref[:, cols], lnb_ref[:, cols])
        ws = jnp.where(causal, ws_ref[g], 0.0).astype(BF16)
        sv = jnp.dot(ws, v.astype(BF16), preferred_element_type=F32) + bs_ref[:, cols]
        o_ref[:, cols] = (u * sv).astype(o_ref.dtype)


def _gmlp(h_uv, ln_g, ln_b, ws, bs):
    t_tokens = h_uv.shape[0]
    L = GMLP_CHUNK
    bs_cols = jnp.repeat(bs.T, GMLP_GROUP_DIM, axis=1)
    return pl.pallas_call(
        _gmlp_kernel,
        grid=(t_tokens // L,),
        in_specs=[
            pl.BlockSpec((L, 2 * GMLP_WIDTH), lambda i: (i, 0)),
            pl.BlockSpec((1, GMLP_WIDTH), lambda i: (0, 0)),
            pl.BlockSpec((1, GMLP_WIDTH), lambda i: (0, 0)),
            pl.BlockSpec((GMLP_GROUPS, L, L), lambda i: (0, 0, 0)),
            pl.BlockSpec((L, GMLP_WIDTH), lambda i: (0, 0)),
        ],
        out_specs=pl.BlockSpec((L, GMLP_WIDTH), lambda i: (i, 0)),
        out_shape=jax.ShapeDtypeStruct((t_tokens, GMLP_WIDTH), BF16),
        compiler_params=_params("arbitrary"),
        name="gmlp",
    )(h_uv, ln_g.reshape(1, GMLP_WIDTH), ln_b.reshape(1, GMLP_WIDTH), ws, bs_cols)


def _router_kernel(x_ref, rw_ref, rb_ref, eidx_ref, gate_ref, rank_ref, cnt_ref, carry_sc):
    tm = x_ref.shape[0]
    G, K = N_EXPERT_GROUPS, EXPERTS_PER_GROUP

    @pl.when(pl.program_id(0) == 0)
    def _():
        carry_sc[...] = jnp.zeros_like(carry_sc)

    x_hi, x_lo = _split_bf16(x_ref[...])
    w_hi, w_lo = _split_bf16(rw_ref[...])
    logits_t = (jnp.dot(x_hi, w_hi, preferred_element_type=F32) + jnp.dot(x_hi, w_lo, preferred_element_type=F32)
                + jnp.dot(x_lo, w_hi, preferred_element_type=F32))
    logits = logits_t.T[0:N_EXPERTS]
    scores = jax.nn.sigmoid(logits)
    biased = scores + rb_ref[...]
    a = [biased[k * G:(k + 1) * G] for k in range(K)]
    sc = [scores[k * G:(k + 1) * G] for k in range(K)]
    hi01, lo01 = jnp.maximum(a[0], a[1]), jnp.minimum(a[0], a[1])
    hi23, lo23 = jnp.maximum(a[2], a[3]), jnp.minimum(a[2], a[3])
    grp_score = jnp.maximum(hi01, hi23) + jnp.maximum(jnp.minimum(hi01, hi23), jnp.maximum(lo01, lo23))
    gidx = lax.broadcasted_iota(I32, (G, tm), 0)
    g_best = jnp.max(grp_score, axis=0, keepdims=True)
    g_sel = jnp.min(jnp.where(grp_score == g_best, gidx, G), axis=0, keepdims=True)
    in_grp = gidx == g_sel
    cand = [jnp.sum(jnp.where(in_grp, a[k], 0.0), axis=0, keepdims=True) for k in range(K)]
    cand_sc = [jnp.sum(jnp.where(in_grp, sc[k], 0.0), axis=0, keepdims=True) for k in range(K)]

    def first_argmax(vals):
        best, idx = vals[0], jnp.zeros((1, tm), I32)
        for k in range(1, K):
            gt = vals[k] > best
            best = jnp.where(gt, vals[k], best)
            idx = jnp.where(gt, k, idx)
        return idx

    i1 = first_argmax(cand)
    i2 = first_argmax([jnp.where(i1 == k, -jnp.inf, cand[k]) for k in range(K)])
    v1 = sum(jnp.where(i1 == k, cand_sc[k], 0.0) for k in range(K))
    v2 = sum(jnp.where(i2 == k, cand_sc[k], 0.0) for k in range(K))
    den = v1 + v2
    eidx_ref[0:1, :] = g_sel * K + i1
    eidx_ref[1:2, :] = g_sel * K + i2
    gate_ref[0:1, :] = v1 / den
    gate_ref[1:2, :] = v2 / den

    member = jnp.concatenate(
        [jnp.where(in_grp & ((i1 == k) | (i2 == k)), 1.0, 0.0) for k in range(K)], axis=0)
    trow = lax.broadcasted_iota(I32, (tm, tm), 0)
    tcol = lax.broadcasted_iota(I32, (tm, tm), 1)
    before = (trow < tcol).astype(BF16)
    prefix = jnp.dot(member.astype(BF16), before, preferred_element_type=F32) + carry_sc[...]
    r1 = sum(jnp.sum(jnp.where(in_grp & (i1 == k), prefix[k * G:(k + 1) * G], 0.0), axis=0, keepdims=True)
             for k in range(K))
    r2 = sum(jnp.sum(jnp.where(in_grp & (i2 == k), prefix[k * G:(k + 1) * G], 0.0), axis=0, keepdims=True)
             for k in range(K))
    rank_ref[0:1, :] = r1.astype(I32)
    rank_ref[1:2, :] = r2.astype(I32)
    carry_sc[...] = carry_sc[...] + jnp.sum(member, axis=1, keepdims=True)
    cnt_ref[...] = jnp.broadcast_to(carry_sc[...], cnt_ref.shape).astype(I32)


def _slot_of_expert(e):
    return (e % EXPERTS_PER_GROUP) * N_EXPERT_GROUPS + e // EXPERTS_PER_GROUP


def _expert_of_slot(r):
    return (r % N_EXPERT_GROUPS) * EXPERTS_PER_GROUP + r // N_EXPERT_GROUPS


def _route(x, router_w, router_bias):
    t_tokens, d = x.shape
    tm = ROUTER_ROWS
    slot_expert = _expert_of_slot(jnp.arange(N_EXPERTS))
    rw = jnp.pad(router_w[:, slot_expert], ((0, 0), (0, LANES - N_EXPERTS)))
    rb = router_bias[slot_expert].reshape(N_EXPERTS, 1)
    tok = pl.BlockSpec((TOP_K, tm), lambda i: (0, i))
    return pl.pallas_call(
        _router_kernel,
        grid=(t_tokens // tm,),
        in_specs=[
            pl.BlockSpec((tm, d), lambda i: (i, 0)),
            pl.BlockSpec((d, LANES), lambda i: (0, 0)),
            pl.BlockSpec((N_EXPERTS, 1), lambda i: (0, 0)),
        ],
        out_specs=[tok, tok, tok, pl.BlockSpec((N_EXPERTS, LANES), lambda i: (0, 0))],
        out_shape=[
            jax.ShapeDtypeStruct((TOP_K, t_tokens), I32),
            jax.ShapeDtypeStruct((TOP_K, t_tokens), F32),
            jax.ShapeDtypeStruct((TOP_K, t_tokens), I32),
            jax.ShapeDtypeStruct((N_EXPERTS, LANES), I32),
        ],
        scratch_shapes=[pltpu.VMEM((N_EXPERTS, 1), F32)],
        compiler_params=_params("arbitrary"),
        name="router",
    )(x, rw, rb)


def _ffn_kernel(nused_ref, iexp_ref, nval_ref, src_ref, src_next_ref, xp_hbm, wg_ref, wu_ref, wd_ref,
                yk_hbm, xbuf, hg_sc, hu_sc, hid_sc, obuf, gsem, ssem):
    item = pl.program_id(0)
    step = pl.program_id(1)
    n_used = nused_ref[0]
    slot = lax.rem(item, 2)
    sub = MOE_SUB_ROWS
    half = MOE_K_CHUNK // 2

    n_tokens = xp_hbm.shape[0]
    unroll = MOE_DMA_UNROLL

    def for_rows(count, fn):
        def body(i, carry):
            fn(i)
            return carry
        lax.fori_loop(0, count, body, 0)

    def for_rows_unrolled(count, fn):
        def body(b, carry):
            base = pl.multiple_of(b * unroll, unroll)
            for u in range(unroll):
                fn(base + u)
            return carry
        n_full = count // unroll
        lax.fori_loop(0, n_full, body, 0)
        lax.fori_loop(n_full * unroll, count, lambda i, c: (fn(i), c)[1], 0)

    def gather_rows(slot_rows, count, to_slot):
        def start(i):
            slot_id = slot_rows[0, i]
            if n_tokens & (n_tokens - 1) == 0:
                tok = slot_id & (n_tokens - 1)
            else:
                tok = lax.rem(slot_id, n_tokens)
            pltpu.make_async_copy(xp_hbm.at[pl.ds(tok, 1), :], xbuf.at[to_slot, pl.ds(i, 1), :],
                                  gsem.at[to_slot]).start(priority=1)
        for_rows_unrolled(count, start)

    def scatter_rows(count):
        def start(i):
            pltpu.make_async_copy(obuf.at[pl.ds(i, 1), :], yk_hbm.at[pl.ds(src_ref[0, i], 1), :],
                                  ssem).start(priority=1)
        for_rows_unrolled(count, start)

    def wait_rows(count, n_rows_copy):
        n_full = count // unroll
        for_rows(n_full, lambda i: n_rows_copy(unroll).wait())
        for_rows(count - n_full * unroll, lambda i: n_rows_copy(1).wait())

    def gathered(to_slot):
        return lambda n: pltpu.make_async_copy(xp_hbm.at[pl.ds(0, n), :], xbuf.at[to_slot, pl.ds(0, n), :],
                                               gsem.at[to_slot])

    def scattered(n):
        return pltpu.make_async_copy(obuf.at[pl.ds(0, n), :], yk_hbm.at[pl.ds(0, n), :], ssem)

    @pl.when(item < n_used)
    def _():
        n_rows = nval_ref[item]
        n_sub = (n_rows + sub - 1) // sub

        @pl.when(step == 0)
        def _():
            @pl.when(item == 0)
            def _():
                xbuf[...] = jnp.zeros_like(xbuf)
                gather_rows(src_ref, n_rows, 0)

            @pl.when(item + 1 < n_used)
            def _():
                gather_rows(src_next_ref, nval_ref[item + 1], 1 - slot)

            wait_rows(n_rows, gathered(slot))

        for c in range(MOE_K_STEPS):
            @pl.when(step == c)
            def _(c=c):
                wg = wg_ref[...].astype(BF16)
                wu = wu_ref[...].astype(BF16)

                def sub_block(r):
                    rows = pl.ds(pl.multiple_of(r * sub, sub), sub)
                    lo, hi = _unpack_bf16_pair(xbuf[slot, rows, c * half:(c + 1) * half])
                    xk = jnp.concatenate([lo, hi], axis=1).astype(BF16)
                    g = jnp.dot(xk, wg, preferred_element_type=F32)
                    u = jnp.dot(xk, wu, preferred_element_type=F32)
                    if c > 0:
                        g = g + hg_sc[rows, :]
                        u = u + hu_sc[rows, :]
                    if c < MOE_K_STEPS - 1:
                        hg_sc[rows, :] = g
                        hu_sc[rows, :] = u
                    else:
                        hid_sc[rows, :] = (_silu(g) * u).astype(BF16)

                for_rows(n_sub, sub_block)

        @pl.when(step == MOE_K_STEPS)
        def _():
            @pl.when(item > 0)
            def _():
                wait_rows(nval_ref[item - 1], scattered)

            wd = wd_ref[...].astype(BF16)

            def sub_block(r):
                rows = pl.ds(pl.multiple_of(r * sub, sub), sub)
                o = jnp.dot(hid_sc[rows, :], wd, preferred_element_type=F32)
                obuf[rows, :] = lax.bitcast_convert_type(o.astype(BF16).astype(F32), U32) >> 16

            for_rows(n_sub, sub_block)

        @pl.when(step == MOE_K_STEPS + 1)
        def _():
            wd = wd_ref[...].astype(BF16)

            def sub_block(r):
                rows = pl.ds(pl.multiple_of(r * sub, sub), sub)
                o = jnp.dot(hid_sc[rows, :], wd, preferred_element_type=F32)
                obuf[rows, :] = obuf[rows, :] | (lax.bitcast_convert_type(o.astype(BF16).astype(F32), U32)
                                                 & HIGH_HALF)

            for_rows(n_sub, sub_block)
            scatter_rows(n_rows)

            @pl.when(item == n_used - 1)
            def _():
                wait_rows(n_rows, scattered)


def _expert_ffn(xp, item_expert, item_rows, n_used, row_slot, w_gate, w_up, w_down, layer):
    t_tokens, dp = xp.shape
    d = 2 * dp
    rows = MOE_ROWS
    n_items = item_expert.shape[0]
    oc = d // MOE_O_STEPS
    last = MOE_K_STEPS + MOE_O_STEPS - 1

    def pos(i, s, nu):
        used = i < nu[0]
        return jnp.where(used, i, nu[0] - 1), jnp.where(used, s, last)

    def w_in_map(i, s, nu, ie, nv):
        ii, ss = pos(i, s, nu)
        return layer, ie[ii], jnp.minimum(ss, MOE_K_STEPS - 1), 0

    def w_down_map(i, s, nu, ie, nv):
        ii, ss = pos(i, s, nu)
        early = ss < MOE_K_STEPS
        return (layer, ie[jnp.where(early, jnp.maximum(ii - 1, 0), ii)], 0,
                jnp.where(early, MOE_O_STEPS - 1, ss - MOE_K_STEPS))

    def rows_map(i, s, nu, ie, nv):
        return pos(i, s, nu)[0], 0, 0

    def next_rows_map(i, s, nu, ie, nv):
        return jnp.minimum(pos(i, s, nu)[0] + 1, n_items - 1), 0, 0

    idx_block = (None, 1, rows)
    return pl.pallas_call(
        _ffn_kernel,
        grid_spec=pltpu.PrefetchScalarGridSpec(
            num_scalar_prefetch=3,
            grid=(n_items, MOE_K_STEPS + MOE_O_STEPS),
            in_specs=[
                pl.BlockSpec(idx_block, rows_map, memory_space=pltpu.SMEM),
                pl.BlockSpec(idx_block, next_rows_map, memory_space=pltpu.SMEM),
                pl.BlockSpec(memory_space=pl.ANY),
                pl.BlockSpec((None, None, MOE_K_CHUNK, D_FF), w_in_map),
                pl.BlockSpec((None, None, MOE_K_CHUNK, D_FF), w_in_map),
                pl.BlockSpec((None, None, D_FF, oc), w_down_map),
            ],
            out_specs=pl.BlockSpec(memory_space=pl.ANY),
            scratch_shapes=[
                pltpu.VMEM((2, rows, dp), U32),
                pltpu.VMEM((rows, D_FF), F32),
                pltpu.VMEM((rows, D_FF), F32),
                pltpu.VMEM((rows, D_FF), BF16),
                pltpu.VMEM((rows, dp), U32),
                pltpu.SemaphoreType.DMA((2,)),
                pltpu.SemaphoreType.DMA(()),
            ],
        ),
        out_shape=jax.ShapeDtypeStruct((TOP_K * t_tokens, dp), U32),
        compiler_params=_params("arbitrary", "arbitrary"),
        name="moe_ffn",
    )(n_used, item_expert, item_rows, row_slot.reshape(n_items, 1, rows), row_slot.reshape(n_items, 1, rows),
      xp, w_gate, w_up, w_down)


def _combine_kernel(y0_ref, y1_ref, gate_ref, x_ref, g_ref, b_ref, o_ref, obf_ref):
    gates = gate_ref[...]
    ffn = (gates[:, 0:1] * jnp.concatenate(_unpack_bf16_pair(y0_ref[...]), axis=1)
           + gates[:, 1:2] * jnp.concatenate(_unpack_bf16_pair(y1_ref[...]), axis=1))
    y = _layer_norm_rows(ALPHA * x_ref[...] + ffn, g_ref[...], b_ref[...])
    o_ref[...] = y
    obf_ref[...] = y.astype(BF16)


def _combine_layer_norm(x, yk, gate, g, b):
    t_tokens, d = x.shape
    tm = LN_ROWS
    n_steps = t_tokens // tm
    row = pl.BlockSpec((tm, d), lambda i: (i, 0))
    vec = pl.BlockSpec((1, d), lambda i: (0, 0))
    return pl.pallas_call(
        _combine_kernel,
        grid=(n_steps,),
        in_specs=[
            pl.BlockSpec((tm, d // 2), lambda i: (i, 0)),
            pl.BlockSpec((tm, d // 2), lambda i: (i + n_steps, 0)),
            pl.BlockSpec((tm, TOP_K), lambda i: (i, 0)),
            row, vec, vec,
        ],
        out_specs=[row, row],
        out_shape=[jax.ShapeDtypeStruct((t_tokens, d), F32), jax.ShapeDtypeStruct((t_tokens, d), BF16)],
        compiler_params=_params("arbitrary"),
        name="moe_combine",
    )(yk, yk, gate.T, x, g.reshape(1, d), b.reshape(1, d))


def _moe_layer_norm(x, xp, layer, router_w, router_bias, w_gate, w_up, w_down, ln_g, ln_b):
    t_tokens, _ = x.shape
    rows = MOE_ROWS
    eidx, gate, rank, counts = _route(x, router_w, router_bias)
    cnt = counts[:, 0]
    items_per_slot = (cnt + rows - 1) // rows
    item_end = jnp.cumsum(items_per_slot)
    row_start = (item_end - items_per_slot) * rows
    in_slot = _slot_of_expert(eidx)[..., None] == jnp.arange(N_EXPERTS, dtype=I32)
    dest = (jnp.sum(jnp.where(in_slot, row_start, 0), axis=-1) + rank).reshape(-1)
    n_items = (t_tokens * TOP_K) // rows + N_EXPERTS
    n_used = item_end[-1].astype(I32).reshape(1)
    item_id = jnp.arange(n_items, dtype=I32)
    item_slot = jnp.minimum(jnp.sum(item_id[:, None] >= item_end[None, :], axis=1), N_EXPERTS - 1)
    item_expert = _expert_of_slot(item_slot).astype(I32)
    item_rows = jnp.clip(row_start[item_slot] + cnt[item_slot] - item_id * rows, 0, rows)
    item_rows = jnp.where(item_id < n_used[0], item_rows, 0).astype(I32)
    row_slot = jnp.zeros((n_items * rows,), I32).at[dest].set(jnp.arange(TOP_K * t_tokens, dtype=I32))

    yk = _expert_ffn(xp, item_expert, item_rows, n_used, row_slot, w_gate, w_up, w_down, layer)
    return _combine_layer_norm(x, yk, gate, ln_g, ln_b)


def kernel(x, rel_bias, even_w_in, even_w_out, diff_lambda, diff_subln_g, odd_w_in, odd_w_out, ssd_conv_w, ssd_conv_b, ssd_dt_bias, ssd_a_log, ssd_d, ssd_norm_g, gmlp_ln_g, gmlp_ln_b, gmlp_ws, gmlp_bs, router_w, router_bias, moe_w_gate, moe_w_up, moe_w_down, ln_mix_g, ln_mix_b, ln_ffn_g, ln_ffn_b):
    bsz, s, d = x.shape
    t_tokens = bsz * s
    xf = x.reshape(t_tokens, d)
    xb = xf.astype(BF16)
    moe_args = (router_w, router_bias, moe_w_gate, moe_w_up, moe_w_down)

    lambda_init = 0.8 - 0.6 * math.exp(-0.3 * 0)
    h = _project([xb], even_w_in[0], [0], 0, EVEN_IN, BF16, tm=1024, tn=512)
    attn = _even_attention(h.reshape(bsz, s, EVEN_IN), _bias_tiles(rel_bias), diff_lambda[0],
                           diff_subln_g[0], lambda_init)
    mix = _project([attn.reshape(t_tokens, d)], even_w_out[0], [0], 0, d, F32, tm=1024, tn=512)
    xf, xp = _residual_layer_norm(xf, mix, ln_mix_g[0], ln_mix_b[0])
    xf, xb = _moe_layer_norm(xf, xp, 0, *moe_args, ln_ffn_g[0], ln_ffn_b[0])

    zx_cols = SSD_D_INNER + SSD_CONV_DIM
    w_in_t = jnp.swapaxes(odd_w_in[0], 0, 1)
    h_zx = _project([xb], w_in_t, [0], 0, zx_cols, F32, tm=1024, tn=512, w_transposed=True)
    h_dt = _project([xb], w_in_t, [0], zx_cols, LANES, F32, tm=1024, tn=LANES, w_transposed=True)
    h_uv = _project([xb], w_in_t, [0], C_IN, 2 * GMLP_WIDTH, F32, tm=1024, tn=512, w_transposed=True)
    y_ssd = _ssd(h_zx, h_dt, bsz, ssd_conv_w[0], ssd_conv_b[0], ssd_dt_bias[0], ssd_a_log[0], ssd_d[0],
                 ssd_norm_g[0])
    y_gmlp = _gmlp(h_uv, gmlp_ln_g[0], gmlp_ln_b[0], gmlp_ws[0], gmlp_bs[0])
    mix = _project([y_ssd, y_gmlp], odd_w_out[0], [0, SSD_D_INNER // GMLP_WIDTH], 0, d, F32, tm=512, tn=512)
    xf, xp = _residual_layer_norm(xf, mix, ln_mix_g[1], ln_mix_b[1])
    xf, xb = _moe_layer_norm(xf, xp, 1, *moe_args, ln_ffn_g[1], ln_ffn_b[1])
    return xf.reshape(bsz, s, d)
```

```python
import functools
import math

import numpy as np
import jax
import jax.numpy as jnp
from jax import lax
from jax.experimental import pallas as pl
from jax.experimental.pallas import tpu as pltpu

F32 = jnp.float32
BF16 = jnp.bfloat16
I32 = jnp.int32

D_MODEL = 4096
DEPTH = 2
HEAD_DIM = 128
N_HEADS_A = 16
N_HEADS_B = 16
DIFF_HALF = HEAD_DIM // 2
N_BUCKETS = 32
MAX_DISTANCE = 128
A_IN = 3 * N_HEADS_A * HEAD_DIM
EVEN_IN = A_IN + 3 * N_HEADS_B * HEAD_DIM
DIFF_SCALE = DIFF_HALF ** -0.5
SB_SCALE = HEAD_DIM ** -0.5

SSD_HEAD_DIM = 64
SSD_D_INNER = D_MODEL
SSD_N_HEADS = SSD_D_INNER // SSD_HEAD_DIM
SSD_N_GROUPS = 8
SSD_HEADS_PER_GROUP = SSD_N_HEADS // SSD_N_GROUPS
SSD_D_STATE = 128
SSD_CONV = 4
SSD_CHUNK = 128
SSD_GN = SSD_N_GROUPS * SSD_D_STATE
SSD_CONV_DIM = SSD_D_INNER + 2 * SSD_GN
SSD_GROUP_DIM = SSD_D_INNER // SSD_N_GROUPS
C_IN = SSD_D_INNER + SSD_CONV_DIM + SSD_N_HEADS
GMLP_WIDTH = D_MODEL // 2
GMLP_GROUPS = 8
GMLP_GROUP_DIM = GMLP_WIDTH // GMLP_GROUPS
GMLP_CHUNK = 128

N_EXPERTS = 32
N_EXPERT_GROUPS = 8
EXPERTS_PER_GROUP = N_EXPERTS // N_EXPERT_GROUPS
TOP_K = 2
D_FF = 768

ALPHA = (2 * DEPTH) ** 0.25
LN_EPS = 1e-5

LANES = 128
SUBLANES = 8
VMEM_LIMIT_BYTES = 58 * 1024 * 1024

ATTN_TILE = 256
ATTN_HEADS_PER_STEP = 8
ATTN_ONES_ROWS = 16
MOE_SUB_ROWS = 272
MOE_ROWS = 3 * MOE_SUB_ROWS
MOE_K_STEPS = 4
MOE_K_CHUNK = D_MODEL // MOE_K_STEPS
MOE_O_STEPS = 2
MOE_DMA_UNROLL = 8
U32 = jnp.uint32
HIGH_HALF = np.uint32(0xFFFF0000)
ROUTER_ROWS = 512
LN_ROWS = 256


def _params(*sem):
    return pltpu.CompilerParams(dimension_semantics=sem, vmem_limit_bytes=VMEM_LIMIT_BYTES)


def _proj_kernel(*refs, n_in, w_transposed):
    x_refs = refs[:n_in]
    w_refs = refs[n_in:2 * n_in]
    o_ref = refs[2 * n_in]
    wbf_refs = refs[2 * n_in + 1:]

    @pl.when(pl.program_id(1) == 0)
    def _():
        for w_ref, wbf_ref in zip(w_refs, wbf_refs):
            wbf_ref[...] = w_ref[...].astype(BF16)

    mm = _dot_nt if w_transposed else functools.partial(jnp.dot, preferred_element_type=F32)
    acc = mm(x_refs[0][...], wbf_refs[0][...])
    for x_ref, wbf_ref in zip(x_refs[1:], wbf_refs[1:]):
        acc = acc + mm(x_ref[...], wbf_ref[...])
    o_ref[...] = acc.astype(o_ref.dtype)


def _project(xs, w, row_blocks, col0, n_cols, out_dtype, tm, tn, w_transposed=False):
    m = xs[0].shape[0]
    n_in = len(xs)
    in_specs = [pl.BlockSpec((tm, x.shape[1]), lambda j, i: (i, 0)) for x in xs]
    for x, rb in zip(xs, row_blocks):
        k = x.shape[1]
        if not w_transposed:
            in_specs.append(pl.BlockSpec((k, tn), lambda j, i, rb=rb: (rb, j + col0 // tn)))
        elif col0 % tn == 0:
            in_specs.append(pl.BlockSpec((tn, k), lambda j, i, rb=rb: (j + col0 // tn, rb)))
        else:
            in_specs.append(pl.BlockSpec((pl.Element(tn), pl.Element(k)),
                                         lambda j, i, rb=rb, k=k: (pl.multiple_of(col0 + j * tn, SUBLANES),
                                                                   rb * k)))
    return pl.pallas_call(
        functools.partial(_proj_kernel, n_in=n_in, w_transposed=w_transposed),
        grid=(n_cols // tn, m // tm),
        in_specs=in_specs,
        out_specs=pl.BlockSpec((tm, tn), lambda j, i: (i, j)),
        out_shape=jax.ShapeDtypeStruct((m, n_cols), out_dtype),
        scratch_shapes=[pltpu.VMEM((tn, x.shape[1]) if w_transposed else (x.shape[1], tn), BF16) for x in xs],
        compiler_params=_params("arbitrary", "arbitrary"),
        name="proj",
    )(*xs, *([w] * n_in))


def _layer_norm_rows(y, g, b):
    mu = jnp.mean(y, axis=-1, keepdims=True)
    yc = y - mu
    var = jnp.mean(yc * yc, axis=-1, keepdims=True)
    return yc * lax.rsqrt(var + LN_EPS) * g + b


def _pack_bf16_pair(lo, hi):
    lo_bits = lax.bitcast_convert_type(lo.astype(BF16).astype(F32), U32)
    hi_bits = lax.bitcast_convert_type(hi.astype(BF16).astype(F32), U32)
    return (lo_bits >> 16) | (hi_bits & HIGH_HALF)


def _unpack_bf16_pair(p):
    return (lax.bitcast_convert_type(p << 16, F32), lax.bitcast_convert_type(p & HIGH_HALF, F32))


def _pack_rows(y):
    half = MOE_K_CHUNK // 2
    parts = [_pack_bf16_pair(y[:, c * MOE_K_CHUNK:c * MOE_K_CHUNK + half],
                             y[:, c * MOE_K_CHUNK + half:(c + 1) * MOE_K_CHUNK])
             for c in range(y.shape[1] // MOE_K_CHUNK)]
    return jnp.concatenate(parts, axis=1)


def _res_ln_kernel(x_ref, mix_ref, g_ref, b_ref, o_ref, opk_ref):
    y = _layer_norm_rows(ALPHA * x_ref[...] + mix_ref[...], g_ref[...], b_ref[...])
    o_ref[...] = y
    opk_ref[...] = _pack_rows(y)


def _residual_layer_norm(x, mix, g, b):
    m, d = x.shape
    tm = LN_ROWS
    row = pl.BlockSpec((tm, d), lambda i: (i, 0))
    half_row = pl.BlockSpec((tm, d // 2), lambda i: (i, 0))
    vec = pl.BlockSpec((1, d), lambda i: (0, 0))
    return pl.pallas_call(
        _res_ln_kernel,
        grid=(m // tm,),
        in_specs=[row, row, vec, vec],
        out_specs=[row, half_row],
        out_shape=[jax.ShapeDtypeStruct((m, d), F32), jax.ShapeDtypeStruct((m, d // 2), U32)],
        compiler_params=_params("arbitrary"),
        name="res_ln",
    )(x, mix, g.reshape(1, d), b.reshape(1, d))


def _bias_tile_kernel(rb_ref, o_ref):
    head = pl.program_id(0)
    t = ATTN_TILE
    key = lax.broadcasted_iota(I32, (t, t), 0)
    qry = lax.broadcasted_iota(I32, (t, t), 1)
    max_exact = N_BUCKETS // 2
    for d in range(2):
        n = jnp.maximum(d * t + qry - key, 0)
        nf = jnp.maximum(n, 1).astype(F32)
        large = max_exact + (jnp.log(nf / max_exact) / math.log(MAX_DISTANCE / max_exact)
                             * (N_BUCKETS - max_exact)).astype(I32)
        large = jnp.minimum(large, N_BUCKETS - 1)
        bucket = jnp.where(n < max_exact, n, large)
        acc = jnp.zeros((t, t), F32)
        for bkt in range(N_BUCKETS):
            acc = jnp.where(bucket == bkt, rb_ref[bkt, head], acc)
        o_ref[d] = acc


def _bias_tiles(rel_bias):
    t = ATTN_TILE
    return pl.pallas_call(
        _bias_tile_kernel,
        grid=(N_HEADS_A,),
        in_specs=[pl.BlockSpec(memory_space=pltpu.SMEM)],
        out_specs=pl.BlockSpec((None, 2, t, t), lambda h: (h, 0, 0, 0)),
        out_shape=jax.ShapeDtypeStruct((N_HEADS_A, 2, t, t), F32),
        compiler_params=_params("arbitrary"),
        name="bias_tiles",
    )(rel_bias)


def _dot_nt(a, b):
    return lax.dot_general(a, b, (((1,), (1,)), ((), ())), preferred_element_type=F32)


def _head_cols(h):
    return slice(h * HEAD_DIM, (h + 1) * HEAD_DIM)


def _value_transposes(v_ref, vt_sc):
    t = ATTN_TILE
    ones = jnp.ones((ATTN_ONES_ROWS, t), BF16)
    for h in range(ATTN_HEADS_PER_STEP):
        for j in range(v_ref.shape[0] // t):
            vb = v_ref[j * t:(j + 1) * t, _head_cols(h)]
            vt_sc[h, j, 0:HEAD_DIM, :] = vb.astype(F32).T.astype(BF16)
            vt_sc[h, j, HEAD_DIM:, :] = ones


def _diff_attention(dl_ref, g_ref, q_ref, k_ref, vt_sc, bias_ref, o_ref, m_sc, acc_sc, lambda_init):
    t = ATTN_TILE
    qi = pl.program_id(2)
    lane = lax.broadcasted_iota(I32, (t, HEAD_DIM), 1)
    q_maps = []
    for h in range(ATTN_HEADS_PER_STEP):
        q = q_ref[:, _head_cols(h)] * DIFF_SCALE
        zero = jnp.zeros_like(q)
        q_maps.append((jnp.where(lane < DIFF_HALF, q, zero), jnp.where(lane >= DIFF_HALF, q, zero)))

    chains = [(h, mp) for h in range(ATTN_HEADS_PER_STEP) for mp in range(2)]

    def block(j, bias_of_head, mask, first):
        kv_start = pl.multiple_of(j * t, t)
        kbs = [k_ref[pl.ds(kv_start, t), _head_cols(h)] for h in range(ATTN_HEADS_PER_STEP)]
        scores = [_dot_nt(kbs[h], q_maps[h][mp]) for h, mp in chains]
        probs, alphas = [], []
        for (h, mp), s in zip(chains, scores):
            bias = bias_of_head(h)
            uniform = bias.shape == (1, 1)
            if not uniform:
                s = s + bias
            if mask is not None:
                s = jnp.where(mask, s, -jnp.inf)
            m_new = jnp.max(s, axis=0, keepdims=True)
            if uniform:
                m_new = m_new + bias
            if not first:
                m_old = m_sc[h, mp]
                m_new = jnp.maximum(m_old, m_new)
                alphas.append(jnp.exp(m_old - m_new))
            m_sc[h, mp] = m_new
            probs.append(jnp.exp(s - (m_new - bias if uniform else m_new)).astype(BF16))
        for i, (h, mp) in enumerate(chains):
            pv = jnp.dot(vt_sc[h, j], probs[i], preferred_element_type=F32)
            acc_sc[h, mp] = pv if first else alphas[i] * acc_sc[h, mp] + pv

    key = lax.broadcasted_iota(I32, (t, t), 0)
    qry = lax.broadcasted_iota(I32, (t, t), 1)
    block(qi, lambda h: bias_ref[h, 0], qry >= key, True)

    @pl.when(qi >= 1)
    def _():
        block(qi - 1, lambda h: bias_ref[h, 1], None, False)

    def far_body(j, carry):
        block(j, lambda h: bias_ref[h, 1, 0:1, t - 1:t], None, False)
        return carry

    lax.fori_loop(0, qi - 1, far_body, 0)

    dl = dl_ref[...]
    lam = (jnp.exp(jnp.sum(dl[0:1] * dl[1:2], axis=-1, keepdims=True))
           - jnp.exp(jnp.sum(dl[2:3] * dl[3:4], axis=-1, keepdims=True)) + lambda_init)
    for h in range(ATTN_HEADS_PER_STEP):
        a0 = acc_sc[h, 0]
        a1 = acc_sc[h, 1]
        oa = (a0[0:HEAD_DIM] / a0[HEAD_DIM:HEAD_DIM + 1]
              - lam * (a1[0:HEAD_DIM] / a1[HEAD_DIM:HEAD_DIM + 1]))
        ms = jnp.mean(oa * oa, axis=0, keepdims=True)
        oa = (oa * lax.rsqrt(ms + LN_EPS)).T * g_ref[...] * (1.0 - lambda_init)
        o_ref[:, _head_cols(h)] = oa.astype(o_ref.dtype)


def _log_sigmoid(z):
    return jnp.minimum(z, 0.0) - jnp.log(1.0 + jnp.exp(-jnp.abs(z)))


def _split_bf16(x):
    hi = x.astype(BF16)
    lo = (x - hi.astype(F32)).astype(BF16)
    return hi, lo


def _stick_breaking_attention(q_ref, k_ref, vt_sc, o_ref, c_sc, acc_sc):
    t = ATTN_TILE
    qi = pl.program_id(2)
    key = lax.broadcasted_iota(I32, (t, t), 0)
    qry = lax.broadcasted_iota(I32, (t, t), 1)
    later = qry > key
    after = later.astype(BF16)

    heads = range(ATTN_HEADS_PER_STEP)

    def block(j, strict, first):
        kv_start = pl.multiple_of(j * t, t)
        zs = [_dot_nt(k_ref[pl.ds(kv_start, t), _head_cols(h)], q_ref[:, _head_cols(h)]) for h in heads]
        log_betas, splits, col_sums = [], [], []
        for h in heads:
            z = zs[h] * SB_SCALE
            log_beta = _log_sigmoid(z)
            log_1mb = log_beta - z
            if strict is not None:
                log_1mb = jnp.where(strict, log_1mb, 0.0)
            log_betas.append(log_beta)
            splits.append(_split_bf16(log_1mb))
            col_sums.append(jnp.sum(log_1mb, axis=0, keepdims=True))
        tails = [jnp.dot(after, hi, preferred_element_type=F32) + jnp.dot(after, lo, preferred_element_type=F32)
                 for hi, lo in splits]
        weights = []
        for h in heads:
            tail = tails[h] if first else tails[h] + c_sc[h]
            w = jnp.exp(log_betas[h] + tail)
            if strict is not None:
                w = jnp.where(strict, w, 0.0)
            weights.append(w.astype(BF16))
            c_sc[h] = col_sums[h] if first else c_sc[h] + col_sums[h]
        for h in heads:
            pv = jnp.dot(vt_sc[h, j][0:HEAD_DIM], weights[h], preferred_element_type=F32)
            acc_sc[h, 0, 0:HEAD_DIM] = pv if first else acc_sc[h, 0, 0:HEAD_DIM] + pv

    block(qi, later, True)

    def body(step, carry):
        block(qi - 1 - step, None, False)
        return carry

    lax.fori_loop(0, qi, body, 0)
    for h in range(ATTN_HEADS_PER_STEP):
        o_ref[:, _head_cols(h)] = acc_sc[h, 0, 0:HEAD_DIM].T.astype(o_ref.dtype)


def _attn_kernel(dl_ref, g_ref, q_ref, k_ref, v_ref, bias_ref, o_ref, vt_sc, m_sc, c_sc, acc_sc, *, lambda_init):
    grp = pl.program_id(1)

    @pl.when(pl.program_id(2) == 0)
    def _():
        _value_transposes(v_ref, vt_sc)

    @pl.when(grp < N_HEADS_A // ATTN_HEADS_PER_STEP)
    def _():
        _diff_attention(dl_ref, g_ref, q_ref, k_ref, vt_sc, bias_ref, o_ref, m_sc, acc_sc, lambda_init)

    @pl.when(grp >= N_HEADS_A // ATTN_HEADS_PER_STEP)
    def _():
        _stick_breaking_attention(q_ref, k_ref, vt_sc, o_ref, c_sc, acc_sc)


def _even_attention(h, bias_tiles, diff_lambda, subln_g, lambda_init):
    bsz, s, _ = h.shape
    t = ATTN_TILE
    nh = ATTN_HEADS_PER_STEP
    width = nh * HEAD_DIM
    groups_a = N_HEADS_A // nh
    n_groups = (N_HEADS_A + N_HEADS_B) // nh

    def q_col(g):
        return g + jnp.where(g >= groups_a, 2 * groups_a, 0)

    return pl.pallas_call(
        functools.partial(_attn_kernel, lambda_init=lambda_init),
        grid=(bsz, n_groups, s // t),
        in_specs=[
            pl.BlockSpec((4, DIFF_HALF), lambda b, g, qi: (0, 0)),
            pl.BlockSpec((1, HEAD_DIM), lambda b, g, qi: (0, 0)),
            pl.BlockSpec((None, t, width), lambda b, g, qi: (b, qi, q_col(g))),
            pl.BlockSpec((None, s, width), lambda b, g, qi: (b, 0, q_col(g) + groups_a)),
            pl.BlockSpec((None, s, width), lambda b, g, qi: (b, 0, q_col(g) + 2 * groups_a)),
            pl.BlockSpec((nh, 2, t, t), lambda b, g, qi: (jnp.minimum(g, groups_a - 1), 0, 0, 0)),
        ],
        out_specs=pl.BlockSpec((None, t, width), lambda b, g, qi: (b, qi, g)),
        out_shape=jax.ShapeDtypeStruct((bsz, s, n_groups * width), BF16),
        scratch_shapes=[
            pltpu.VMEM((nh, s // t, HEAD_DIM + ATTN_ONES_ROWS, t), BF16),
            pltpu.VMEM((nh, 2, 1, t), F32),
            pltpu.VMEM((nh, 1, t), F32),
            pltpu.VMEM((nh, 2, HEAD_DIM + ATTN_ONES_ROWS, t), F32),
        ],
        compiler_params=_params("arbitrary", "arbitrary", "arbitrary"),
        name="even_attention",
    )(diff_lambda, subln_g.reshape(1, HEAD_DIM), h, h, h, bias_tiles)


def _silu(x):
    return x * jax.nn.sigmoid(x)


def _softplus(x):
    return jnp.maximum(x, 0.0) + jnp.log(1.0 + jnp.exp(-jnp.abs(x)))


def _ssd_kernel(z_ref, x_ref, b_ref, c_ref, dt_ref, cw_ref, cb_ref, dtb_ref, alog_ref, dskip_ref, ng_ref,
                o_ref, pad_sc, state_sc, acst_sc, dtt_sc):
    pad_sc[0:SUBLANES, :] = jnp.zeros((SUBLANES, pad_sc.shape[1]), F32)
    state_sc[...] = jnp.zeros_like(state_sc)

    def chunk_body(chunk, carry):
        _ssd_chunk(chunk, z_ref, x_ref, b_ref, c_ref, dt_ref, cw_ref, cb_ref, dtb_ref, alog_ref, dskip_ref,
                   ng_ref, o_ref, pad_sc, state_sc, acst_sc, dtt_sc)
        return carry

    lax.fori_loop(0, x_ref.shape[0] // SSD_CHUNK, chunk_body, 0)


def _ssd_chunk(chunk, z_ref, x_ref, b_ref, c_ref, dt_ref, cw_ref, cb_ref, dtb_ref, alog_ref, dskip_ref, ng_ref,
               o_ref, pad_sc, state_sc, acst_sc, dtt_sc):
    grp = pl.program_id(1)
    L = SSD_CHUNK
    P2 = 2 * SSD_HEAD_DIM
    n_conv = SSD_GROUP_DIM + 2 * SSD_D_STATE
    halo = SUBLANES
    rows = pl.ds(pl.multiple_of(chunk * L, L), L)

    pad_sc[halo:halo + L, 0:SSD_GROUP_DIM] = x_ref[rows, :]
    pad_sc[halo:halo + L, SSD_GROUP_DIM:SSD_GROUP_DIM + SSD_D_STATE] = b_ref[rows, :]
    pad_sc[halo:halo + L, SSD_GROUP_DIM + SSD_D_STATE:n_conv] = c_ref[rows, :]
    conv = cb_ref[...]
    for j in range(SSD_CONV):
        start = halo - (SSD_CONV - 1) + j
        conv = conv + cw_ref[j:j + 1, :] * pad_sc[start:start + L, :]
    pad_sc[0:halo, :] = pad_sc[L:L + halo, :]
    xbc = _silu(conv)
    xg = xbc[:, 0:SSD_GROUP_DIM]
    bg = xbc[:, SSD_GROUP_DIM:SSD_GROUP_DIM + SSD_D_STATE]
    cg = xbc[:, SSD_GROUP_DIM + SSD_D_STATE:n_conv]
    xg_bf = xg.astype(BF16)

    dt = _softplus(dt_ref[rows, :] + dtb_ref[...])
    adt = dt * (-jnp.exp(alog_ref[...]))
    row = lax.broadcasted_iota(I32, (L, L), 0)
    col = lax.broadcasted_iota(I32, (L, L), 1)
    causal = row >= col
    acs = jnp.dot(causal.astype(F32), adt, preferred_element_type=F32, precision=lax.Precision.HIGHEST)
    acst_sc[...] = acs.T
    dtt_sc[...] = dt.T

    cb = _dot_nt(cg.astype(BF16), bg.astype(BF16))
    bgt = bg.T
    cg_bf = cg.astype(BF16)
    prev = state_sc[...]
    y_off = jnp.dot(cg_bf, prev.astype(BF16), preferred_element_type=F32)

    lane = lax.broadcasted_iota(I32, (L, P2), 1)
    first_head = lane < SSD_HEAD_DIM
    lane_row = lax.broadcasted_iota(I32, (1, P2), 1) < SSD_HEAD_DIM
    y_pairs = []
    for pair in range(SSD_HEADS_PER_GROUP // 2):
        x_pair = xg_bf[:, pair * P2:(pair + 1) * P2]
        y_diag, st, e_acs, decay = [], [], [], []
        for sub in range(2):
            head = grp * SSD_HEADS_PER_GROUP + 2 * pair + sub
            acs_row = acst_sc[pl.ds(head, 1), :]
            dt_row = dtt_sc[pl.ds(head, 1), :]
            row_b = jnp.broadcast_to(acs_row, (L, L))
            col_b = row_b.T
            decay_m = jnp.exp(jnp.where(causal, col_b - row_b, -jnp.inf))
            m = (cb * decay_m * dt_row).astype(BF16)
            y_diag.append(jnp.dot(m, x_pair, preferred_element_type=F32))
            acs_last = col_b[L - 1:L, :]
            w_row = jnp.exp(acs_last - acs_row) * dt_row
            st.append(jnp.dot((bgt * w_row).astype(BF16), x_pair, preferred_element_type=F32))
            e_acs.append(jnp.exp(col_b))
            decay.append(jnp.exp(acs_last))
        cols = slice(pair * P2, (pair + 1) * P2)
        y_pair = (jnp.where(first_head, y_diag[0], y_diag[1])
                  + jnp.where(first_head, e_acs[0], e_acs[1]) * y_off[:, cols])
        state_sc[:, cols] = (prev[:, cols] * jnp.where(lane_row, decay[0], decay[1])
                             + jnp.where(first_head, st[0], st[1]))
        y_pair = y_pair + dskip_ref[:, cols] * xg[:, cols]
        y_pairs.append(y_pair * _silu(z_ref[rows, cols]))

    y = jnp.concatenate(y_pairs, axis=-1)
    ms = jnp.mean(y * y, axis=-1, keepdims=True)
    o_ref[rows, :] = (y * lax.rsqrt(ms + LN_EPS) * ng_ref[...]).astype(o_ref.dtype)


def _ssd(h_zx, h_dt, bsz, conv_w, conv_b, dt_bias, a_log, d_skip, norm_g):
    t_tokens = h_zx.shape[0]
    n_chunks = t_tokens // bsz // SSD_CHUNK
    L = SSD_CHUNK
    gd, ns, G = SSD_GROUP_DIM, SSD_D_STATE, SSD_N_GROUPS
    n_conv = gd + 2 * ns

    def pack(p):
        xs = p[..., :SSD_D_INNER].reshape(p.shape[:-1] + (G, gd))
        bs_ = p[..., SSD_D_INNER:SSD_D_INNER + SSD_GN].reshape(p.shape[:-1] + (G, ns))
        cs = p[..., SSD_D_INNER + SSD_GN:].reshape(p.shape[:-1] + (G, ns))
        return jnp.moveaxis(jnp.concatenate([xs, bs_, cs], axis=-1), -2, 0)

    cw = pack(conv_w)
    cb = pack(conv_b.reshape(1, -1))
    pad_heads = LANES - SSD_N_HEADS
    dtb = jnp.pad(dt_bias, (0, pad_heads)).reshape(1, LANES)
    alog = jnp.pad(a_log, (0, pad_heads)).reshape(1, LANES)
    dskip = jnp.repeat(d_skip, SSD_HEAD_DIM).reshape(1, SSD_D_INNER)

    seq = n_chunks * L
    x_blk0 = SSD_D_INNER // gd
    b_blk0 = (2 * SSD_D_INNER) // ns
    c_blk0 = (2 * SSD_D_INNER + SSD_GN) // ns
    return pl.pallas_call(
        _ssd_kernel,
        grid=(bsz, G),
        in_specs=[
            pl.BlockSpec((seq, gd), lambda b, g: (b, g)),
            pl.BlockSpec((seq, gd), lambda b, g: (b, x_blk0 + g)),
            pl.BlockSpec((seq, ns), lambda b, g: (b, b_blk0 + g)),
            pl.BlockSpec((seq, ns), lambda b, g: (b, c_blk0 + g)),
            pl.BlockSpec((seq, LANES), lambda b, g: (b, 0)),
            pl.BlockSpec((None, SSD_CONV, n_conv), lambda b, g: (g, 0, 0)),
            pl.BlockSpec((None, 1, n_conv), lambda b, g: (g, 0, 0)),
            pl.BlockSpec((1, LANES), lambda b, g: (0, 0)),
            pl.BlockSpec((1, LANES), lambda b, g: (0, 0)),
            pl.BlockSpec((1, gd), lambda b, g: (0, g)),
            pl.BlockSpec((1, gd), lambda b, g: (0, g)),
        ],
        out_specs=pl.BlockSpec((seq, gd), lambda b, g: (b, g)),
        out_shape=jax.ShapeDtypeStruct((t_tokens, SSD_D_INNER), BF16),
        scratch_shapes=[
            pltpu.VMEM((SUBLANES + L, n_conv), F32),
            pltpu.VMEM((ns, gd), F32),
            pltpu.VMEM((LANES, L), F32),
            pltpu.VMEM((LANES, L), F32),
        ],
        compiler_params=_params("arbitrary", "arbitrary"),
        name="ssd",
    )(h_zx, h_zx, h_zx, h_zx, h_dt, cw, cb, dtb, alog, dskip, norm_g.reshape(1, SSD_D_INNER))


def _gelu_tanh(x):
    return 0.5 * x * (1.0 + jnp.tanh(math.sqrt(2.0 / math.pi) * (x + 0.044715 * (x * x * x))))


def _gmlp_kernel(uv_ref, lng_ref, lnb_ref, ws_ref, bs_ref, o_ref):
    L = GMLP_CHUNK
    gdim = GMLP_GROUP_DIM
    row = lax.broadcasted_iota(I32, (L, L), 0)
    col = lax.broadcasted_iota(I32, (L, L), 1)
    causal = row >= col
    for g in range(GMLP_GROUPS):
        cols = slice(g * gdim, (g + 1) * gdim)
        u = _gelu_tanh(uv_ref[:, cols])
        v = _gelu_tanh(uv_ref[:, GMLP_WIDTH + g * gdim:GMLP_WIDTH + (g + 1) * gdim])
        v = _layer_norm_rows(v, lng_ref[:, cols], lnb_ref[:, cols])
        ws = jnp.where(causal, ws_ref[g], 0.0).astype(BF16)
        sv = jnp.dot(ws, v.astype(BF16), preferred_element_type=F32) + bs_ref[:, cols]
        o_ref[:, cols] = (u * sv).astype(o_ref.dtype)


def _gmlp(h_uv, ln_g, ln_b, ws, bs):
    t_tokens = h_uv.shape[0]
    L = GMLP_CHUNK
    bs_cols = jnp.repeat(bs.T, GMLP_GROUP_DIM, axis=1)
    return pl.pallas_call(
        _gmlp_kernel,
        grid=(t_tokens // L,),
        in_specs=[
            pl.BlockSpec((L, 2 * GMLP_WIDTH), lambda i: (i, 0)),
            pl.BlockSpec((1, GMLP_WIDTH), lambda i: (0, 0)),
            pl.BlockSpec((1, GMLP_WIDTH), lambda i: (0, 0)),
            pl.BlockSpec((GMLP_GROUPS, L, L), lambda i: (0, 0, 0)),
            pl.BlockSpec((L, GMLP_WIDTH), lambda i: (0, 0)),
        ],
        out_specs=pl.BlockSpec((L, GMLP_WIDTH), lambda i: (i, 0)),
        out_shape=jax.ShapeDtypeStruct((t_tokens, GMLP_WIDTH), BF16),
        compiler_params=_params("arbitrary"),
        name="gmlp",
    )(h_uv, ln_g.reshape(1, GMLP_WIDTH), ln_b.reshape(1, GMLP_WIDTH), ws, bs_cols)


def _router_kernel(x_ref, rw_ref, rb_ref, eidx_ref, gate_ref, rank_ref, cnt_ref, carry_sc):
    tm = x_ref.shape[0]
    G, K = N_EXPERT_GROUPS, EXPERTS_PER_GROUP

    @pl.when(pl.program_id(0) == 0)
    def _():
        carry_sc[...] = jnp.zeros_like(carry_sc)

    x_hi, x_lo = _split_bf16(x_ref[...])
    w_hi, w_lo = _split_bf16(rw_ref[...])
    logits_t = (jnp.dot(x_hi, w_hi, preferred_element_type=F32) + jnp.dot(x_hi, w_lo, preferred_element_type=F32)
                + jnp.dot(x_lo, w_hi, preferred_element_type=F32))
    logits = logits_t.T[0:N_EXPERTS]
    scores = jax.nn.sigmoid(logits)
    biased = scores + rb_ref[...]
    a = [biased[k * G:(k + 1) * G] for k in range(K)]
    sc = [scores[k * G:(k + 1) * G] for k in range(K)]
    hi01, lo01 = jnp.maximum(a[0], a[1]), jnp.minimum(a[0], a[1])
    hi23, lo23 = jnp.maximum(a[2], a[3]), jnp.minimum(a[2], a[3])
    grp_score = jnp.maximum(hi01, hi23) + jnp.maximum(jnp.minimum(hi01, hi23), jnp.maximum(lo01, lo23))
    gidx = lax.broadcasted_iota(I32, (G, tm), 0)
    g_best = jnp.max(grp_score, axis=0, keepdims=True)
    g_sel = jnp.min(jnp.where(grp_score == g_best, gidx, G), axis=0, keepdims=True)
    in_grp = gidx == g_sel
    cand = [jnp.sum(jnp.where(in_grp, a[k], 0.0), axis=0, keepdims=True) for k in range(K)]
    cand_sc = [jnp.sum(jnp.where(in_grp, sc[k], 0.0), axis=0, keepdims=True) for k in range(K)]

    def first_argmax(vals):
        best, idx = vals[0], jnp.zeros((1, tm), I32)
        for k in range(1, K):
            gt = vals[k] > best
            best = jnp.where(gt, vals[k], best)
            idx = jnp.where(gt, k, idx)
        return idx

    i1 = first_argmax(cand)
    i2 = first_argmax([jnp.where(i1 == k, -jnp.inf, cand[k]) for k in range(K)])
    v1 = sum(jnp.where(i1 == k, cand_sc[k], 0.0) for k in range(K))
    v2 = sum(jnp.where(i2 == k, cand_sc[k], 0.0) for k in range(K))
    den = v1 + v2
    eidx_ref[0:1, :] = g_sel * K + i1
    eidx_ref[1:2, :] = g_sel * K + i2
    gate_ref[0:1, :] = v1 / den
    gate_ref[1:2, :] = v2 / den

    member = jnp.concatenate(
        [jnp.where(in_grp & ((i1 == k) | (i2 == k)), 1.0, 0.0) for k in range(K)], axis=0)
    trow = lax.broadcasted_iota(I32, (tm, tm), 0)
    tcol = lax.broadcasted_iota(I32, (tm, tm), 1)
    before = (trow < tcol).astype(BF16)
    prefix = jnp.dot(member.astype(BF16), before, preferred_element_type=F32) + carry_sc[...]
    r1 = sum(jnp.sum(jnp.where(in_grp & (i1 == k), prefix[k * G:(k + 1) * G], 0.0), axis=0, keepdims=True)
             for k in range(K))
    r2 = sum(jnp.sum(jnp.where(in_grp & (i2 == k), prefix[k * G:(k + 1) * G], 0.0), axis=0, keepdims=True)
             for k in range(K))
    rank_ref[0:1, :] = r1.astype(I32)
    rank_ref[1:2, :] = r2.astype(I32)
    carry_sc[...] = carry_sc[...] + jnp.sum(member, axis=1, keepdims=True)
    cnt_ref[...] = jnp.broadcast_to(carry_sc[...], cnt_ref.shape).astype(I32)


def _slot_of_expert(e):
    return (e % EXPERTS_PER_GROUP) * N_EXPERT_GROUPS + e // EXPERTS_PER_GROUP


def _expert_of_slot(r):
    return (r % N_EXPERT_GROUPS) * EXPERTS_PER_GROUP + r // N_EXPERT_GROUPS


def _route(x, router_w, router_bias):
    t_tokens, d = x.shape
    tm = ROUTER_ROWS
    slot_expert = _expert_of_slot(jnp.arange(N_EXPERTS))
    rw = jnp.pad(router_w[:, slot_expert], ((0, 0), (0, LANES - N_EXPERTS)))
    rb = router_bias[slot_expert].reshape(N_EXPERTS, 1)
    tok = pl.BlockSpec((TOP_K, tm), lambda i: (0, i))
    return pl.pallas_call(
        _router_kernel,
        grid=(t_tokens // tm,),
        in_specs=[
            pl.BlockSpec((tm, d), lambda i: (i, 0)),
            pl.BlockSpec((d, LANES), lambda i: (0, 0)),
            pl.BlockSpec((N_EXPERTS, 1), lambda i: (0, 0)),
        ],
        out_specs=[tok, tok, tok, pl.BlockSpec((N_EXPERTS, LANES), lambda i: (0, 0))],
        out_shape=[
            jax.ShapeDtypeStruct((TOP_K, t_tokens), I32),
            jax.ShapeDtypeStruct((TOP_K, t_tokens), F32),
            jax.ShapeDtypeStruct((TOP_K, t_tokens), I32),
            jax.ShapeDtypeStruct((N_EXPERTS, LANES), I32),
        ],
        scratch_shapes=[pltpu.VMEM((N_EXPERTS, 1), F32)],
        compiler_params=_params("arbitrary"),
        name="router",
    )(x, rw, rb)


def _ffn_kernel(nused_ref, iexp_ref, nval_ref, src_ref, src_next_ref, xp_hbm, wg_ref, wu_ref, wd_ref,
                yk_hbm, xbuf, hg_sc, hu_sc, hid_sc, obuf, gsem, ssem):
    item = pl.program_id(0)
    step = pl.program_id(1)
    n_used = nused_ref[0]
    slot = lax.rem(item, 2)
    sub = MOE_SUB_ROWS
    half = MOE_K_CHUNK // 2

    n_tokens = xp_hbm.shape[0]
    unroll = MOE_DMA_UNROLL

    def for_rows(count, fn):
        def body(i, carry):
            fn(i)
            return carry
        lax.fori_loop(0, count, body, 0)

    def for_rows_unrolled(count, fn):
        def body(b, carry):
            base = pl.multiple_of(b * unroll, unroll)
            for u in range(unroll):
                fn(base + u)
            return carry
        n_full = count // unroll
        lax.fori_loop(0, n_full, body, 0)
        lax.fori_loop(n_full * unroll, count, lambda i, c: (fn(i), c)[1], 0)

    def gather_rows(slot_rows, count, to_slot):
        def start(i):
            slot_id = slot_rows[0, i]
            if n_tokens & (n_tokens - 1) == 0:
                tok = slot_id & (n_tokens - 1)
            else:
                tok = lax.rem(slot_id, n_tokens)
            pltpu.make_async_copy(xp_hbm.at[pl.ds(tok, 1), :], xbuf.at[to_slot, pl.ds(i, 1), :],
                                  gsem.at[to_slot]).start(priority=1)
        for_rows_unrolled(count, start)

    def scatter_rows(count):
        def start(i):
            pltpu.make_async_copy(obuf.at[pl.ds(i, 1), :], yk_hbm.at[pl.ds(src_ref[0, i], 1), :],
                                  ssem).start(priority=1)
        for_rows_unrolled(count, start)

    def wait_rows(count, n_rows_copy):
        n_full = count // unroll
        for_rows(n_full, lambda i: n_rows_copy(unroll).wait())
        for_rows(count - n_full * unroll, lambda i: n_rows_copy(1).wait())

    def gathered(to_slot):
        return lambda n: pltpu.make_async_copy(xp_hbm.at[pl.ds(0, n), :], xbuf.at[to_slot, pl.ds(0, n), :],
                                               gsem.at[to_slot])

    def scattered(n):
        return pltpu.make_async_copy(obuf.at[pl.ds(0, n), :], yk_hbm.at[pl.ds(0, n), :], ssem)

    @pl.when(item < n_used)
    def _():
        n_rows = nval_ref[item]
        n_sub = (n_rows + sub - 1) // sub

        @pl.when(step == 0)
        def _():
            @pl.when(item == 0)
            def _():
                xbuf[...] = jnp.zeros_like(xbuf)
                gather_rows(src_ref, n_rows, 0)

            @pl.when(item + 1 < n_used)
            def _():
                gather_rows(src_next_ref, nval_ref[item + 1], 1 - slot)

            wait_rows(n_rows, gathered(slot))

        for c in range(MOE_K_STEPS):
            @pl.when(step == c)
            def _(c=c):
                wg = wg_ref[...].astype(BF16)
                wu = wu_ref[...].astype(BF16)

                def sub_block(r):
                    rows = pl.ds(pl.multiple_of(r * sub, sub), sub)
                    lo, hi = _unpack_bf16_pair(xbuf[slot, rows, c * half:(c + 1) * half])
                    xk = jnp.concatenate([lo, hi], axis=1).astype(BF16)
                    g = jnp.dot(xk, wg, preferred_element_type=F32)
                    u = jnp.dot(xk, wu, preferred_element_type=F32)
                    if c > 0:
                        g = g + hg_sc[rows, :]
                        u = u + hu_sc[rows, :]
                    if c < MOE_K_STEPS - 1:
                        hg_sc[rows, :] = g
                        hu_sc[rows, :] = u
                    else:
                        hid_sc[rows, :] = (_silu(g) * u).astype(BF16)

                for_rows(n_sub, sub_block)

        @pl.when(step == MOE_K_STEPS)
        def _():
            @pl.when(item > 0)
            def _():
                wait_rows(nval_ref[item - 1], scattered)

            wd = wd_ref[...].astype(BF16)

            def sub_block(r):
                rows = pl.ds(pl.multiple_of(r * sub, sub), sub)
                o = jnp.dot(hid_sc[rows, :], wd, preferred_element_type=F32)
                obuf[rows, :] = lax.bitcast_convert_type(o.astype(BF16).astype(F32), U32) >> 16

            for_rows(n_sub, sub_block)

        @pl.when(step == MOE_K_STEPS + 1)
        def _():
            wd = wd_ref[...].astype(BF16)

            def sub_block(r):
                rows = pl.ds(pl.multiple_of(r * sub, sub), sub)
                o = jnp.dot(hid_sc[rows, :], wd, preferred_element_type=F32)
                obuf[rows, :] = obuf[rows, :] | (lax.bitcast_convert_type(o.astype(BF16).astype(F32), U32)
                                                 & HIGH_HALF)

            for_rows(n_sub, sub_block)
            scatter_rows(n_rows)

            @pl.when(item == n_used - 1)
            def _():
                wait_rows(n_rows, scattered)


def _expert_ffn(xp, item_expert, item_rows, n_used, row_slot, w_gate, w_up, w_down, layer):
    t_tokens, dp = xp.shape
    d = 2 * dp
    rows = MOE_ROWS
    n_items = item_expert.shape[0]
    oc = d // MOE_O_STEPS
    last = MOE_K_STEPS + MOE_O_STEPS - 1

    def pos(i, s, nu):
        used = i < nu[0]
        return jnp.where(used, i, nu[0] - 1), jnp.where(used, s, last)

    def w_in_map(i, s, nu, ie, nv):
        ii, ss = pos(i, s, nu)
        return layer, ie[ii], jnp.minimum(ss, MOE_K_STEPS - 1), 0

    def w_down_map(i, s, nu, ie, nv):
        ii, ss = pos(i, s, nu)
        early = ss < MOE_K_STEPS
        return (layer, ie[jnp.where(early, jnp.maximum(ii - 1, 0), ii)], 0,
                jnp.where(early, MOE_O_STEPS - 1, ss - MOE_K_STEPS))

    def rows_map(i, s, nu, ie, nv):
        return pos(i, s, nu)[0], 0, 0

    def next_rows_map(i, s, nu, ie, nv):
        return jnp.minimum(pos(i, s, nu)[0] + 1, n_items - 1), 0, 0

    idx_block = (None, 1, rows)
    return pl.pallas_call(
        _ffn_kernel,
        grid_spec=pltpu.PrefetchScalarGridSpec(
            num_scalar_prefetch=3,
            grid=(n_items, MOE_K_STEPS + MOE_O_STEPS),
            in_specs=[
                pl.BlockSpec(idx_block, rows_map, memory_space=pltpu.SMEM),
                pl.BlockSpec(idx_block, next_rows_map, memory_space=pltpu.SMEM),
                pl.BlockSpec(memory_space=pl.ANY),
                pl.BlockSpec((None, None, MOE_K_CHUNK, D_FF), w_in_map),
                pl.BlockSpec((None, None, MOE_K_CHUNK, D_FF), w_in_map),
                pl.BlockSpec((None, None, D_FF, oc), w_down_map),
            ],
            out_specs=pl.BlockSpec(memory_space=pl.ANY),
            scratch_shapes=[
                pltpu.VMEM((2, rows, dp), U32),
                pltpu.VMEM((rows, D_FF), F32),
                pltpu.VMEM((rows, D_FF), F32),
                pltpu.VMEM((rows, D_FF), BF16),
                pltpu.VMEM((rows, dp), U32),
                pltpu.SemaphoreType.DMA((2,)),
                pltpu.SemaphoreType.DMA(()),
            ],
        ),
        out_shape=jax.ShapeDtypeStruct((TOP_K * t_tokens, dp), U32),
        compiler_params=_params("arbitrary", "arbitrary"),
        name="moe_ffn",
    )(n_used, item_expert, item_rows, row_slot.reshape(n_items, 1, rows), row_slot.reshape(n_items, 1, rows),
      xp, w_gate, w_up, w_down)


def _combine_kernel(y0_ref, y1_ref, gate_ref, x_ref, g_ref, b_ref, o_ref, obf_ref):
    gates = gate_ref[...]
    ffn = (gates[:, 0:1] * jnp.concatenate(_unpack_bf16_pair(y0_ref[...]), axis=1)
           + gates[:, 1:2] * jnp.concatenate(_unpack_bf16_pair(y1_ref[...]), axis=1))
    y = _layer_norm_rows(ALPHA * x_ref[...] + ffn, g_ref[...], b_ref[...])
    o_ref[...] = y
    obf_ref[...] = y.astype(BF16)


def _combine_layer_norm(x, yk, gate, g, b):
    t_tokens, d = x.shape
    tm = LN_ROWS
    n_steps = t_tokens // tm
    row = pl.BlockSpec((tm, d), lambda i: (i, 0))
    vec = pl.BlockSpec((1, d), lambda i: (0, 0))
    return pl.pallas_call(
        _combine_kernel,
        grid=(n_steps,),
        in_specs=[
            pl.BlockSpec((tm, d // 2), lambda i: (i, 0)),
            pl.BlockSpec((tm, d // 2), lambda i: (i + n_steps, 0)),
            pl.BlockSpec((tm, TOP_K), lambda i: (i, 0)),
            row, vec, vec,
        ],
        out_specs=[row, row],
        out_shape=[jax.ShapeDtypeStruct((t_tokens, d), F32), jax.ShapeDtypeStruct((t_tokens, d), BF16)],
        compiler_params=_params("arbitrary"),
        name="moe_combine",
    )(yk, yk, gate.T, x, g.reshape(1, d), b.reshape(1, d))


def _moe_layer_norm(x, xp, layer, router_w, router_bias, w_gate, w_up, w_down, ln_g, ln_b):
    t_tokens, _ = x.shape
    rows = MOE_ROWS
    eidx, gate, rank, counts = _route(x, router_w, router_bias)
    cnt = counts[:, 0]
    items_per_slot = (cnt + rows - 1) // rows
    item_end = jnp.cumsum(items_per_slot)
    row_start = (item_end - items_per_slot) * rows
    in_slot = _slot_of_expert(eidx)[..., None] == jnp.arange(N_EXPERTS, dtype=I32)
    dest = (jnp.sum(jnp.where(in_slot, row_start, 0), axis=-1) + rank).reshape(-1)
    n_items = (t_tokens * TOP_K) // rows + N_EXPERTS
    n_used = item_end[-1].astype(I32).reshape(1)
    item_id = jnp.arange(n_items, dtype=I32)
    item_slot = jnp.minimum(jnp.sum(item_id[:, None] >= item_end[None, :], axis=1), N_EXPERTS - 1)
    item_expert = _expert_of_slot(item_slot).astype(I32)
    item_rows = jnp.clip(row_start[item_slot] + cnt[item_slot] - item_id * rows, 0, rows)
    item_rows = jnp.where(item_id < n_used[0], item_rows, 0).astype(I32)
    row_slot = jnp.zeros((n_items * rows,), I32).at[dest].set(jnp.arange(TOP_K * t_tokens, dtype=I32))

    yk = _expert_ffn(xp, item_expert, item_rows, n_used, row_slot, w_gate, w_up, w_down, layer)
    return _combine_layer_norm(x, yk, gate, ln_g, ln_b)


def kernel(x, rel_bias, even_w_in, even_w_out, diff_lambda, diff_subln_g, odd_w_in, odd_w_out, ssd_conv_w, ssd_conv_b, ssd_dt_bias, ssd_a_log, ssd_d, ssd_norm_g, gmlp_ln_g, gmlp_ln_b, gmlp_ws, gmlp_bs, router_w, router_bias, moe_w_gate, moe_w_up, moe_w_down, ln_mix_g, ln_mix_b, ln_ffn_g, ln_ffn_b):
    bsz, s, d = x.shape
    t_tokens = bsz * s
    xf = x.reshape(t_tokens, d)
    xb = xf.astype(BF16)
    moe_args = (router_w, router_bias, moe_w_gate, moe_w_up, moe_w_down)

    lambda_init = 0.8 - 0.6 * math.exp(-0.3 * 0)
    h = _project([xb], even_w_in[0], [0], 0, EVEN_IN, BF16, tm=1024, tn=768)
    attn = _even_attention(h.reshape(bsz, s, EVEN_IN), _bias_tiles(rel_bias), diff_lambda[0],
                           diff_subln_g[0], lambda_init)
    mix = _project([attn.reshape(t_tokens, d)], even_w_out[0], [0], 0, d, F32, tm=1024, tn=512)
    xf, xp = _residual_layer_norm(xf, mix, ln_mix_g[0], ln_mix_b[0])
    xf, xb = _moe_layer_norm(xf, xp, 0, *moe_args, ln_ffn_g[0], ln_ffn_b[0])

    zx_cols = SSD_D_INNER + SSD_CONV_DIM
    w_in_t = jnp.swapaxes(odd_w_in[0], 0, 1)
    h_zx = _project([xb], w_in_t, [0], 0, zx_cols, F32, tm=1024, tn=512, w_transposed=True)
    h_dt = _project([xb], w_in_t, [0], zx_cols, LANES, F32, tm=1024, tn=LANES, w_transposed=True)
    h_uv = _project([xb], w_in_t, [0], C_IN, 2 * GMLP_WIDTH, F32, tm=1024, tn=512, w_transposed=True)
    y_ssd = _ssd(h_zx, h_dt, bsz, ssd_conv_w[0], ssd_conv_b[0], ssd_dt_bias[0], ssd_a_log[0], ssd_d[0],
                 ssd_norm_g[0])
    y_gmlp = _gmlp(h_uv, gmlp_ln_g[0], gmlp_ln_b[0], gmlp_ws[0], gmlp_bs[0])
    mix = _project([y_ssd, y_gmlp], odd_w_out[0], [0, SSD_D_INNER // GMLP_WIDTH], 0, d, F32, tm=512, tn=512)
    xf, xp = _residual_layer_norm(xf, mix, ln_mix_g[1], ln_mix_b[1])
    xf, xb = _moe_layer_norm(xf, xp, 1, *moe_args, ln_ffn_g[1], ln_ffn_b[1])
    return xf.reshape(bsz, s, d)
```

```python
import functools
import math

import numpy as np
import jax
import jax.numpy as jnp
from jax import lax
from jax.experimental import pallas as pl
from jax.experimental.pallas import tpu as pltpu

F32 = jnp.float32
BF16 = jnp.bfloat16
I32 = jnp.int32

D_MODEL = 4096
DEPTH = 2
HEAD_DIM = 128
N_HEADS_A = 16
N_HEADS_B = 16
DIFF_HALF = HEAD_DIM // 2
N_BUCKETS = 32
MAX_DISTANCE = 128
A_IN = 3 * N_HEADS_A * HEAD_DIM
EVEN_IN = A_IN + 3 * N_HEADS_B * HEAD_DIM
DIFF_SCALE = DIFF_HALF ** -0.5
SB_SCALE = HEAD_DIM ** -0.5

SSD_HEAD_DIM = 64
SSD_D_INNER = D_MODEL
SSD_N_HEADS = SSD_D_INNER // SSD_HEAD_DIM
SSD_N_GROUPS = 8
SSD_HEADS_PER_GROUP = SSD_N_HEADS // SSD_N_GROUPS
SSD_D_STATE = 128
SSD_CONV = 4
SSD_CHUNK = 128
SSD_GN = SSD_N_GROUPS * SSD_D_STATE
SSD_CONV_DIM = SSD_D_INNER + 2 * SSD_GN
SSD_GROUP_DIM = SSD_D_INNER // SSD_N_GROUPS
C_IN = SSD_D_INNER + SSD_CONV_DIM + SSD_N_HEADS
GMLP_WIDTH = D_MODEL // 2
GMLP_GROUPS = 8
GMLP_GROUP_DIM = GMLP_WIDTH // GMLP_GROUPS
GMLP_CHUNK = 128

N_EXPERTS = 32
N_EXPERT_GROUPS = 8
EXPERTS_PER_GROUP = N_EXPERTS // N_EXPERT_GROUPS
TOP_K = 2
D_FF = 768

ALPHA = (2 * DEPTH) ** 0.25
LN_EPS = 1e-5

LANES = 128
SUBLANES = 8
VMEM_LIMIT_BYTES = 58 * 1024 * 1024

ATTN_TILE = 256
ATTN_HEADS_PER_STEP = 8
ATTN_ONES_ROWS = 16
MOE_SUB_ROWS = 576
MOE_ROWS = 1 * MOE_SUB_ROWS
MOE_K_STEPS = 4
MOE_K_CHUNK = D_MODEL // MOE_K_STEPS
MOE_O_STEPS = 2
MOE_DMA_UNROLL = 8
U32 = jnp.uint32
HIGH_HALF = np.uint32(0xFFFF0000)
ROUTER_ROWS = 512
LN_ROWS = 256


def _params(*sem):
    return pltpu.CompilerParams(dimension_semantics=sem, vmem_limit_bytes=VMEM_LIMIT_BYTES)


def _proj_kernel(*refs, n_in, w_transposed):
    x_refs = refs[:n_in]
    w_refs = refs[n_in:2 * n_in]
    o_ref = refs[2 * n_in]
    wbf_refs = refs[2 * n_in + 1:]

    @pl.when(pl.program_id(1) == 0)
    def _():
        for w_ref, wbf_ref in zip(w_refs, wbf_refs):
            wbf_ref[...] = w_ref[...].astype(BF16)

    mm = _dot_nt if w_transposed else functools.partial(jnp.dot, preferred_element_type=F32)
    acc = mm(x_refs[0][...], wbf_refs[0][...])
    for x_ref, wbf_ref in zip(x_refs[1:], wbf_refs[1:]):
        acc = acc + mm(x_ref[...], wbf_ref[...])
    o_ref[...] = acc.astype(o_ref.dtype)


def _project(xs, w, row_blocks, col0, n_cols, out_dtype, tm, tn, w_transposed=False):
    m = xs[0].shape[0]
    n_in = len(xs)
    in_specs = [pl.BlockSpec((tm, x.shape[1]), lambda j, i: (i, 0)) for x in xs]
    for x, rb in zip(xs, row_blocks):
        k = x.shape[1]
        if not w_transposed:
            in_specs.append(pl.BlockSpec((k, tn), lambda j, i, rb=rb: (rb, j + col0 // tn)))
        elif col0 % tn == 0:
            in_specs.append(pl.BlockSpec((tn, k), lambda j, i, rb=rb: (j + col0 // tn, rb)))
        else:
            in_specs.append(pl.BlockSpec((pl.Element(tn), pl.Element(k)),
                                         lambda j, i, rb=rb, k=k: (pl.multiple_of(col0 + j * tn, SUBLANES),
                                                                   rb * k)))
    return pl.pallas_call(
        functools.partial(_proj_kernel, n_in=n_in, w_transposed=w_transposed),
        grid=(n_cols // tn, m // tm),
        in_specs=in_specs,
        out_specs=pl.BlockSpec((tm, tn), lambda j, i: (i, j)),
        out_shape=jax.ShapeDtypeStruct((m, n_cols), out_dtype),
        scratch_shapes=[pltpu.VMEM((tn, x.shape[1]) if w_transposed else (x.shape[1], tn), BF16) for x in xs],
        compiler_params=_params("arbitrary", "arbitrary"),
        name="proj",
    )(*xs, *([w] * n_in))


def _layer_norm_rows(y, g, b):
    mu = jnp.mean(y, axis=-1, keepdims=True)
    yc = y - mu
    var = jnp.mean(yc * yc, axis=-1, keepdims=True)
    return yc * lax.rsqrt(var + LN_EPS) * g + b


def _pack_bf16_pair(lo, hi):
    lo_bits = lax.bitcast_convert_type(lo.astype(BF16).astype(F32), U32)
    hi_bits = lax.bitcast_convert_type(hi.astype(BF16).astype(F32), U32)
    return (lo_bits >> 16) | (hi_bits & HIGH_HALF)


def _unpack_bf16_pair(p):
    return (lax.bitcast_convert_type(p << 16, F32), lax.bitcast_convert_type(p & HIGH_HALF, F32))


def _pack_rows(y):
    half = MOE_K_CHUNK // 2
    parts = [_pack_bf16_pair(y[:, c * MOE_K_CHUNK:c * MOE_K_CHUNK + half],
                             y[:, c * MOE_K_CHUNK + half:(c + 1) * MOE_K_CHUNK])
             for c in range(y.shape[1] // MOE_K_CHUNK)]
    return jnp.concatenate(parts, axis=1)


def _res_ln_kernel(x_ref, mix_ref, g_ref, b_ref, o_ref, opk_ref):
    y = _layer_norm_rows(ALPHA * x_ref[...] + mix_ref[...], g_ref[...], b_ref[...])
    o_ref[...] = y
    opk_ref[...] = _pack_rows(y)


def _residual_layer_norm(x, mix, g, b):
    m, d = x.shape
    tm = LN_ROWS
    row = pl.BlockSpec((tm, d), lambda i: (i, 0))
    half_row = pl.BlockSpec((tm, d // 2), lambda i: (i, 0))
    vec = pl.BlockSpec((1, d), lambda i: (0, 0))
    return pl.pallas_call(
        _res_ln_kernel,
        grid=(m // tm,),
        in_specs=[row, row, vec, vec],
        out_specs=[row, half_row],
        out_shape=[jax.ShapeDtypeStruct((m, d), F32), jax.ShapeDtypeStruct((m, d // 2), U32)],
        compiler_params=_params("arbitrary"),
        name="res_ln",
    )(x, mix, g.reshape(1, d), b.reshape(1, d))


def _bias_tile_kernel(rb_ref, o_ref):
    head = pl.program_id(0)
    t = ATTN_TILE
    key = lax.broadcasted_iota(I32, (t, t), 0)
    qry = lax.broadcasted_iota(I32, (t, t), 1)
    max_exact = N_BUCKETS // 2
    for d in range(2):
        n = jnp.maximum(d * t + qry - key, 0)
        nf = jnp.maximum(n, 1).astype(F32)
        large = max_exact + (jnp.log(nf / max_exact) / math.log(MAX_DISTANCE / max_exact)
                             * (N_BUCKETS - max_exact)).astype(I32)
        large = jnp.minimum(large, N_BUCKETS - 1)
        bucket = jnp.where(n < max_exact, n, large)
        acc = jnp.zeros((t, t), F32)
        for bkt in range(N_BUCKETS):
            acc = jnp.where(bucket == bkt, rb_ref[bkt, head], acc)
        o_ref[d] = acc


def _bias_tiles(rel_bias):
    t = ATTN_TILE
    return pl.pallas_call(
        _bias_tile_kernel,
        grid=(N_HEADS_A,),
        in_specs=[pl.BlockSpec(memory_space=pltpu.SMEM)],
        out_specs=pl.BlockSpec((None, 2, t, t), lambda h: (h, 0, 0, 0)),
        out_shape=jax.ShapeDtypeStruct((N_HEADS_A, 2, t, t), F32),
        compiler_params=_params("arbitrary"),
        name="bias_tiles",
    )(rel_bias)


def _dot_nt(a, b):
    return lax.dot_general(a, b, (((1,), (1,)), ((), ())), preferred_element_type=F32)


def _head_cols(h):
    return slice(h * HEAD_DIM, (h + 1) * HEAD_DIM)


def _value_transposes(v_ref, vt_sc):
    t = ATTN_TILE
    ones = jnp.ones((ATTN_ONES_ROWS, t), BF16)
    for h in range(ATTN_HEADS_PER_STEP):
        for j in range(v_ref.shape[0] // t):
            vb = v_ref[j * t:(j + 1) * t, _head_cols(h)]
            vt_sc[h, j, 0:HEAD_DIM, :] = vb.astype(F32).T.astype(BF16)
            vt_sc[h, j, HEAD_DIM:, :] = ones


def _diff_attention(dl_ref, g_ref, q_ref, k_ref, vt_sc, bias_ref, o_ref, m_sc, acc_sc, lambda_init):
    t = ATTN_TILE
    qi = pl.program_id(2)
    lane = lax.broadcasted_iota(I32, (t, HEAD_DIM), 1)
    q_maps = []
    for h in range(ATTN_HEADS_PER_STEP):
        q = q_ref[:, _head_cols(h)] * DIFF_SCALE
        zero = jnp.zeros_like(q)
        q_maps.append((jnp.where(lane < DIFF_HALF, q, zero), jnp.where(lane >= DIFF_HALF, q, zero)))

    chains = [(h, mp) for h in range(ATTN_HEADS_PER_STEP) for mp in range(2)]

    def block(j, bias_of_head, mask, first):
        kv_start = pl.multiple_of(j * t, t)
        kbs = [k_ref[pl.ds(kv_start, t), _head_cols(h)] for h in range(ATTN_HEADS_PER_STEP)]
        scores = [_dot_nt(kbs[h], q_maps[h][mp]) for h, mp in chains]
        probs, alphas = [], []
        for (h, mp), s in zip(chains, scores):
            bias = bias_of_head(h)
            uniform = bias.shape == (1, 1)
            if not uniform:
                s = s + bias
            if mask is not None:
                s = jnp.where(mask, s, -jnp.inf)
            m_new = jnp.max(s, axis=0, keepdims=True)
            if uniform:
                m_new = m_new + bias
            if not first:
                m_old = m_sc[h, mp]
                m_new = jnp.maximum(m_old, m_new)
                alphas.append(jnp.exp(m_old - m_new))
            m_sc[h, mp] = m_new
            probs.append(jnp.exp(s - (m_new - bias if uniform else m_new)).astype(BF16))
        for i, (h, mp) in enumerate(chains):
            pv = jnp.dot(vt_sc[h, j], probs[i], preferred_element_type=F32)
            acc_sc[h, mp] = pv if first else alphas[i] * acc_sc[h, mp] + pv

    key = lax.broadcasted_iota(I32, (t, t), 0)
    qry = lax.broadcasted_iota(I32, (t, t), 1)
    block(qi, lambda h: bias_ref[h, 0], qry >= key, True)

    @pl.when(qi >= 1)
    def _():
        block(qi - 1, lambda h: bias_ref[h, 1], None, False)

    def far_body(j, carry):
        block(j, lambda h: bias_ref[h, 1, 0:1, t - 1:t], None, False)
        return carry

    lax.fori_loop(0, qi - 1, far_body, 0)

    dl = dl_ref[...]
    lam = (jnp.exp(jnp.sum(dl[0:1] * dl[1:2], axis=-1, keepdims=True))
           - jnp.exp(jnp.sum(dl[2:3] * dl[3:4], axis=-1, keepdims=True)) + lambda_init)
    for h in range(ATTN_HEADS_PER_STEP):
        a0 = acc_sc[h, 0]
        a1 = acc_sc[h, 1]
        oa = (a0[0:HEAD_DIM] / a0[HEAD_DIM:HEAD_DIM + 1]
              - lam * (a1[0:HEAD_DIM] / a1[HEAD_DIM:HEAD_DIM + 1]))
        ms = jnp.mean(oa * oa, axis=0, keepdims=True)
        oa = (oa * lax.rsqrt(ms + LN_EPS)).T * g_ref[...] * (1.0 - lambda_init)
        o_ref[:, _head_cols(h)] = oa.astype(o_ref.dtype)


def _log_sigmoid(z):
    return jnp.minimum(z, 0.0) - jnp.log(1.0 + jnp.exp(-jnp.abs(z)))


def _split_bf16(x):
    hi = x.astype(BF16)
    lo = (x - hi.astype(F32)).astype(BF16)
    return hi, lo


def _stick_breaking_attention(q_ref, k_ref, vt_sc, o_ref, c_sc, acc_sc):
    t = ATTN_TILE
    qi = pl.program_id(2)
    key = lax.broadcasted_iota(I32, (t, t), 0)
    qry = lax.broadcasted_iota(I32, (t, t), 1)
    later = qry > key
    after = later.astype(BF16)

    heads = range(ATTN_HEADS_PER_STEP)

    def block(j, strict, first):
        kv_start = pl.multiple_of(j * t, t)
        zs = [_dot_nt(k_ref[pl.ds(kv_start, t), _head_cols(h)], q_ref[:, _head_cols(h)]) for h in heads]
        log_betas, splits, col_sums = [], [], []
        for h in heads:
            z = zs[h] * SB_SCALE
            log_beta = _log_sigmoid(z)
            log_1mb = log_beta - z
            if strict is not None:
                log_1mb = jnp.where(strict, log_1mb, 0.0)
            log_betas.append(log_beta)
            splits.append(_split_bf16(log_1mb))
            col_sums.append(jnp.sum(log_1mb, axis=0, keepdims=True))
        tails = [jnp.dot(after, hi, preferred_element_type=F32) + jnp.dot(after, lo, preferred_element_type=F32)
                 for hi, lo in splits]
        weights = []
        for h in heads:
            tail = tails[h] if first else tails[h] + c_sc[h]
            w = jnp.exp(log_betas[h] + tail)
            if strict is not None:
                w = jnp.where(strict, w, 0.0)
            weights.append(w.astype(BF16))
            c_sc[h] = col_sums[h] if first else c_sc[h] + col_sums[h]
        for h in heads:
            pv = jnp.dot(vt_sc[h, j][0:HEAD_DIM], weights[h], preferred_element_type=F32)
            acc_sc[h, 0, 0:HEAD_DIM] = pv if first else acc_sc[h, 0, 0:HEAD_DIM] + pv

    block(qi, later, True)

    def body(step, carry):
        block(qi - 1 - step, None, False)
        return carry

    lax.fori_loop(0, qi, body, 0)
    for h in range(ATTN_HEADS_PER_STEP):
        o_ref[:, _head_cols(h)] = acc_sc[h, 0, 0:HEAD_DIM].T.astype(o_ref.dtype)


def _attn_kernel(dl_ref, g_ref, q_ref, k_ref, v_ref, bias_ref, o_ref, vt_sc, m_sc, c_sc, acc_sc, *, lambda_init):
    grp = pl.program_id(1)

    @pl.when(pl.program_id(2) == 0)
    def _():
        _value_transposes(v_ref, vt_sc)

    @pl.when(grp < N_HEADS_A // ATTN_HEADS_PER_STEP)
    def _():
        _diff_attention(dl_ref, g_ref, q_ref, k_ref, vt_sc, bias_ref, o_ref, m_sc, acc_sc, lambda_init)

    @pl.when(grp >= N_HEADS_A // ATTN_HEADS_PER_STEP)
    def _():
        _stick_breaking_attention(q_ref, k_ref, vt_sc, o_ref, c_sc, acc_sc)


def _even_attention(h, bias_tiles, diff_lambda, subln_g, lambda_init):
    bsz, s, _ = h.shape
    t = ATTN_TILE
    nh = ATTN_HEADS_PER_STEP
    width = nh * HEAD_DIM
    groups_a = N_HEADS_A // nh
    n_groups = (N_HEADS_A + N_HEADS_B) // nh

    def q_col(g):
        return g + jnp.where(g >= groups_a, 2 * groups_a, 0)

    return pl.pallas_call(
        functools.partial(_attn_kernel, lambda_init=lambda_init),
        grid=(bsz, n_groups, s // t),
        in_specs=[
            pl.BlockSpec((4, DIFF_HALF), lambda b, g, qi: (0, 0)),
            pl.BlockSpec((1, HEAD_DIM), lambda b, g, qi: (0, 0)),
            pl.BlockSpec((None, t, width), lambda b, g, qi: (b, qi, q_col(g))),
            pl.BlockSpec((None, s, width), lambda b, g, qi: (b, 0, q_col(g) + groups_a)),
            pl.BlockSpec((None, s, width), lambda b, g, qi: (b, 0, q_col(g) + 2 * groups_a)),
            pl.BlockSpec((nh, 2, t, t), lambda b, g, qi: (jnp.minimum(g, groups_a - 1), 0, 0, 0)),
        ],
        out_specs=pl.BlockSpec((None, t, width), lambda b, g, qi: (b, qi, g)),
        out_shape=jax.ShapeDtypeStruct((bsz, s, n_groups * width), BF16),
        scratch_shapes=[
            pltpu.VMEM((nh, s // t, HEAD_DIM + ATTN_ONES_ROWS, t), BF16),
            pltpu.VMEM((nh, 2, 1, t), F32),
            pltpu.VMEM((nh, 1, t), F32),
            pltpu.VMEM((nh, 2, HEAD_DIM + ATTN_ONES_ROWS, t), F32),
        ],
        compiler_params=_params("arbitrary", "arbitrary", "arbitrary"),
        name="even_attention",
    )(diff_lambda, subln_g.reshape(1, HEAD_DIM), h, h, h, bias_tiles)


def _silu(x):
    return x * jax.nn.sigmoid(x)


def _softplus(x):
    return jnp.maximum(x, 0.0) + jnp.log(1.0 + jnp.exp(-jnp.abs(x)))


def _ssd_kernel(z_ref, x_ref, b_ref, c_ref, dt_ref, cw_ref, cb_ref, dtb_ref, alog_ref, dskip_ref, ng_ref,
                o_ref, pad_sc, state_sc, acst_sc, dtt_sc):
    pad_sc[0:SUBLANES, :] = jnp.zeros((SUBLANES, pad_sc.shape[1]), F32)
    state_sc[...] = jnp.zeros_like(state_sc)

    def chunk_body(chunk, carry):
        _ssd_chunk(chunk, z_ref, x_ref, b_ref, c_ref, dt_ref, cw_ref, cb_ref, dtb_ref, alog_ref, dskip_ref,
                   ng_ref, o_ref, pad_sc, state_sc, acst_sc, dtt_sc)
        return carry

    lax.fori_loop(0, x_ref.shape[0] // SSD_CHUNK, chunk_body, 0)


def _ssd_chunk(chunk, z_ref, x_ref, b_ref, c_ref, dt_ref, cw_ref, cb_ref, dtb_ref, alog_ref, dskip_ref, ng_ref,
               o_ref, pad_sc, state_sc, acst_sc, dtt_sc):
    grp = pl.program_id(1)
    L = SSD_CHUNK
    P2 = 2 * SSD_HEAD_DIM
    n_conv = SSD_GROUP_DIM + 2 * SSD_D_STATE
    halo = SUBLANES
    rows = pl.ds(pl.multiple_of(chunk * L, L), L)

    pad_sc[halo:halo + L, 0:SSD_GROUP_DIM] = x_ref[rows, :]
    pad_sc[halo:halo + L, SSD_GROUP_DIM:SSD_GROUP_DIM + SSD_D_STATE] = b_ref[rows, :]
    pad_sc[halo:halo + L, SSD_GROUP_DIM + SSD_D_STATE:n_conv] = c_ref[rows, :]
    conv = cb_ref[...]
    for j in range(SSD_CONV):
        start = halo - (SSD_CONV - 1) + j
        conv = conv + cw_ref[j:j + 1, :] * pad_sc[start:start + L, :]
    pad_sc[0:halo, :] = pad_sc[L:L + halo, :]
    xbc = _silu(conv)
    xg = xbc[:, 0:SSD_GROUP_DIM]
    bg = xbc[:, SSD_GROUP_DIM:SSD_GROUP_DIM + SSD_D_STATE]
    cg = xbc[:, SSD_GROUP_DIM + SSD_D_STATE:n_conv]
    xg_bf = xg.astype(BF16)

    dt = _softplus(dt_ref[rows, :] + dtb_ref[...])
    adt = dt * (-jnp.exp(alog_ref[...]))
    row = lax.broadcasted_iota(I32, (L, L), 0)
    col = lax.broadcasted_iota(I32, (L, L), 1)
    causal = row >= col
    acs = jnp.dot(causal.astype(F32), adt, preferred_element_type=F32, precision=lax.Precision.HIGHEST)
    acst_sc[...] = acs.T
    dtt_sc[...] = dt.T

    cb = _dot_nt(cg.astype(BF16), bg.astype(BF16))
    bgt = bg.T
    cg_bf = cg.astype(BF16)
    prev = state_sc[...]
    y_off = jnp.dot(cg_bf, prev.astype(BF16), preferred_element_type=F32)

    lane = lax.broadcasted_iota(I32, (L, P2), 1)
    first_head = lane < SSD_HEAD_DIM
    lane_row = lax.broadcasted_iota(I32, (1, P2), 1) < SSD_HEAD_DIM
    y_pairs = []
    for pair in range(SSD_HEADS_PER_GROUP // 2):
        x_pair = xg_bf[:, pair * P2:(pair + 1) * P2]
        y_diag, st, e_acs, decay = [], [], [], []
        for sub in range(2):
            head = grp * SSD_HEADS_PER_GROUP + 2 * pair + sub
            acs_row = acst_sc[pl.ds(head, 1), :]
            dt_row = dtt_sc[pl.ds(head, 1), :]
            row_b = jnp.broadcast_to(acs_row, (L, L))
            col_b = row_b.T
            decay_m = jnp.exp(jnp.where(causal, col_b - row_b, -jnp.inf))
            m = (cb * decay_m * dt_row).astype(BF16)
            y_diag.append(jnp.dot(m, x_pair, preferred_element_type=F32))
            acs_last = col_b[L - 1:L, :]
            w_row = jnp.exp(acs_last - acs_row) * dt_row
            st.append(jnp.dot((bgt * w_row).astype(BF16), x_pair, preferred_element_type=F32))
            e_acs.append(jnp.exp(col_b))
            decay.append(jnp.exp(acs_last))
        cols = slice(pair * P2, (pair + 1) * P2)
        y_pair = (jnp.where(first_head, y_diag[0], y_diag[1])
                  + jnp.where(first_head, e_acs[0], e_acs[1]) * y_off[:, cols])
        state_sc[:, cols] = (prev[:, cols] * jnp.where(lane_row, decay[0], decay[1])
                             + jnp.where(first_head, st[0], st[1]))
        y_pair = y_pair + dskip_ref[:, cols] * xg[:, cols]
        y_pairs.append(y_pair * _silu(z_ref[rows, cols]))

    y = jnp.concatenate(y_pairs, axis=-1)
    ms = jnp.mean(y * y, axis=-1, keepdims=True)
    o_ref[rows, :] = (y * lax.rsqrt(ms + LN_EPS) * ng_ref[...]).astype(o_ref.dtype)


def _ssd(h_zx, h_dt, bsz, conv_w, conv_b, dt_bias, a_log, d_skip, norm_g):
    t_tokens = h_zx.shape[0]
    n_chunks = t_tokens // bsz // SSD_CHUNK
    L = SSD_CHUNK
    gd, ns, G = SSD_GROUP_DIM, SSD_D_STATE, SSD_N_GROUPS
    n_conv = gd + 2 * ns

    def pack(p):
        xs = p[..., :SSD_D_INNER].reshape(p.shape[:-1] + (G, gd))
        bs_ = p[..., SSD_D_INNER:SSD_D_INNER + SSD_GN].reshape(p.shape[:-1] + (G, ns))
        cs = p[..., SSD_D_INNER + SSD_GN:].reshape(p.shape[:-1] + (G, ns))
        return jnp.moveaxis(jnp.concatenate([xs, bs_, cs], axis=-1), -2, 0)

    cw = pack(conv_w)
    cb = pack(conv_b.reshape(1, -1))
    pad_heads = LANES - SSD_N_HEADS
    dtb = jnp.pad(dt_bias, (0, pad_heads)).reshape(1, LANES)
    alog = jnp.pad(a_log, (0, pad_heads)).reshape(1, LANES)
    dskip = jnp.repeat(d_skip, SSD_HEAD_DIM).reshape(1, SSD_D_INNER)

    seq = n_chunks * L
    x_blk0 = SSD_D_INNER // gd
    b_blk0 = (2 * SSD_D_INNER) // ns
    c_blk0 = (2 * SSD_D_INNER + SSD_GN) // ns
    return pl.pallas_call(
        _ssd_kernel,
        grid=(bsz, G),
        in_specs=[
            pl.BlockSpec((seq, gd), lambda b, g: (b, g)),
            pl.BlockSpec((seq, gd), lambda b, g: (b, x_blk0 + g)),
            pl.BlockSpec((seq, ns), lambda b, g: (b, b_blk0 + g)),
            pl.BlockSpec((seq, ns), lambda b, g: (b, c_blk0 + g)),
            pl.BlockSpec((seq, LANES), lambda b, g: (b, 0)),
            pl.BlockSpec((None, SSD_CONV, n_conv), lambda b, g: (g, 0, 0)),
            pl.BlockSpec((None, 1, n_conv), lambda b, g: (g, 0, 0)),
            pl.BlockSpec((1, LANES), lambda b, g: (0, 0)),
            pl.BlockSpec((1, LANES), lambda b, g: (0, 0)),
            pl.BlockSpec((1, gd), lambda b, g: (0, g)),
            pl.BlockSpec((1, gd), lambda b, g: (0, g)),
        ],
        out_specs=pl.BlockSpec((seq, gd), lambda b, g: (b, g)),
        out_shape=jax.ShapeDtypeStruct((t_tokens, SSD_D_INNER), BF16),
        scratch_shapes=[
            pltpu.VMEM((SUBLANES + L, n_conv), F32),
            pltpu.VMEM((ns, gd), F32),
            pltpu.VMEM((LANES, L), F32),
            pltpu.VMEM((LANES, L), F32),
        ],
        compiler_params=_params("arbitrary", "arbitrary"),
        name="ssd",
    )(h_zx, h_zx, h_zx, h_zx, h_dt, cw, cb, dtb, alog, dskip, norm_g.reshape(1, SSD_D_INNER))


def _gelu_tanh(x):
    return 0.5 * x * (1.0 + jnp.tanh(math.sqrt(2.0 / math.pi) * (x + 0.044715 * (x * x * x))))


def _gmlp_kernel(uv_ref, lng_ref, lnb_ref, ws_ref, bs_ref, o_ref):
    L = GMLP_CHUNK
    gdim = GMLP_GROUP_DIM
    row = lax.broadcasted_iota(I32, (L, L), 0)
    col = lax.broadcasted_iota(I32, (L, L), 1)
    causal = row >= col
    for g in range(GMLP_GROUPS):
        cols = slice(g * gdim, (g + 1) * gdim)
        u = _gelu_tanh(uv_ref[:, cols])
        v = _gelu_tanh(uv_ref[:, GMLP_WIDTH + g * gdim:GMLP_WIDTH + (g + 1) * gdim])
        v = _layer_norm_rows(v, lng_ref[:, cols], lnb_ref[:, cols])
        ws = jnp.where(causal, ws_ref[g], 0.0).astype(BF16)
        sv = jnp.dot(ws, v.astype(BF16), preferred_element_type=F32) + bs_ref[:, cols]
        o_ref[:, cols] = (u * sv).astype(o_ref.dtype)


def _gmlp(h_uv, ln_g, ln_b, ws, bs):
    t_tokens = h_uv.shape[0]
    L = GMLP_CHUNK
    bs_cols = jnp.repeat(bs.T, GMLP_GROUP_DIM, axis=1)
    return pl.pallas_call(
        _gmlp_kernel,
        grid=(t_tokens // L,),
        in_specs=[
            pl.BlockSpec((L, 2 * GMLP_WIDTH), lambda i: (i, 0)),
            pl.BlockSpec((1, GMLP_WIDTH), lambda i: (0, 0)),
            pl.BlockSpec((1, GMLP_WIDTH), lambda i: (0, 0)),
            pl.BlockSpec((GMLP_GROUPS, L, L), lambda i: (0, 0, 0)),
            pl.BlockSpec((L, GMLP_WIDTH), lambda i: (0, 0)),
        ],
        out_specs=pl.BlockSpec((L, GMLP_WIDTH), lambda i: (i, 0)),
        out_shape=jax.ShapeDtypeStruct((t_tokens, GMLP_WIDTH), BF16),
        compiler_params=_params("arbitrary"),
        name="gmlp",
    )(h_uv, ln_g.reshape(1, GMLP_WIDTH), ln_b.reshape(1, GMLP_WIDTH), ws, bs_cols)


def _router_kernel(x_ref, rw_ref, rb_ref, eidx_ref, gate_ref, rank_ref, cnt_ref, carry_sc):
    tm = x_ref.shape[0]
    G, K = N_EXPERT_GROUPS, EXPERTS_PER_GROUP

    @pl.when(pl.program_id(0) == 0)
    def _():
        carry_sc[...] = jnp.zeros_like(carry_sc)

    x_hi, x_lo = _split_bf16(x_ref[...])
    w_hi, w_lo = _split_bf16(rw_ref[...])
    logits_t = (jnp.dot(x_hi, w_hi, preferred_element_type=F32) + jnp.dot(x_hi, w_lo, preferred_element_type=F32)
                + jnp.dot(x_lo, w_hi, preferred_element_type=F32))
    logits = logits_t.T[0:N_EXPERTS]
    scores = jax.nn.sigmoid(logits)
    biased = scores + rb_ref[...]
    a = [biased[k * G:(k + 1) * G] for k in range(K)]
    sc = [scores[k * G:(k + 1) * G] for k in range(K)]
    hi01, lo01 = jnp.maximum(a[0], a[1]), jnp.minimum(a[0], a[1])
    hi23, lo23 = jnp.maximum(a[2], a[3]), jnp.minimum(a[2], a[3])
    grp_score = jnp.maximum(hi01, hi23) + jnp.maximum(jnp.minimum(hi01, hi23), jnp.maximum(lo01, lo23))
    gidx = lax.broadcasted_iota(I32, (G, tm), 0)
    g_best = jnp.max(grp_score, axis=0, keepdims=True)
    g_sel = jnp.min(jnp.where(grp_score == g_best, gidx, G), axis=0, keepdims=True)
    in_grp = gidx == g_sel
    cand = [jnp.sum(jnp.where(in_grp, a[k], 0.0), axis=0, keepdims=True) for k in range(K)]
    cand_sc = [jnp.sum(jnp.where(in_grp, sc[k], 0.0), axis=0, keepdims=True) for k in range(K)]

    def first_argmax(vals):
        best, idx = vals[0], jnp.zeros((1, tm), I32)
        for k in range(1, K):
            gt = vals[k] > best
            best = jnp.where(gt, vals[k], best)
            idx = jnp.where(gt, k, idx)
        return idx

    i1 = first_argmax(cand)
    i2 = first_argmax([jnp.where(i1 == k, -jnp.inf, cand[k]) for k in range(K)])
    v1 = sum(jnp.where(i1 == k, cand_sc[k], 0.0) for k in range(K))
    v2 = sum(jnp.where(i2 == k, cand_sc[k], 0.0) for k in range(K))
    den = v1 + v2
    eidx_ref[0:1, :] = g_sel * K + i1
    eidx_ref[1:2, :] = g_sel * K + i2
    gate_ref[0:1, :] = v1 / den
    gate_ref[1:2, :] = v2 / den

    member = jnp.concatenate(
        [jnp.where(in_grp & ((i1 == k) | (i2 == k)), 1.0, 0.0) for k in range(K)], axis=0)
    trow = lax.broadcasted_iota(I32, (tm, tm), 0)
    tcol = lax.broadcasted_iota(I32, (tm, tm), 1)
    before = (trow < tcol).astype(BF16)
    prefix = jnp.dot(member.astype(BF16), before, preferred_element_type=F32) + carry_sc[...]
    r1 = sum(jnp.sum(jnp.where(in_grp & (i1 == k), prefix[k * G:(k + 1) * G], 0.0), axis=0, keepdims=True)
             for k in range(K))
    r2 = sum(jnp.sum(jnp.where(in_grp & (i2 == k), prefix[k * G:(k + 1) * G], 0.0), axis=0, keepdims=True)
             for k in range(K))
    rank_ref[0:1, :] = r1.astype(I32)
    rank_ref[1:2, :] = r2.astype(I32)
    carry_sc[...] = carry_sc[...] + jnp.sum(member, axis=1, keepdims=True)
    cnt_ref[...] = jnp.broadcast_to(carry_sc[...], cnt_ref.shape).astype(I32)


def _slot_of_expert(e):
    return (e % EXPERTS_PER_GROUP) * N_EXPERT_GROUPS + e // EXPERTS_PER_GROUP


def _expert_of_slot(r):
    return (r % N_EXPERT_GROUPS) * EXPERTS_PER_GROUP + r // N_EXPERT_GROUPS


def _route(x, router_w, router_bias):
    t_tokens, d = x.shape
    tm = ROUTER_ROWS
    slot_expert = _expert_of_slot(jnp.arange(N_EXPERTS))
    rw = jnp.pad(router_w[:, slot_expert], ((0, 0), (0, LANES - N_EXPERTS)))
    rb = router_bias[slot_expert].reshape(N_EXPERTS, 1)
    tok = pl.BlockSpec((TOP_K, tm), lambda i: (0, i))
    return pl.pallas_call(
        _router_kernel,
        grid=(t_tokens // tm,),
        in_specs=[
            pl.BlockSpec((tm, d), lambda i: (i, 0)),
            pl.BlockSpec((d, LANES), lambda i: (0, 0)),
            pl.BlockSpec((N_EXPERTS, 1), lambda i: (0, 0)),
        ],
        out_specs=[tok, tok, tok, pl.BlockSpec((N_EXPERTS, LANES), lambda i: (0, 0))],
        out_shape=[
            jax.ShapeDtypeStruct((TOP_K, t_tokens), I32),
            jax.ShapeDtypeStruct((TOP_K, t_tokens), F32),
            jax.ShapeDtypeStruct((TOP_K, t_tokens), I32),
            jax.ShapeDtypeStruct((N_EXPERTS, LANES), I32),
        ],
        scratch_shapes=[pltpu.VMEM((N_EXPERTS, 1), F32)],
        compiler_params=_params("arbitrary"),
        name="router",
    )(x, rw, rb)


def _ffn_kernel(nused_ref, iexp_ref, nval_ref, src_ref, src_next_ref, xp_hbm, wg_ref, wu_ref, wd_ref,
                yk_hbm, xbuf, hg_sc, hu_sc, hid_sc, obuf, gsem, ssem):
    item = pl.program_id(0)
    step = pl.program_id(1)
    n_used = nused_ref[0]
    slot = lax.rem(item, 2)
    sub = MOE_SUB_ROWS
    half = MOE_K_CHUNK // 2

    n_tokens = xp_hbm.shape[0]
    unroll = MOE_DMA_UNROLL

    def for_rows(count, fn):
        def body(i, carry):
            fn(i)
            return carry
        lax.fori_loop(0, count, body, 0)

    def for_rows_unrolled(count, fn):
        def body(b, carry):
            base = pl.multiple_of(b * unroll, unroll)
            for u in range(unroll):
                fn(base + u)
            return carry
        n_full = count // unroll
        lax.fori_loop(0, n_full, body, 0)
        lax.fori_loop(n_full * unroll, count, lambda i, c: (fn(i), c)[1], 0)

    def gather_rows(slot_rows, count, to_slot):
        def start(i):
            slot_id = slot_rows[0, i]
            if n_tokens & (n_tokens - 1) == 0:
                tok = slot_id & (n_tokens - 1)
            else:
                tok = lax.rem(slot_id, n_tokens)
            pltpu.make_async_copy(xp_hbm.at[pl.ds(tok, 1), :], xbuf.at[to_slot, pl.ds(i, 1), :],
                                  gsem.at[to_slot]).start(priority=1)
        for_rows_unrolled(count, start)

    def scatter_rows(count):
        def start(i):
            pltpu.make_async_copy(obuf.at[pl.ds(i, 1), :], yk_hbm.at[pl.ds(src_ref[0, i], 1), :],
                                  ssem).start(priority=1)
        for_rows_unrolled(count, start)

    def wait_rows(count, n_rows_copy):
        n_full = count // unroll
        for_rows(n_full, lambda i: n_rows_copy(unroll).wait())
        for_rows(count - n_full * unroll, lambda i: n_rows_copy(1).wait())

    def gathered(to_slot):
        return lambda n: pltpu.make_async_copy(xp_hbm.at[pl.ds(0, n), :], xbuf.at[to_slot, pl.ds(0, n), :],
                                               gsem.at[to_slot])

    def scattered(n):
        return pltpu.make_async_copy(obuf.at[pl.ds(0, n), :], yk_hbm.at[pl.ds(0, n), :], ssem)

    @pl.when(item < n_used)
    def _():
        n_rows = nval_ref[item]
        n_sub = (n_rows + sub - 1) // sub

        @pl.when(step == 0)
        def _():
            @pl.when(item == 0)
            def _():
                xbuf[...] = jnp.zeros_like(xbuf)
                gather_rows(src_ref, n_rows, 0)

            @pl.when(item + 1 < n_used)
            def _():
                gather_rows(src_next_ref, nval_ref[item + 1], 1 - slot)

            wait_rows(n_rows, gathered(slot))

        for c in range(MOE_K_STEPS):
            @pl.when(step == c)
            def _(c=c):
                wg = wg_ref[...].astype(BF16)
                wu = wu_ref[...].astype(BF16)

                def sub_block(r):
                    rows = pl.ds(pl.multiple_of(r * sub, sub), sub)
                    lo, hi = _unpack_bf16_pair(xbuf[slot, rows, c * half:(c + 1) * half])
                    xk = jnp.concatenate([lo, hi], axis=1).astype(BF16)
                    g = jnp.dot(xk, wg, preferred_element_type=F32)
                    u = jnp.dot(xk, wu, preferred_element_type=F32)
                    if c > 0:
                        g = g + hg_sc[rows, :]
                        u = u + hu_sc[rows, :]
                    if c < MOE_K_STEPS - 1:
                        hg_sc[rows, :] = g
                        hu_sc[rows, :] = u
                    else:
                        hid_sc[rows, :] = (_silu(g) * u).astype(BF16)

                for_rows(n_sub, sub_block)

        @pl.when(step == MOE_K_STEPS)
        def _():
            @pl.when(item > 0)
            def _():
                wait_rows(nval_ref[item - 1], scattered)

            wd = wd_ref[...].astype(BF16)

            def sub_block(r):
                rows = pl.ds(pl.multiple_of(r * sub, sub), sub)
                o = jnp.dot(hid_sc[rows, :], wd, preferred_element_type=F32)
                obuf[rows, :] = lax.bitcast_convert_type(o.astype(BF16).astype(F32), U32) >> 16

            for_rows(n_sub, sub_block)

        @pl.when(step == MOE_K_STEPS + 1)
        def _():
            wd = wd_ref[...].astype(BF16)

            def sub_block(r):
                rows = pl.ds(pl.multiple_of(r * sub, sub), sub)
                o = jnp.dot(hid_sc[rows, :], wd, preferred_element_type=F32)
                obuf[rows, :] = obuf[rows, :] | (lax.bitcast_convert_type(o.astype(BF16).astype(F32), U32)
                                                 & HIGH_HALF)

            for_rows(n_sub, sub_block)
            scatter_rows(n_rows)

            @pl.when(item == n_used - 1)
            def _():
                wait_rows(n_rows, scattered)


def _expert_ffn(xp, item_expert, item_rows, n_used, row_slot, w_gate, w_up, w_down, layer):
    t_tokens, dp = xp.shape
    d = 2 * dp
    rows = MOE_ROWS
    n_items = item_expert.shape[0]
    oc = d // MOE_O_STEPS
    last = MOE_K_STEPS + MOE_O_STEPS - 1

    def pos(i, s, nu):
        used = i < nu[0]
        return jnp.where(used, i, nu[0] - 1), jnp.where(used, s, last)

    def w_in_map(i, s, nu, ie, nv):
        ii, ss = pos(i, s, nu)
        return layer, ie[ii], jnp.minimum(ss, MOE_K_STEPS - 1), 0

    def w_down_map(i, s, nu, ie, nv):
        ii, ss = pos(i, s, nu)
        early = ss < MOE_K_STEPS
        return (layer, ie[jnp.where(early, jnp.maximum(ii - 1, 0), ii)], 0,
                jnp.where(early, MOE_O_STEPS - 1, ss - MOE_K_STEPS))

    def rows_map(i, s, nu, ie, nv):
        return pos(i, s, nu)[0], 0, 0

    def next_rows_map(i, s, nu, ie, nv):
        return jnp.minimum(pos(i, s, nu)[0] + 1, n_items - 1), 0, 0

    idx_block = (None, 1, rows)
    return pl.pallas_call(
        _ffn_kernel,
        grid_spec=pltpu.PrefetchScalarGridSpec(
            num_scalar_prefetch=3,
            grid=(n_items, MOE_K_STEPS + MOE_O_STEPS),
            in_specs=[
                pl.BlockSpec(idx_block, rows_map, memory_space=pltpu.SMEM),
                pl.BlockSpec(idx_block, next_rows_map, memory_space=pltpu.SMEM),
                pl.BlockSpec(memory_space=pl.ANY),
                pl.BlockSpec((None, None, MOE_K_CHUNK, D_FF), w_in_map),
                pl.BlockSpec((None, None, MOE_K_CHUNK, D_FF), w_in_map),
                pl.BlockSpec((None, None, D_FF, oc), w_down_map),
            ],
            out_specs=pl.BlockSpec(memory_space=pl.ANY),
            scratch_shapes=[
                pltpu.VMEM((2, rows, dp), U32),
                pltpu.VMEM((rows, D_FF), F32),
                pltpu.VMEM((rows, D_FF), F32),
                pltpu.VMEM((rows, D_FF), BF16),
                pltpu.VMEM((rows, dp), U32),
                pltpu.SemaphoreType.DMA((2,)),
                pltpu.SemaphoreType.DMA(()),
            ],
        ),
        out_shape=jax.ShapeDtypeStruct((TOP_K * t_tokens, dp), U32),
        compiler_params=_params("arbitrary", "arbitrary"),
        name="moe_ffn",
    )(n_used, item_expert, item_rows, row_slot.reshape(n_items, 1, rows), row_slot.reshape(n_items, 1, rows),
      xp, w_gate, w_up, w_down)


def _combine_kernel(y0_ref, y1_ref, gate_ref, x_ref, g_ref, b_ref, o_ref, obf_ref):
    gates = gate_ref[...]
    ffn = (gates[:, 0:1] * jnp.concatenate(_unpack_bf16_pair(y0_ref[...]), axis=1)
           + gates[:, 1:2] * jnp.concatenate(_unpack_bf16_pair(y1_ref[...]), axis=1))
    y = _layer_norm_rows(ALPHA * x_ref[...] + ffn, g_ref[...], b_ref[...])
    o_ref[...] = y
    obf_ref[...] = y.astype(BF16)


def _combine_layer_norm(x, yk, gate, g, b):
    t_tokens, d = x.shape
    tm = LN_ROWS
    n_steps = t_tokens // tm
    row = pl.BlockSpec((tm, d), lambda i: (i, 0))
    vec = pl.BlockSpec((1, d), lambda i: (0, 0))
    return pl.pallas_call(
        _combine_kernel,
        grid=(n_steps,),
        in_specs=[
            pl.BlockSpec((tm, d // 2), lambda i: (i, 0)),
            pl.BlockSpec((tm, d // 2), lambda i: (i + n_steps, 0)),
            pl.BlockSpec((tm, TOP_K), lambda i: (i, 0)),
            row, vec, vec,
        ],
        out_specs=[row, row],
        out_shape=[jax.ShapeDtypeStruct((t_tokens, d), F32), jax.ShapeDtypeStruct((t_tokens, d), BF16)],
        compiler_params=_params("arbitrary"),
        name="moe_combine",
    )(yk, yk, gate.T, x, g.reshape(1, d), b.reshape(1, d))


def _moe_layer_norm(x, xp, layer, router_w, router_bias, w_gate, w_up, w_down, ln_g, ln_b):
    t_tokens, _ = x.shape
    rows = MOE_ROWS
    eidx, gate, rank, counts = _route(x, router_w, router_bias)
    cnt = counts[:, 0]
    items_per_slot = (cnt + rows - 1) // rows
    item_end = jnp.cumsum(items_per_slot)
    row_start = (item_end - items_per_slot) * rows
    in_slot = _slot_of_expert(eidx)[..., None] == jnp.arange(N_EXPERTS, dtype=I32)
    dest = (jnp.sum(jnp.where(in_slot, row_start, 0), axis=-1) + rank).reshape(-1)
    n_items = (t_tokens * TOP_K) // rows + N_EXPERTS
    n_used = item_end[-1].astype(I32).reshape(1)
    item_id = jnp.arange(n_items, dtype=I32)
    item_slot = jnp.minimum(jnp.sum(item_id[:, None] >= item_end[None, :], axis=1), N_EXPERTS - 1)
    item_expert = _expert_of_slot(item_slot).astype(I32)
    item_rows = jnp.clip(row_start[item_slot] + cnt[item_slot] - item_id * rows, 0, rows)
    item_rows = jnp.where(item_id < n_used[0], item_rows, 0).astype(I32)
    row_slot = jnp.zeros((n_items * rows,), I32).at[dest].set(jnp.arange(TOP_K * t_tokens, dtype=I32))

    yk = _expert_ffn(xp, item_expert, item_rows, n_used, row_slot, w_gate, w_up, w_down, layer)
    return _combine_layer_norm(x, yk, gate, ln_g, ln_b)


def kernel(x, rel_bias, even_w_in, even_w_out, diff_lambda, diff_subln_g, odd_w_in, odd_w_out, ssd_conv_w, ssd_conv_b, ssd_dt_bias, ssd_a_log, ssd_d, ssd_norm_g, gmlp_ln_g, gmlp_ln_b, gmlp_ws, gmlp_bs, router_w, router_bias, moe_w_gate, moe_w_up, moe_w_down, ln_mix_g, ln_mix_b, ln_ffn_g, ln_ffn_b):
    bsz, s, d = x.shape
    t_tokens = bsz * s
    xf = x.reshape(t_tokens, d)
    xb = xf.astype(BF16)
    moe_args = (router_w, router_bias, moe_w_gate, moe_w_up, moe_w_down)

    lambda_init = 0.8 - 0.6 * math.exp(-0.3 * 0)
    h = _project([xb], even_w_in[0], [0], 0, EVEN_IN, BF16, tm=1024, tn=768)
    attn = _even_attention(h.reshape(bsz, s, EVEN_IN), _bias_tiles(rel_bias), diff_lambda[0],
                           diff_subln_g[0], lambda_init)
    mix = _project([attn.reshape(t_tokens, d)], even_w_out[0], [0], 0, d, F32, tm=1024, tn=512)
    xf, xp = _residual_layer_norm(xf, mix, ln_mix_g[0], ln_mix_b[0])
    xf, xb = _moe_layer_norm(xf, xp, 0, *moe_args, ln_ffn_g[0], ln_ffn_b[0])

    zx_cols = SSD_D_INNER + SSD_CONV_DIM
    w_in_t = jnp.swapaxes(odd_w_in[0], 0, 1)
    h_zx = _project([xb], w_in_t, [0], 0, zx_cols, F32, tm=1024, tn=512, w_transposed=True)
    h_dt = _project([xb], w_in_t, [0], zx_cols, LANES, F32, tm=1024, tn=LANES, w_transposed=True)
    h_uv = _project([xb], w_in_t, [0], C_IN, 2 * GMLP_WIDTH, F32, tm=1024, tn=512, w_transposed=True)
    y_ssd = _ssd(h_zx, h_dt, bsz, ssd_conv_w[0], ssd_conv_b[0], ssd_dt_bias[0], ssd_a_log[0], ssd_d[0],
                 ssd_norm_g[0])
    y_gmlp = _gmlp(h_uv, gmlp_ln_g[0], gmlp_ln_b[0], gmlp_ws[0], gmlp_bs[0])
    mix = _project([y_ssd, y_gmlp], odd_w_out[0], [0, SSD_D_INNER // GMLP_WIDTH], 0, d, F32, tm=512, tn=512)
    xf, xp = _residual_layer_norm(xf, mix, ln_mix_g[1], ln_mix_b[1])
    xf, xb = _moe_layer_norm(xf, xp, 1, *moe_args, ln_ffn_g[1], ln_ffn_b[1])
    return xf.reshape(bsz, s, d)
```

```python
import functools
import math

import numpy as np
import jax
import jax.numpy as jnp
from jax import lax
from jax.experimental import pallas as pl
from jax.experimental.pallas import tpu as pltpu

F32 = jnp.float32
BF16 = jnp.bfloat16
I32 = jnp.int32

D_MODEL = 4096
DEPTH = 2
HEAD_DIM = 128
N_HEADS_A = 16
N_HEADS_B = 16
DIFF_HALF = HEAD_DIM // 2
N_BUCKETS = 32
MAX_DISTANCE = 128
A_IN = 3 * N_HEADS_A * HEAD_DIM
EVEN_IN = A_IN + 3 * N_HEADS_B * HEAD_DIM
DIFF_SCALE = DIFF_HALF ** -0.5
SB_SCALE = HEAD_DIM ** -0.5

SSD_HEAD_DIM = 64
SSD_D_INNER = D_MODEL
SSD_N_HEADS = SSD_D_INNER // SSD_HEAD_DIM
SSD_N_GROUPS = 8
SSD_HEADS_PER_GROUP = SSD_N_HEADS // SSD_N_GROUPS
SSD_D_STATE = 128
SSD_CONV = 4
SSD_CHUNK = 128
SSD_GN = SSD_N_GROUPS * SSD_D_STATE
SSD_CONV_DIM = SSD_D_INNER + 2 * SSD_GN
SSD_GROUP_DIM = SSD_D_INNER // SSD_N_GROUPS
C_IN = SSD_D_INNER + SSD_CONV_DIM + SSD_N_HEADS
GMLP_WIDTH = D_MODEL // 2
GMLP_GROUPS = 8
GMLP_GROUP_DIM = GMLP_WIDTH // GMLP_GROUPS
GMLP_CHUNK = 128

N_EXPERTS = 32
N_EXPERT_GROUPS = 8
EXPERTS_PER_GROUP = N_EXPERTS // N_EXPERT_GROUPS
TOP_K = 2
D_FF = 768

ALPHA = (2 * DEPTH) ** 0.25
LN_EPS = 1e-5

LANES = 128
SUBLANES = 8
VMEM_LIMIT_BYTES = 58 * 1024 * 1024

ATTN_TILE = 256
ATTN_HEADS_PER_STEP = 8
ATTN_ONES_ROWS = 16
MOE_SUB_ROWS = 272
MOE_ROWS = 3 * MOE_SUB_ROWS
MOE_K_STEPS = 4
MOE_K_CHUNK = D_MODEL // MOE_K_STEPS
MOE_O_STEPS = 2
MOE_DMA_UNROLL = 8
U32 = jnp.uint32
HIGH_HALF = np.uint32(0xFFFF0000)
ROUTER_ROWS = 512
LN_ROWS = 256


def _params(*sem):
    return pltpu.CompilerParams(dimension_semantics=sem, vmem_limit_bytes=VMEM_LIMIT_BYTES)


def _proj_kernel(*refs, n_in, w_transposed):
    x_refs = refs[:n_in]
    w_refs = refs[n_in:2 * n_in]
    o_ref = refs[2 * n_in]
    wbf_refs = refs[2 * n_in + 1:]

    @pl.when(pl.program_id(1) == 0)
    def _():
        for w_ref, wbf_ref in zip(w_refs, wbf_refs):
            wbf_ref[...] = w_ref[...].astype(BF16)

    mm = _dot_nt if w_transposed else functools.partial(jnp.dot, preferred_element_type=F32)
    acc = mm(x_refs[0][...], wbf_refs[0][...])
    for x_ref, wbf_ref in zip(x_refs[1:], wbf_refs[1:]):
        acc = acc + mm(x_ref[...], wbf_ref[...])
    o_ref[...] = acc.astype(o_ref.dtype)


def _project(xs, w, row_blocks, col0, n_cols, out_dtype, tm, tn, w_transposed=False):
    m = xs[0].shape[0]
    n_in = len(xs)
    in_specs = [pl.BlockSpec((tm, x.shape[1]), lambda j, i: (i, 0)) for x in xs]
    for x, rb in zip(xs, row_blocks):
        k = x.shape[1]
        if not w_transposed:
            in_specs.append(pl.BlockSpec((k, tn), lambda j, i, rb=rb: (rb, j + col0 // tn)))
        elif col0 % tn == 0:
            in_specs.append(pl.BlockSpec((tn, k), lambda j, i, rb=rb: (j + col0 // tn, rb)))
        else:
            in_specs.append(pl.BlockSpec((pl.Element(tn), pl.Element(k)),
                                         lambda j, i, rb=rb, k=k: (pl.multiple_of(col0 + j * tn, SUBLANES),
                                                                   rb * k)))
    return pl.pallas_call(
        functools.partial(_proj_kernel, n_in=n_in, w_transposed=w_transposed),
        grid=(n_cols // tn, m // tm),
        in_specs=in_specs,
        out_specs=pl.BlockSpec((tm, tn), lambda j, i: (i, j)),
        out_shape=jax.ShapeDtypeStruct((m, n_cols), out_dtype),
        scratch_shapes=[pltpu.VMEM((tn, x.shape[1]) if w_transposed else (x.shape[1], tn), BF16) for x in xs],
        compiler_params=_params("arbitrary", "arbitrary"),
        name="proj",
    )(*xs, *([w] * n_in))


def _layer_norm_rows(y, g, b):
    mu = jnp.mean(y, axis=-1, keepdims=True)
    yc = y - mu
    var = jnp.mean(yc * yc, axis=-1, keepdims=True)
    return yc * lax.rsqrt(var + LN_EPS) * g + b


def _pack_bf16_pair(lo, hi):
    lo_bits = lax.bitcast_convert_type(lo.astype(BF16).astype(F32), U32)
    hi_bits = lax.bitcast_convert_type(hi.astype(BF16).astype(F32), U32)
    return (lo_bits >> 16) | (hi_bits & HIGH_HALF)


def _unpack_bf16_pair(p):
    return (lax.bitcast_convert_type(p << 16, F32), lax.bitcast_convert_type(p & HIGH_HALF, F32))


def _pack_rows(y):
    half = MOE_K_CHUNK // 2
    parts = [_pack_bf16_pair(y[:, c * MOE_K_CHUNK:c * MOE_K_CHUNK + half],
                             y[:, c * MOE_K_CHUNK + half:(c + 1) * MOE_K_CHUNK])
             for c in range(y.shape[1] // MOE_K_CHUNK)]
    return jnp.concatenate(parts, axis=1)


def _res_ln_kernel(x_ref, mix_ref, g_ref, b_ref, o_ref, opk_ref):
    y = _layer_norm_rows(ALPHA * x_ref[...] + mix_ref[...], g_ref[...], b_ref[...])
    o_ref[...] = y
    opk_ref[...] = _pack_rows(y)


def _residual_layer_norm(x, mix, g, b):
    m, d = x.shape
    tm = LN_ROWS
    row = pl.BlockSpec((tm, d), lambda i: (i, 0))
    half_row = pl.BlockSpec((tm, d // 2), lambda i: (i, 0))
    vec = pl.BlockSpec((1, d), lambda i: (0, 0))
    return pl.pallas_call(
        _res_ln_kernel,
        grid=(m // tm,),
        in_specs=[row, row, vec, vec],
        out_specs=[row, half_row],
        out_shape=[jax.ShapeDtypeStruct((m, d), F32), jax.ShapeDtypeStruct((m, d // 2), U32)],
        compiler_params=_params("arbitrary"),
        name="res_ln",
    )(x, mix, g.reshape(1, d), b.reshape(1, d))


def _bias_tile_kernel(rb_ref, o_ref):
    head = pl.program_id(0)
    t = ATTN_TILE
    key = lax.broadcasted_iota(I32, (t, t), 0)
    qry = lax.broadcasted_iota(I32, (t, t), 1)
    max_exact = N_BUCKETS // 2
    for d in range(2):
        n = jnp.maximum(d * t + qry - key, 0)
        nf = jnp.maximum(n, 1).astype(F32)
        large = max_exact + (jnp.log(nf / max_exact) / math.log(MAX_DISTANCE / max_exact)
                             * (N_BUCKETS - max_exact)).astype(I32)
        large = jnp.minimum(large, N_BUCKETS - 1)
        bucket = jnp.where(n < max_exact, n, large)
        acc = jnp.zeros((t, t), F32)
        for bkt in range(N_BUCKETS):
            acc = jnp.where(bucket == bkt, rb_ref[bkt, head], acc)
        o_ref[d] = acc


def _bias_tiles(rel_bias):
    t = ATTN_TILE
    return pl.pallas_call(
        _bias_tile_kernel,
        grid=(N_HEADS_A,),
        in_specs=[pl.BlockSpec(memory_space=pltpu.SMEM)],
        out_specs=pl.BlockSpec((None, 2, t, t), lambda h: (h, 0, 0, 0)),
        out_shape=jax.ShapeDtypeStruct((N_HEADS_A, 2, t, t), F32),
        compiler_params=_params("arbitrary"),
        name="bias_tiles",
    )(rel_bias)


def _dot_nt(a, b):
    return lax.dot_general(a, b, (((1,), (1,)), ((), ())), preferred_element_type=F32)


def _head_cols(h):
    return slice(h * HEAD_DIM, (h + 1) * HEAD_DIM)


def _value_transposes(v_ref, vt_sc):
    t = ATTN_TILE
    ones = jnp.ones((ATTN_ONES_ROWS, t), BF16)
    for h in range(ATTN_HEADS_PER_STEP):
        for j in range(v_ref.shape[0] // t):
            vb = v_ref[j * t:(j + 1) * t, _head_cols(h)]
            vt_sc[h, j, 0:HEAD_DIM, :] = vb.astype(F32).T.astype(BF16)
            vt_sc[h, j, HEAD_DIM:, :] = ones


def _diff_attention(dl_ref, g_ref, q_ref, k_ref, vt_sc, bias_ref, o_ref, m_sc, acc_sc, lambda_init):
    t = ATTN_TILE
    qi = pl.program_id(2)
    lane = lax.broadcasted_iota(I32, (t, HEAD_DIM), 1)
    q_maps = []
    for h in range(ATTN_HEADS_PER_STEP):
        q = q_ref[:, _head_cols(h)] * DIFF_SCALE
        zero = jnp.zeros_like(q)
        q_maps.append((jnp.where(lane < DIFF_HALF, q, zero), jnp.where(lane >= DIFF_HALF, q, zero)))

    chains = [(h, mp) for h in range(ATTN_HEADS_PER_STEP) for mp in range(2)]

    def block(j, bias_of_head, mask, first):
        kv_start = pl.multiple_of(j * t, t)
        kbs = [k_ref[pl.ds(kv_start, t), _head_cols(h)] for h in range(ATTN_HEADS_PER_STEP)]
        scores = [_dot_nt(kbs[h], q_maps[h][mp]) for h, mp in chains]
        probs, alphas = [], []
        for (h, mp), s in zip(chains, scores):
            bias = bias_of_head(h)
            uniform = bias.shape == (1, 1)
            if not uniform:
                s = s + bias
            if mask is not None:
                s = jnp.where(mask, s, -jnp.inf)
            m_new = jnp.max(s, axis=0, keepdims=True)
            if uniform:
                m_new = m_new + bias
            if not first:
                m_old = m_sc[h, mp]
                m_new = jnp.maximum(m_old, m_new)
                alphas.append(jnp.exp(m_old - m_new))
            m_sc[h, mp] = m_new
            probs.append(jnp.exp(s - (m_new - bias if uniform else m_new)).astype(BF16))
        for i, (h, mp) in enumerate(chains):
            pv = jnp.dot(vt_sc[h, j], probs[i], preferred_element_type=F32)
            acc_sc[h, mp] = pv if first else alphas[i] * acc_sc[h, mp] + pv

    key = lax.broadcasted_iota(I32, (t, t), 0)
    qry = lax.broadcasted_iota(I32, (t, t), 1)
    block(qi, lambda h: bias_ref[h, 0], qry >= key, True)

    @pl.when(qi >= 1)
    def _():
        block(qi - 1, lambda h: bias_ref[h, 1], None, False)

    def far_body(j, carry):
        block(j, lambda h: bias_ref[h, 1, 0:1, t - 1:t], None, False)
        return carry

    lax.fori_loop(0, qi - 1, far_body, 0)

    dl = dl_ref[...]
    lam = (jnp.exp(jnp.sum(dl[0:1] * dl[1:2], axis=-1, keepdims=True))
           - jnp.exp(jnp.sum(dl[2:3] * dl[3:4], axis=-1, keepdims=True)) + lambda_init)
    for h in range(ATTN_HEADS_PER_STEP):
        a0 = acc_sc[h, 0]
        a1 = acc_sc[h, 1]
        oa = (a0[0:HEAD_DIM] / a0[HEAD_DIM:HEAD_DIM + 1]
              - lam * (a1[0:HEAD_DIM] / a1[HEAD_DIM:HEAD_DIM + 1]))
        ms = jnp.mean(oa * oa, axis=0, keepdims=True)
        oa = (oa * lax.rsqrt(ms + LN_EPS)).T * g_ref[...] * (1.0 - lambda_init)
        o_ref[:, _head_cols(h)] = oa.astype(o_ref.dtype)


def _log_sigmoid(z):
    return jnp.minimum(z, 0.0) - jnp.log(1.0 + jnp.exp(-jnp.abs(z)))


def _split_bf16(x):
    hi = x.astype(BF16)
    lo = (x - hi.astype(F32)).astype(BF16)
    return hi, lo


def _stick_breaking_attention(q_ref, k_ref, vt_sc, o_ref, c_sc, acc_sc):
    t = ATTN_TILE
    qi = pl.program_id(2)
    key = lax.broadcasted_iota(I32, (t, t), 0)
    qry = lax.broadcasted_iota(I32, (t, t), 1)
    later = qry > key
    after = later.astype(BF16)

    heads = range(ATTN_HEADS_PER_STEP)

    def block(j, strict, first):
        kv_start = pl.multiple_of(j * t, t)
        zs = [_dot_nt(k_ref[pl.ds(kv_start, t), _head_cols(h)], q_ref[:, _head_cols(h)]) for h in heads]
        log_betas, splits, col_sums = [], [], []
        for h in heads:
            z = zs[h] * SB_SCALE
            log_beta = _log_sigmoid(z)
            log_1mb = log_beta - z
            if strict is not None:
                log_1mb = jnp.where(strict, log_1mb, 0.0)
            log_betas.append(log_beta)
            splits.append(_split_bf16(log_1mb))
            col_sums.append(jnp.sum(log_1mb, axis=0, keepdims=True))
        tails = [jnp.dot(after, hi, preferred_element_type=F32) + jnp.dot(after, lo, preferred_element_type=F32)
                 for hi, lo in splits]
        weights = []
        for h in heads:
            tail = tails[h] if first else tails[h] + c_sc[h]
            w = jnp.exp(log_betas[h] + tail)
            if strict is not None:
                w = jnp.where(strict, w, 0.0)
            weights.append(w.astype(BF16))
            c_sc[h] = col_sums[h] if first else c_sc[h] + col_sums[h]
        for h in heads:
            pv = jnp.dot(vt_sc[h, j][0:HEAD_DIM], weights[h], preferred_element_type=F32)
            acc_sc[h, 0, 0:HEAD_DIM] = pv if first else acc_sc[h, 0, 0:HEAD_DIM] + pv

    block(qi, later, True)

    def body(step, carry):
        block(qi - 1 - step, None, False)
        return carry

    lax.fori_loop(0, qi, body, 0)
    for h in range(ATTN_HEADS_PER_STEP):
        o_ref[:, _head_cols(h)] = acc_sc[h, 0, 0:HEAD_DIM].T.astype(o_ref.dtype)


def _attn_kernel(dl_ref, g_ref, q_ref, k_ref, v_ref, bias_ref, o_ref, vt_sc, m_sc, c_sc, acc_sc, *, lambda_init):
    grp = pl.program_id(1)

    @pl.when(pl.program_id(2) == 0)
    def _():
        _value_transposes(v_ref, vt_sc)

    @pl.when(grp < N_HEADS_A // ATTN_HEADS_PER_STEP)
    def _():
        _diff_attention(dl_ref, g_ref, q_ref, k_ref, vt_sc, bias_ref, o_ref, m_sc, acc_sc, lambda_init)

    @pl.when(grp >= N_HEADS_A // ATTN_HEADS_PER_STEP)
    def _():
        _stick_breaking_attention(q_ref, k_ref, vt_sc, o_ref, c_sc, acc_sc)


def _even_attention(h, bias_tiles, diff_lambda, subln_g, lambda_init):
    bsz, s, _ = h.shape
    t = ATTN_TILE
    nh = ATTN_HEADS_PER_STEP
    width = nh * HEAD_DIM
    groups_a = N_HEADS_A // nh
    n_groups = (N_HEADS_A + N_HEADS_B) // nh

    def q_col(g):
        return g + jnp.where(g >= groups_a, 2 * groups_a, 0)

    return pl.pallas_call(
        functools.partial(_attn_kernel, lambda_init=lambda_init),
        grid=(bsz, n_groups, s // t),
        in_specs=[
            pl.BlockSpec((4, DIFF_HALF), lambda b, g, qi: (0, 0)),
            pl.BlockSpec((1, HEAD_DIM), lambda b, g, qi: (0, 0)),
            pl.BlockSpec((None, t, width), lambda b, g, qi: (b, qi, q_col(g))),
            pl.BlockSpec((None, s, width), lambda b, g, qi: (b, 0, q_col(g) + groups_a)),
            pl.BlockSpec((None, s, width), lambda b, g, qi: (b, 0, q_col(g) + 2 * groups_a)),
            pl.BlockSpec((nh, 2, t, t), lambda b, g, qi: (jnp.minimum(g, groups_a - 1), 0, 0, 0)),
        ],
        out_specs=pl.BlockSpec((None, t, width), lambda b, g, qi: (b, qi, g)),
        out_shape=jax.ShapeDtypeStruct((bsz, s, n_groups * width), BF16),
        scratch_shapes=[
            pltpu.VMEM((nh, s // t, HEAD_DIM + ATTN_ONES_ROWS, t), BF16),
            pltpu.VMEM((nh, 2, 1, t), F32),
            pltpu.VMEM((nh, 1, t), F32),
            pltpu.VMEM((nh, 2, HEAD_DIM + ATTN_ONES_ROWS, t), F32),
        ],
        compiler_params=_params("arbitrary", "arbitrary", "arbitrary"),
        name="even_attention",
    )(diff_lambda, subln_g.reshape(1, HEAD_DIM), h, h, h, bias_tiles)


def _silu(x):
    return x * jax.nn.sigmoid(x)


def _softplus(x):
    return jnp.maximum(x, 0.0) + jnp.log(1.0 + jnp.exp(-jnp.abs(x)))


def _ssd_kernel(z_ref, x_ref, b_ref, c_ref, dt_ref, cw_ref, cb_ref, dtb_ref, alog_ref, dskip_ref, ng_ref,
                o_ref, pad_sc, state_sc, acst_sc, dtt_sc):
    pad_sc[0:SUBLANES, :] = jnp.zeros((SUBLANES, pad_sc.shape[1]), F32)
    state_sc[...] = jnp.zeros_like(state_sc)

    def chunk_body(chunk, carry):
        _ssd_chunk(chunk, z_ref, x_ref, b_ref, c_ref, dt_ref, cw_ref, cb_ref, dtb_ref, alog_ref, dskip_ref,
                   ng_ref, o_ref, pad_sc, state_sc, acst_sc, dtt_sc)
        return carry

    lax.fori_loop(0, x_ref.shape[0] // SSD_CHUNK, chunk_body, 0)


def _ssd_chunk(chunk, z_ref, x_ref, b_ref, c_ref, dt_ref, cw_ref, cb_ref, dtb_ref, alog_ref, dskip_ref, ng_ref,
               o_ref, pad_sc, state_sc, acst_sc, dtt_sc):
    grp = pl.program_id(1)
    L = SSD_CHUNK
    P2 = 2 * SSD_HEAD_DIM
    n_conv = SSD_GROUP_DIM + 2 * SSD_D_STATE
    halo = SUBLANES
    rows = pl.ds(pl.multiple_of(chunk * L, L), L)

    pad_sc[halo:halo + L, 0:SSD_GROUP_DIM] = x_ref[rows, :]
    pad_sc[halo:halo + L, SSD_GROUP_DIM:SSD_GROUP_DIM + SSD_D_STATE] = b_ref[rows, :]
    pad_sc[halo:halo + L, SSD_GROUP_DIM + SSD_D_STATE:n_conv] = c_ref[rows, :]
    conv = cb_ref[...]
    for j in range(SSD_CONV):
        start = halo - (SSD_CONV - 1) + j
        conv = conv + cw_ref[j:j + 1, :] * pad_sc[start:start + L, :]
    pad_sc[0:halo, :] = pad_sc[L:L + halo, :]
    xbc = _silu(conv)
    xg = xbc[:, 0:SSD_GROUP_DIM]
    bg = xbc[:, SSD_GROUP_DIM:SSD_GROUP_DIM + SSD_D_STATE]
    cg = xbc[:, SSD_GROUP_DIM + SSD_D_STATE:n_conv]
    xg_bf = xg.astype(BF16)

    dt = _softplus(dt_ref[rows, :] + dtb_ref[...])
    adt = dt * (-jnp.exp(alog_ref[...]))
    row = lax.broadcasted_iota(I32, (L, L), 0)
    col = lax.broadcasted_iota(I32, (L, L), 1)
    causal = row >= col
    acs = jnp.dot(causal.astype(F32), adt, preferred_element_type=F32, precision=lax.Precision.HIGHEST)
    acst_sc[...] = acs.T
    dtt_sc[...] = dt.T

    cb = _dot_nt(cg.astype(BF16), bg.astype(BF16))
    bgt = bg.T
    cg_bf = cg.astype(BF16)
    prev = state_sc[...]
    y_off = jnp.dot(cg_bf, prev.astype(BF16), preferred_element_type=F32)

    lane = lax.broadcasted_iota(I32, (L, P2), 1)
    first_head = lane < SSD_HEAD_DIM
    lane_row = lax.broadcasted_iota(I32, (1, P2), 1) < SSD_HEAD_DIM
    y_pairs = []
    for pair in range(SSD_HEADS_PER_GROUP // 2):
        x_pair = xg_bf[:, pair * P2:(pair + 1) * P2]
        y_diag, st, e_acs, decay = [], [], [], []
        for sub in range(2):
            head = grp * SSD_HEADS_PER_GROUP + 2 * pair + sub
            acs_row = acst_sc[pl.ds(head, 1), :]
            dt_row = dtt_sc[pl.ds(head, 1), :]
            row_b = jnp.broadcast_to(acs_row, (L, L))
            col_b = row_b.T
            decay_m = jnp.exp(jnp.where(causal, col_b - row_b, -jnp.inf))
            m = (cb * decay_m * dt_row).astype(BF16)
            y_diag.append(jnp.dot(m, x_pair, preferred_element_type=F32))
            acs_last = col_b[L - 1:L, :]
            w_row = jnp.exp(acs_last - acs_row) * dt_row
            st.append(jnp.dot((bgt * w_row).astype(BF16), x_pair, preferred_element_type=F32))
            e_acs.append(jnp.exp(col_b))
            decay.append(jnp.exp(acs_last))
        cols = slice(pair * P2, (pair + 1) * P2)
        y_pair = (jnp.where(first_head, y_diag[0], y_diag[1])
                  + jnp.where(first_head, e_acs[0], e_acs[1]) * y_off[:, cols])
        state_sc[:, cols] = (prev[:, cols] * jnp.where(lane_row, decay[0], decay[1])
                             + jnp.where(first_head, st[0], st[1]))
        y_pair = y_pair + dskip_ref[:, cols] * xg[:, cols]
        y_pairs.append(y_pair * _silu(z_ref[rows, cols]))

    y = jnp.concatenate(y_pairs, axis=-1)
    ms = jnp.mean(y * y, axis=-1, keepdims=True)
    o_ref[rows, :] = (y * lax.rsqrt(ms + LN_EPS) * ng_ref[...]).astype(o_ref.dtype)


def _ssd(h_zx, h_dt, bsz, conv_w, conv_b, dt_bias, a_log, d_skip, norm_g):
    t_tokens = h_zx.shape[0]
    n_chunks = t_tokens // bsz // SSD_CHUNK
    L = SSD_CHUNK
    gd, ns, G = SSD_GROUP_DIM, SSD_D_STATE, SSD_N_GROUPS
    n_conv = gd + 2 * ns

    def pack(p):
        xs = p[..., :SSD_D_INNER].reshape(p.shape[:-1] + (G, gd))
        bs_ = p[..., SSD_D_INNER:SSD_D_INNER + SSD_GN].reshape(p.shape[:-1] + (G, ns))
        cs = p[..., SSD_D_INNER + SSD_GN:].reshape(p.shape[:-1] + (G, ns))
        return jnp.moveaxis(jnp.concatenate([xs, bs_, cs], axis=-1), -2, 0)

    cw = pack(conv_w)
    cb = pack(conv_b.reshape(1, -1))
    pad_heads = LANES - SSD_N_HEADS
    dtb = jnp.pad(dt_bias, (0, pad_heads)).reshape(1, LANES)
    alog = jnp.pad(a_log, (0, pad_heads)).reshape(1, LANES)
    dskip = jnp.repeat(d_skip, SSD_HEAD_DIM).reshape(1, SSD_D_INNER)

    seq = n_chunks * L
    x_blk0 = SSD_D_INNER // gd
    b_blk0 = (2 * SSD_D_INNER) // ns
    c_blk0 = (2 * SSD_D_INNER + SSD_GN) // ns
    return pl.pallas_call(
        _ssd_kernel,
        grid=(bsz, G),
        in_specs=[
            pl.BlockSpec((seq, gd), lambda b, g: (b, g)),
            pl.BlockSpec((seq, gd), lambda b, g: (b, x_blk0 + g)),
            pl.BlockSpec((seq, ns), lambda b, g: (b, b_blk0 + g)),
            pl.BlockSpec((seq, ns), lambda b, g: (b, c_blk0 + g)),
            pl.BlockSpec((seq, LANES), lambda b, g: (b, 0)),
            pl.BlockSpec((None, SSD_CONV, n_conv), lambda b, g: (g, 0, 0)),
            pl.BlockSpec((None, 1, n_conv), lambda b, g: (g, 0, 0)),
            pl.BlockSpec((1, LANES), lambda b, g: (0, 0)),
            pl.BlockSpec((1, LANES), lambda b, g: (0, 0)),
            pl.BlockSpec((1, gd), lambda b, g: (0, g)),
            pl.BlockSpec((1, gd), lambda b, g: (0, g)),
        ],
        out_specs=pl.BlockSpec((seq, gd), lambda b, g: (b, g)),
        out_shape=jax.ShapeDtypeStruct((t_tokens, SSD_D_INNER), BF16),
        scratch_shapes=[
            pltpu.VMEM((SUBLANES + L, n_conv), F32),
            pltpu.VMEM((ns, gd), F32),
            pltpu.VMEM((LANES, L), F32),
            pltpu.VMEM((LANES, L), F32),
        ],
        compiler_params=_params("arbitrary", "arbitrary"),
        name="ssd",
    )(h_zx, h_zx, h_zx, h_zx, h_dt, cw, cb, dtb, alog, dskip, norm_g.reshape(1, SSD_D_INNER))


def _gelu_tanh(x):
    return 0.5 * x * (1.0 + jnp.tanh(math.sqrt(2.0 / math.pi) * (x + 0.044715 * (x * x * x))))


def _gmlp_kernel(uv_ref, lng_ref, lnb_ref, ws_ref, bs_ref, o_ref):
    L = GMLP_CHUNK
    gdim = GMLP_GROUP_DIM
    row = lax.broadcasted_iota(I32, (L, L), 0)
    col = lax.broadcasted_iota(I32, (L, L), 1)
    causal = row >= col
    for g in range(GMLP_GROUPS):
        cols = slice(g * gdim, (g + 1) * gdim)
        u = _gelu_tanh(uv_ref[:, cols])
        v = _gelu_tanh(uv_ref[:, GMLP_WIDTH + g * gdim:GMLP_WIDTH + (g + 1) * gdim])
        v = _layer_norm_rows(v, lng_ref[:, cols], lnb_ref[:, cols])
        ws = jnp.where(causal, ws_ref[g], 0.0).astype(BF16)
        sv = jnp.dot(ws, v.astype(BF16), preferred_element_type=F32) + bs_ref[:, cols]
        o_ref[:, cols] = (u * sv).astype(o_ref.dtype)


def _gmlp(h_uv, ln_g, ln_b, ws, bs):
    t_tokens = h_uv.shape[0]
    L = GMLP_CHUNK
    bs_cols = jnp.repeat(bs.T, GMLP_GROUP_DIM, axis=1)
    return pl.pallas_call(
        _gmlp_kernel,
        grid=(t_tokens // L,),
        in_specs=[
            pl.BlockSpec((L, 2 * GMLP_WIDTH), lambda i: (i, 0)),
            pl.BlockSpec((1, GMLP_WIDTH), lambda i: (0, 0)),
            pl.BlockSpec((1, GMLP_WIDTH), lambda i: (0, 0)),
            pl.BlockSpec((GMLP_GROUPS, L, L), lambda i: (0, 0, 0)),
            pl.BlockSpec((L, GMLP_WIDTH), lambda i: (0, 0)),
        ],
        out_specs=pl.BlockSpec((L, GMLP_WIDTH), lambda i: (i, 0)),
        out_shape=jax.ShapeDtypeStruct((t_tokens, GMLP_WIDTH), BF16),
        compiler_params=_params("arbitrary"),
        name="gmlp",
    )(h_uv, ln_g.reshape(1, GMLP_WIDTH), ln_b.reshape(1, GMLP_WIDTH), ws, bs_cols)


def _router_kernel(x_ref, rw_ref, rb_ref, eidx_ref, gate_ref, rank_ref, cnt_ref, carry_sc):
    tm = x_ref.shape[0]
    G, K = N_EXPERT_GROUPS, EXPERTS_PER_GROUP

    @pl.when(pl.program_id(0) == 0)
    def _():
        carry_sc[...] = jnp.zeros_like(carry_sc)

    x_hi, x_lo = _split_bf16(x_ref[...])
    w_hi, w_lo = _split_bf16(rw_ref[...])
    logits_t = (jnp.dot(x_hi, w_hi, preferred_element_type=F32) + jnp.dot(x_hi, w_lo, preferred_element_type=F32)
                + jnp.dot(x_lo, w_hi, preferred_element_type=F32))
    logits = logits_t.T[0:N_EXPERTS]
    scores = jax.nn.sigmoid(logits)
    biased = scores + rb_ref[...]
    a = [biased[k * G:(k + 1) * G] for k in range(K)]
    sc = [scores[k * G:(k + 1) * G] for k in range(K)]
    hi01, lo01 = jnp.maximum(a[0], a[1]), jnp.minimum(a[0], a[1])
    hi23, lo23 = jnp.maximum(a[2], a[3]), jnp.minimum(a[2], a[3])
    grp_score = jnp.maximum(hi01, hi23) + jnp.maximum(jnp.minimum(hi01, hi23), jnp.maximum(lo01, lo23))
    gidx = lax.broadcasted_iota(I32, (G, tm), 0)
    g_best = jnp.max(grp_score, axis=0, keepdims=True)
    g_sel = jnp.min(jnp.where(grp_score == g_best, gidx, G), axis=0, keepdims=True)
    in_grp = gidx == g_sel
    cand = [jnp.sum(jnp.where(in_grp, a[k], 0.0), axis=0, keepdims=True) for k in range(K)]
    cand_sc = [jnp.sum(jnp.where(in_grp, sc[k], 0.0), axis=0, keepdims=True) for k in range(K)]

    def first_argmax(vals):
        best, idx = vals[0], jnp.zeros((1, tm), I32)
        for k in range(1, K):
            gt = vals[k] > best
            best = jnp.where(gt, vals[k], best)
            idx = jnp.where(gt, k, idx)
        return idx

    i1 = first_argmax(cand)
    i2 = first_argmax([jnp.where(i1 == k, -jnp.inf, cand[k]) for k in range(K)])
    v1 = sum(jnp.where(i1 == k, cand_sc[k], 0.0) for k in range(K))
    v2 = sum(jnp.where(i2 == k, cand_sc[k], 0.0) for k in range(K))
    den = v1 + v2
    eidx_ref[0:1, :] = g_sel * K + i1
    eidx_ref[1:2, :] = g_sel * K + i2
    gate_ref[0:1, :] = v1 / den
    gate_ref[1:2, :] = v2 / den

    member = jnp.concatenate(
        [jnp.where(in_grp & ((i1 == k) | (i2 == k)), 1.0, 0.0) for k in range(K)], axis=0)
    trow = lax.broadcasted_iota(I32, (tm, tm), 0)
    tcol = lax.broadcasted_iota(I32, (tm, tm), 1)
    before = (trow < tcol).astype(BF16)
    prefix = jnp.dot(member.astype(BF16), before, preferred_element_type=F32) + carry_sc[...]
    r1 = sum(jnp.sum(jnp.where(in_grp & (i1 == k), prefix[k * G:(k + 1) * G], 0.0), axis=0, keepdims=True)
             for k in range(K))
    r2 = sum(jnp.sum(jnp.where(in_grp & (i2 == k), prefix[k * G:(k + 1) * G], 0.0), axis=0, keepdims=True)
             for k in range(K))
    rank_ref[0:1, :] = r1.astype(I32)
    rank_ref[1:2, :] = r2.astype(I32)
    carry_sc[...] = carry_sc[...] + jnp.sum(member, axis=1, keepdims=True)
    cnt_ref[...] = jnp.broadcast_to(carry_sc[...], cnt_ref.shape).astype(I32)


def _slot_of_expert(e):
    return (e % EXPERTS_PER_GROUP) * N_EXPERT_GROUPS + e // EXPERTS_PER_GROUP


def _expert_of_slot(r):
    return (r % N_EXPERT_GROUPS) * EXPERTS_PER_GROUP + r // N_EXPERT_GROUPS


def _route(x, router_w, router_bias):
    t_tokens, d = x.shape
    tm = ROUTER_ROWS
    slot_expert = _expert_of_slot(jnp.arange(N_EXPERTS))
    rw = jnp.pad(router_w[:, slot_expert], ((0, 0), (0, LANES - N_EXPERTS)))
    rb = router_bias[slot_expert].reshape(N_EXPERTS, 1)
    tok = pl.BlockSpec((TOP_K, tm), lambda i: (0, i))
    return pl.pallas_call(
        _router_kernel,
        grid=(t_tokens // tm,),
        in_specs=[
            pl.BlockSpec((tm, d), lambda i: (i, 0)),
            pl.BlockSpec((d, LANES), lambda i: (0, 0)),
            pl.BlockSpec((N_EXPERTS, 1), lambda i: (0, 0)),
        ],
        out_specs=[tok, tok, tok, pl.BlockSpec((N_EXPERTS, LANES), lambda i: (0, 0))],
        out_shape=[
            jax.ShapeDtypeStruct((TOP_K, t_tokens), I32),
            jax.ShapeDtypeStruct((TOP_K, t_tokens), F32),
            jax.ShapeDtypeStruct((TOP_K, t_tokens), I32),
            jax.ShapeDtypeStruct((N_EXPERTS, LANES), I32),
        ],
        scratch_shapes=[pltpu.VMEM((N_EXPERTS, 1), F32)],
        compiler_params=_params("arbitrary"),
        name="router",
    )(x, rw, rb)


def _ffn_kernel(nused_ref, iexp_ref, nval_ref, src_ref, src_next_ref, xp_hbm, wg_ref, wu_ref, wd_hbm,
                yk_hbm, xbuf, hg_sc, hu_sc, hid_sc, obuf, wd_buf, gsem, ssem, wsem, *, layer):
    item = pl.program_id(0)
    step = pl.program_id(1)
    n_used = nused_ref[0]
    slot = lax.rem(item, 2)
    sub = MOE_SUB_ROWS
    half = MOE_K_CHUNK // 2

    n_tokens = xp_hbm.shape[0]
    unroll = MOE_DMA_UNROLL

    def for_rows(count, fn):
        def body(i, carry):
            fn(i)
            return carry
        lax.fori_loop(0, count, body, 0)

    def for_rows_unrolled(count, fn):
        def body(b, carry):
            base = pl.multiple_of(b * unroll, unroll)
            for u in range(unroll):
                fn(base + u)
            return carry
        n_full = count // unroll
        lax.fori_loop(0, n_full, body, 0)
        lax.fori_loop(n_full * unroll, count, lambda i, c: (fn(i), c)[1], 0)

    def gather_rows(slot_rows, count, to_slot):
        def start(i):
            slot_id = slot_rows[0, i]
            if n_tokens & (n_tokens - 1) == 0:
                tok = slot_id & (n_tokens - 1)
            else:
                tok = lax.rem(slot_id, n_tokens)
            pltpu.make_async_copy(xp_hbm.at[pl.ds(tok, 1), :], xbuf.at[to_slot, pl.ds(i, 1), :],
                                  gsem.at[to_slot]).start(priority=1)
        for_rows_unrolled(count, start)

    def scatter_rows(count):
        def start(i):
            pltpu.make_async_copy(obuf.at[pl.ds(i, 1), :], yk_hbm.at[pl.ds(src_ref[0, i], 1), :],
                                  ssem).start(priority=1)
        for_rows_unrolled(count, start)

    def wait_rows(count, n_rows_copy):
        n_full = count // unroll
        for_rows(n_full, lambda i: n_rows_copy(unroll).wait())
        for_rows(count - n_full * unroll, lambda i: n_rows_copy(1).wait())

    def gathered(to_slot):
        return lambda n: pltpu.make_async_copy(xp_hbm.at[pl.ds(0, n), :], xbuf.at[to_slot, pl.ds(0, n), :],
                                               gsem.at[to_slot])

    def scattered(n):
        return pltpu.make_async_copy(obuf.at[pl.ds(0, n), :], yk_hbm.at[pl.ds(0, n), :], ssem)

    def down_copy(h):
        oc = wd_buf.shape[2]
        return pltpu.make_async_copy(wd_hbm.at[layer, iexp_ref[item], :, pl.ds(h * oc, oc)], wd_buf.at[h],
                                     wsem.at[h])

    @pl.when(item < n_used)
    def _():
        n_rows = nval_ref[item]
        n_sub = (n_rows + sub - 1) // sub

        @pl.when(step == 0)
        def _():
            @pl.when(item == 0)
            def _():
                xbuf[...] = jnp.zeros_like(xbuf)
                gather_rows(src_ref, n_rows, 0)

            @pl.when(item + 1 < n_used)
            def _():
                gather_rows(src_next_ref, nval_ref[item + 1], 1 - slot)

            wait_rows(n_rows, gathered(slot))

        for c in range(MOE_K_STEPS):
            @pl.when(step == c)
            def _(c=c):
                if c == 1:
                    down_copy(0).start(priority=1)
                if c == MOE_K_STEPS - 1:
                    down_copy(1).start(priority=1)
                wg = wg_ref[...].astype(BF16)
                wu = wu_ref[...].astype(BF16)

                def sub_block(r):
                    rows = pl.ds(pl.multiple_of(r * sub, sub), sub)
                    lo, hi = _unpack_bf16_pair(xbuf[slot, rows, c * half:(c + 1) * half])
                    xk = jnp.concatenate([lo, hi], axis=1).astype(BF16)
                    g = jnp.dot(xk, wg, preferred_element_type=F32)
                    u = jnp.dot(xk, wu, preferred_element_type=F32)
                    if c > 0:
                        g = g + hg_sc[rows, :]
                        u = u + hu_sc[rows, :]
                    if c < MOE_K_STEPS - 1:
                        hg_sc[rows, :] = g
                        hu_sc[rows, :] = u
                    else:
                        hid_sc[rows, :] = (_silu(g) * u).astype(BF16)

                for_rows(n_sub, sub_block)

        @pl.when(step == MOE_K_STEPS)
        def _():
            @pl.when(item > 0)
            def _():
                wait_rows(nval_ref[item - 1], scattered)

            down_copy(0).wait()
            wd = wd_buf[0].astype(BF16)

            def sub_block(r):
                rows = pl.ds(pl.multiple_of(r * sub, sub), sub)
                o = jnp.dot(hid_sc[rows, :], wd, preferred_element_type=F32)
                obuf[rows, :] = lax.bitcast_convert_type(o.astype(BF16).astype(F32), U32) >> 16

            for_rows(n_sub, sub_block)

        @pl.when(step == MOE_K_STEPS + 1)
        def _():
            down_copy(1).wait()
            wd = wd_buf[1].astype(BF16)

            def sub_block(r):
                rows = pl.ds(pl.multiple_of(r * sub, sub), sub)
                o = jnp.dot(hid_sc[rows, :], wd, preferred_element_type=F32)
                obuf[rows, :] = obuf[rows, :] | (lax.bitcast_convert_type(o.astype(BF16).astype(F32), U32)
                                                 & HIGH_HALF)

            for_rows(n_sub, sub_block)
            scatter_rows(n_rows)

            @pl.when(item == n_used - 1)
            def _():
                wait_rows(n_rows, scattered)


def _expert_ffn(xp, item_expert, item_rows, n_used, row_slot, w_gate, w_up, w_down, layer):
    t_tokens, dp = xp.shape
    d = 2 * dp
    rows = MOE_ROWS
    n_items = item_expert.shape[0]
    oc = d // MOE_O_STEPS
    last = MOE_K_STEPS + MOE_O_STEPS - 1

    def pos(i, s, nu):
        used = i < nu[0]
        return jnp.where(used, i, nu[0] - 1), jnp.where(used, s, last)

    def w_in_map(i, s, nu, ie, nv):
        ii, ss = pos(i, s, nu)
        return layer, ie[ii], jnp.minimum(ss, MOE_K_STEPS - 1), 0

    def rows_map(i, s, nu, ie, nv):
        return pos(i, s, nu)[0], 0, 0

    def next_rows_map(i, s, nu, ie, nv):
        return jnp.minimum(pos(i, s, nu)[0] + 1, n_items - 1), 0, 0

    idx_block = (None, 1, rows)
    return pl.pallas_call(
        functools.partial(_ffn_kernel, layer=layer),
        grid_spec=pltpu.PrefetchScalarGridSpec(
            num_scalar_prefetch=3,
            grid=(n_items, MOE_K_STEPS + MOE_O_STEPS),
            in_specs=[
                pl.BlockSpec(idx_block, rows_map, memory_space=pltpu.SMEM),
                pl.BlockSpec(idx_block, next_rows_map, memory_space=pltpu.SMEM),
                pl.BlockSpec(memory_space=pl.ANY),
                pl.BlockSpec((None, None, MOE_K_CHUNK, D_FF), w_in_map),
                pl.BlockSpec((None, None, MOE_K_CHUNK, D_FF), w_in_map),
                pl.BlockSpec(memory_space=pl.ANY),
            ],
            out_specs=pl.BlockSpec(memory_space=pl.ANY),
            scratch_shapes=[
                pltpu.VMEM((2, rows, dp), U32),
                pltpu.VMEM((rows, D_FF), F32),
                pltpu.VMEM((rows, D_FF), F32),
                pltpu.VMEM((rows, D_FF), BF16),
                pltpu.VMEM((rows, dp), U32),
                pltpu.VMEM((MOE_O_STEPS, D_FF, oc), F32),
                pltpu.SemaphoreType.DMA((2,)),
                pltpu.SemaphoreType.DMA(()),
                pltpu.SemaphoreType.DMA((MOE_O_STEPS,)),
            ],
        ),
        out_shape=jax.ShapeDtypeStruct((TOP_K * t_tokens, dp), U32),
        compiler_params=_params("arbitrary", "arbitrary"),
        name="moe_ffn",
    )(n_used, item_expert, item_rows, row_slot.reshape(n_items, 1, rows), row_slot.reshape(n_items, 1, rows),
      xp, w_gate, w_up, w_down)


def _combine_kernel(y0_ref, y1_ref, gate_ref, x_ref, g_ref, b_ref, o_ref, obf_ref):
    gates = gate_ref[...]
    ffn = (gates[:, 0:1] * jnp.concatenate(_unpack_bf16_pair(y0_ref[...]), axis=1)
           + gates[:, 1:2] * jnp.concatenate(_unpack_bf16_pair(y1_ref[...]), axis=1))
    y = _layer_norm_rows(ALPHA * x_ref[...] + ffn, g_ref[...], b_ref[...])
    o_ref[...] = y
    obf_ref[...] = y.astype(BF16)


def _combine_layer_norm(x, yk, gate, g, b):
    t_tokens, d = x.shape
    tm = LN_ROWS
    n_steps = t_tokens // tm
    row = pl.BlockSpec((tm, d), lambda i: (i, 0))
    vec = pl.BlockSpec((1, d), lambda i: (0, 0))
    return pl.pallas_call(
        _combine_kernel,
        grid=(n_steps,),
        in_specs=[
            pl.BlockSpec((tm, d // 2), lambda i: (i, 0)),
            pl.BlockSpec((tm, d // 2), lambda i: (i + n_steps, 0)),
            pl.BlockSpec((tm, TOP_K), lambda i: (i, 0)),
            row, vec, vec,
        ],
        out_specs=[row, row],
        out_shape=[jax.ShapeDtypeStruct((t_tokens, d), F32), jax.ShapeDtypeStruct((t_tokens, d), BF16)],
        compiler_params=_params("arbitrary"),
        name="moe_combine",
    )(yk, yk, gate.T, x, g.reshape(1, d), b.reshape(1, d))


def _moe_layer_norm(x, xp, layer, router_w, router_bias, w_gate, w_up, w_down, ln_g, ln_b):
    t_tokens, _ = x.shape
    rows = MOE_ROWS
    eidx, gate, rank, counts = _route(x, router_w, router_bias)
    cnt = counts[:, 0]
    items_per_slot = (cnt + rows - 1) // rows
    item_end = jnp.cumsum(items_per_slot)
    row_start = (item_end - items_per_slot) * rows
    in_slot = _slot_of_expert(eidx)[..., None] == jnp.arange(N_EXPERTS, dtype=I32)
    dest = (jnp.sum(jnp.where(in_slot, row_start, 0), axis=-1) + rank).reshape(-1)
    n_items = (t_tokens * TOP_K) // rows + N_EXPERTS
    n_used = item_end[-1].astype(I32).reshape(1)
    item_id = jnp.arange(n_items, dtype=I32)
    item_slot = jnp.minimum(jnp.sum(item_id[:, None] >= item_end[None, :], axis=1), N_EXPERTS - 1)
    item_expert = _expert_of_slot(item_slot).astype(I32)
    item_rows = jnp.clip(row_start[item_slot] + cnt[item_slot] - item_id * rows, 0, rows)
    item_rows = jnp.where(item_id < n_used[0], item_rows, 0).astype(I32)
    row_slot = jnp.zeros((n_items * rows,), I32).at[dest].set(jnp.arange(TOP_K * t_tokens, dtype=I32))

    yk = _expert_ffn(xp, item_expert, item_rows, n_used, row_slot, w_gate, w_up, w_down, layer)
    return _combine_layer_norm(x, yk, gate, ln_g, ln_b)


def kernel(x, rel_bias, even_w_in, even_w_out, diff_lambda, diff_subln_g, odd_w_in, odd_w_out, ssd_conv_w, ssd_conv_b, ssd_dt_bias, ssd_a_log, ssd_d, ssd_norm_g, gmlp_ln_g, gmlp_ln_b, gmlp_ws, gmlp_bs, router_w, router_bias, moe_w_gate, moe_w_up, moe_w_down, ln_mix_g, ln_mix_b, ln_ffn_g, ln_ffn_b):
    bsz, s, d = x.shape
    t_tokens = bsz * s
    xf = x.reshape(t_tokens, d)
    xb = xf.astype(BF16)
    moe_args = (router_w, router_bias, moe_w_gate, moe_w_up, moe_w_down)

    lambda_init = 0.8 - 0.6 * math.exp(-0.3 * 0)
    h = _project([xb], even_w_in[0], [0], 0, EVEN_IN, BF16, tm=1024, tn=768)
    attn = _even_attention(h.reshape(bsz, s, EVEN_IN), _bias_tiles(rel_bias), diff_lambda[0],
                           diff_subln_g[0], lambda_init)
    mix = _project([attn.reshape(t_tokens, d)], even_w_out[0], [0], 0, d, F32, tm=1024, tn=512)
    xf, xp = _residual_layer_norm(xf, mix, ln_mix_g[0], ln_mix_b[0])
    xf, xb = _moe_layer_norm(xf, xp, 0, *moe_args, ln_ffn_g[0], ln_ffn_b[0])

    zx_cols = SSD_D_INNER + SSD_CONV_DIM
    w_in_t = jnp.swapaxes(odd_w_in[0], 0, 1)
    h_zx = _project([xb], w_in_t, [0], 0, zx_cols, F32, tm=1024, tn=512, w_transposed=True)
    h_dt = _project([xb], w_in_t, [0], zx_cols, LANES, F32, tm=1024, tn=LANES, w_transposed=True)
    h_uv = _project([xb], w_in_t, [0], C_IN, 2 * GMLP_WIDTH, F32, tm=1024, tn=512, w_transposed=True)
    y_ssd = _ssd(h_zx, h_dt, bsz, ssd_conv_w[0], ssd_conv_b[0], ssd_dt_bias[0], ssd_a_log[0], ssd_d[0],
                 ssd_norm_g[0])
    y_gmlp = _gmlp(h_uv, gmlp_ln_g[0], gmlp_ln_b[0], gmlp_ws[0], gmlp_bs[0])
    mix = _project([y_ssd, y_gmlp], odd_w_out[0], [0, SSD_D_INNER // GMLP_WIDTH], 0, d, F32, tm=512, tn=512)
    xf, xp = _residual_layer_norm(xf, mix, ln_mix_g[1], ln_mix_b[1])
    xf, xb = _moe_layer_norm(xf, xp, 1, *moe_args, ln_ffn_g[1], ln_ffn_b[1])
    return xf.reshape(bsz, s, d)
```

```python
import functools
import math

import numpy as np
import jax
import jax.numpy as jnp
from jax import lax
from jax.experimental import pallas as pl
from jax.experimental.pallas import tpu as pltpu

F32 = jnp.float32
BF16 = jnp.bfloat16
I32 = jnp.int32

D_MODEL = 4096
DEPTH = 2
HEAD_DIM = 128
N_HEADS_A = 16
N_HEADS_B = 16
DIFF_HALF = HEAD_DIM // 2
N_BUCKETS = 32
MAX_DISTANCE = 128
A_IN = 3 * N_HEADS_A * HEAD_DIM
EVEN_IN = A_IN + 3 * N_HEADS_B * HEAD_DIM
DIFF_SCALE = DIFF_HALF ** -0.5
SB_SCALE = HEAD_DIM ** -0.5

SSD_HEAD_DIM = 64
SSD_D_INNER = D_MODEL
SSD_N_HEADS = SSD_D_INNER // SSD_HEAD_DIM
SSD_N_GROUPS = 8
SSD_HEADS_PER_GROUP = SSD_N_HEADS // SSD_N_GROUPS
SSD_D_STATE = 128
SSD_CONV = 4
SSD_CHUNK = 128
SSD_GN = SSD_N_GROUPS * SSD_D_STATE
SSD_CONV_DIM = SSD_D_INNER + 2 * SSD_GN
SSD_GROUP_DIM = SSD_D_INNER // SSD_N_GROUPS
C_IN = SSD_D_INNER + SSD_CONV_DIM + SSD_N_HEADS
GMLP_WIDTH = D_MODEL // 2
GMLP_GROUPS = 8
GMLP_GROUP_DIM = GMLP_WIDTH // GMLP_GROUPS
GMLP_CHUNK = 128

N_EXPERTS = 32
N_EXPERT_GROUPS = 8
EXPERTS_PER_GROUP = N_EXPERTS // N_EXPERT_GROUPS
TOP_K = 2
D_FF = 768

ALPHA = (2 * DEPTH) ** 0.25
LN_EPS = 1e-5

LANES = 128
SUBLANES = 8
VMEM_LIMIT_BYTES = 58 * 1024 * 1024

ATTN_TILE = 256
ATTN_HEADS_PER_STEP = 8
ATTN_ONES_ROWS = 16
MOE_SUB_ROWS = 272
MOE_ROWS = 3 * MOE_SUB_ROWS
MOE_K_STEPS = 4
MOE_K_CHUNK = D_MODEL // MOE_K_STEPS
MOE_O_STEPS = 2
MOE_DMA_UNROLL = 8
U32 = jnp.uint32
HIGH_HALF = np.uint32(0xFFFF0000)
ROUTER_ROWS = 512
LN_ROWS = 256


def _params(*sem):
    return pltpu.CompilerParams(dimension_semantics=sem, vmem_limit_bytes=VMEM_LIMIT_BYTES)


def _proj_kernel(*refs, n_in, w_transposed):
    x_refs = refs[:n_in]
    w_refs = refs[n_in:2 * n_in]
    o_ref = refs[2 * n_in]
    wbf_refs = refs[2 * n_in + 1:]

    @pl.when(pl.program_id(1) == 0)
    def _():
        for w_ref, wbf_ref in zip(w_refs, wbf_refs):
            wbf_ref[...] = w_ref[...].astype(BF16)

    mm = _dot_nt if w_transposed else functools.partial(jnp.dot, preferred_element_type=F32)
    acc = mm(x_refs[0][...], wbf_refs[0][...])
    for x_ref, wbf_ref in zip(x_refs[1:], wbf_refs[1:]):
        acc = acc + mm(x_ref[...], wbf_ref[...])
    o_ref[...] = acc.astype(o_ref.dtype)


def _project(xs, w, row_blocks, col0, n_cols, out_dtype, tm, tn, w_transposed=False):
    m = xs[0].shape[0]
    n_in = len(xs)
    in_specs = [pl.BlockSpec((tm, x.shape[1]), lambda j, i: (i, 0)) for x in xs]
    for x, rb in zip(xs, row_blocks):
        k = x.shape[1]
        if not w_transposed:
            in_specs.append(pl.BlockSpec((k, tn), lambda j, i, rb=rb: (rb, j + col0 // tn)))
        elif col0 % tn == 0:
            in_specs.append(pl.BlockSpec((tn, k), lambda j, i, rb=rb: (j + col0 // tn, rb)))
        else:
            in_specs.append(pl.BlockSpec((pl.Element(tn), pl.Element(k)),
                                         lambda j, i, rb=rb, k=k: (pl.multiple_of(col0 + j * tn, SUBLANES),
                                                                   rb * k)))
    return pl.pallas_call(
        functools.partial(_proj_kernel, n_in=n_in, w_transposed=w_transposed),
        grid=(n_cols // tn, m // tm),
        in_specs=in_specs,
        out_specs=pl.BlockSpec((tm, tn), lambda j, i: (i, j)),
        out_shape=jax.ShapeDtypeStruct((m, n_cols), out_dtype),
        scratch_shapes=[pltpu.VMEM((tn, x.shape[1]) if w_transposed else (x.shape[1], tn), BF16) for x in xs],
        compiler_params=_params("arbitrary", "arbitrary"),
        name="proj",
    )(*xs, *([w] * n_in))


def _layer_norm_rows(y, g, b):
    mu = jnp.mean(y, axis=-1, keepdims=True)
    yc = y - mu
    var = jnp.mean(yc * yc, axis=-1, keepdims=True)
    return yc * lax.rsqrt(var + LN_EPS) * g + b


def _pack_bf16_pair(lo, hi):
    lo_bits = lax.bitcast_convert_type(lo.astype(BF16).astype(F32), U32)
    hi_bits = lax.bitcast_convert_type(hi.astype(BF16).astype(F32), U32)
    return (lo_bits >> 16) | (hi_bits & HIGH_HALF)


def _unpack_bf16_pair(p):
    return (lax.bitcast_convert_type(p << 16, F32), lax.bitcast_convert_type(p & HIGH_HALF, F32))


def _pack_rows(y):
    half = MOE_K_CHUNK // 2
    parts = [_pack_bf16_pair(y[:, c * MOE_K_CHUNK:c * MOE_K_CHUNK + half],
                             y[:, c * MOE_K_CHUNK + half:(c + 1) * MOE_K_CHUNK])
             for c in range(y.shape[1] // MOE_K_CHUNK)]
    return jnp.concatenate(parts, axis=1)


def _res_ln_kernel(x_ref, mix_ref, g_ref, b_ref, rw_ref, o_ref, opk_ref, lg_ref):
    y = _layer_norm_rows(ALPHA * x_ref[...] + mix_ref[...], g_ref[...], b_ref[...])
    o_ref[...] = y
    opk_ref[...] = _pack_rows(y)
    y_hi, y_lo = _split_bf16(y)
    w_hi, w_lo = _split_bf16(rw_ref[...])
    lg_ref[...] = (jnp.dot(y_hi, w_hi, preferred_element_type=F32) + jnp.dot(y_hi, w_lo, preferred_element_type=F32)
                   + jnp.dot(y_lo, w_hi, preferred_element_type=F32))


def _residual_layer_norm(x, mix, g, b, router_w):
    m, d = x.shape
    tm = LN_ROWS
    rw = jnp.pad(router_w[:, _expert_of_slot(jnp.arange(N_EXPERTS))], ((0, 0), (0, LANES - N_EXPERTS)))
    row = pl.BlockSpec((tm, d), lambda i: (i, 0))
    half_row = pl.BlockSpec((tm, d // 2), lambda i: (i, 0))
    vec = pl.BlockSpec((1, d), lambda i: (0, 0))
    return pl.pallas_call(
        _res_ln_kernel,
        grid=(m // tm,),
        in_specs=[row, row, vec, vec, pl.BlockSpec((d, LANES), lambda i: (0, 0))],
        out_specs=[row, half_row, pl.BlockSpec((tm, LANES), lambda i: (i, 0))],
        out_shape=[jax.ShapeDtypeStruct((m, d), F32), jax.ShapeDtypeStruct((m, d // 2), U32),
                   jax.ShapeDtypeStruct((m, LANES), F32)],
        compiler_params=_params("arbitrary"),
        name="res_ln",
    )(x, mix, g.reshape(1, d), b.reshape(1, d), rw)


def _bias_tile_kernel(rb_ref, o_ref):
    head = pl.program_id(0)
    t = ATTN_TILE
    key = lax.broadcasted_iota(I32, (t, t), 0)
    qry = lax.broadcasted_iota(I32, (t, t), 1)
    max_exact = N_BUCKETS // 2
    for d in range(2):
        n = jnp.maximum(d * t + qry - key, 0)
        nf = jnp.maximum(n, 1).astype(F32)
        large = max_exact + (jnp.log(nf / max_exact) / math.log(MAX_DISTANCE / max_exact)
                             * (N_BUCKETS - max_exact)).astype(I32)
        large = jnp.minimum(large, N_BUCKETS - 1)
        bucket = jnp.where(n < max_exact, n, large)
        acc = jnp.zeros((t, t), F32)
        for bkt in range(N_BUCKETS):
            acc = jnp.where(bucket == bkt, rb_ref[bkt, head], acc)
        o_ref[d] = acc


def _bias_tiles(rel_bias):
    t = ATTN_TILE
    return pl.pallas_call(
        _bias_tile_kernel,
        grid=(N_HEADS_A,),
        in_specs=[pl.BlockSpec(memory_space=pltpu.SMEM)],
        out_specs=pl.BlockSpec((None, 2, t, t), lambda h: (h, 0, 0, 0)),
        out_shape=jax.ShapeDtypeStruct((N_HEADS_A, 2, t, t), F32),
        compiler_params=_params("arbitrary"),
        name="bias_tiles",
    )(rel_bias)


def _dot_nt(a, b):
    return lax.dot_general(a, b, (((1,), (1,)), ((), ())), preferred_element_type=F32)


def _head_cols(h):
    return slice(h * HEAD_DIM, (h + 1) * HEAD_DIM)


def _value_transposes(v_ref, vt_sc):
    t = ATTN_TILE
    ones = jnp.ones((ATTN_ONES_ROWS, t), BF16)
    for h in range(ATTN_HEADS_PER_STEP):
        for j in range(v_ref.shape[0] // t):
            vb = v_ref[j * t:(j + 1) * t, _head_cols(h)]
            vt_sc[h, j, 0:HEAD_DIM, :] = vb.astype(F32).T.astype(BF16)
            vt_sc[h, j, HEAD_DIM:, :] = ones


def _diff_attention(dl_ref, g_ref, q_ref, k_ref, vt_sc, bias_ref, o_ref, m_sc, acc_sc, lambda_init):
    t = ATTN_TILE
    qi = pl.program_id(2)
    lane = lax.broadcasted_iota(I32, (t, HEAD_DIM), 1)
    q_maps = []
    for h in range(ATTN_HEADS_PER_STEP):
        q = q_ref[:, _head_cols(h)] * DIFF_SCALE
        zero = jnp.zeros_like(q)
        q_maps.append((jnp.where(lane < DIFF_HALF, q, zero), jnp.where(lane >= DIFF_HALF, q, zero)))

    chains = [(h, mp) for h in range(ATTN_HEADS_PER_STEP) for mp in range(2)]

    def block(j, bias_of_head, mask, first):
        kv_start = pl.multiple_of(j * t, t)
        kbs = [k_ref[pl.ds(kv_start, t), _head_cols(h)] for h in range(ATTN_HEADS_PER_STEP)]
        scores = [_dot_nt(kbs[h], q_maps[h][mp]) for h, mp in chains]
        probs, alphas = [], []
        for (h, mp), s in zip(chains, scores):
            bias = bias_of_head(h)
            uniform = bias.shape == (1, 1)
            if not uniform:
                s = s + bias
            if mask is not None:
                s = jnp.where(mask, s, -jnp.inf)
            m_new = jnp.max(s, axis=0, keepdims=True)
            if uniform:
                m_new = m_new + bias
            if not first:
                m_old = m_sc[h, mp]
                m_new = jnp.maximum(m_old, m_new)
                alphas.append(jnp.exp(m_old - m_new))
            m_sc[h, mp] = m_new
            probs.append(jnp.exp(s - (m_new - bias if uniform else m_new)).astype(BF16))
        for i, (h, mp) in enumerate(chains):
            pv = jnp.dot(vt_sc[h, j], probs[i], preferred_element_type=F32)
            acc_sc[h, mp] = pv if first else alphas[i] * acc_sc[h, mp] + pv

    key = lax.broadcasted_iota(I32, (t, t), 0)
    qry = lax.broadcasted_iota(I32, (t, t), 1)
    block(qi, lambda h: bias_ref[h, 0], qry >= key, True)

    @pl.when(qi >= 1)
    def _():
        block(qi - 1, lambda h: bias_ref[h, 1], None, False)

    def far_body(j, carry):
        block(j, lambda h: bias_ref[h, 1, 0:1, t - 1:t], None, False)
        return carry

    lax.fori_loop(0, qi - 1, far_body, 0)

    dl = dl_ref[...]
    lam = (jnp.exp(jnp.sum(dl[0:1] * dl[1:2], axis=-1, keepdims=True))
           - jnp.exp(jnp.sum(dl[2:3] * dl[3:4], axis=-1, keepdims=True)) + lambda_init)
    for h in range(ATTN_HEADS_PER_STEP):
        a0 = acc_sc[h, 0]
        a1 = acc_sc[h, 1]
        oa = (a0[0:HEAD_DIM] / a0[HEAD_DIM:HEAD_DIM + 1]
              - lam * (a1[0:HEAD_DIM] / a1[HEAD_DIM:HEAD_DIM + 1]))
        ms = jnp.mean(oa * oa, axis=0, keepdims=True)
        oa = (oa * lax.rsqrt(ms + LN_EPS)).T * g_ref[...] * (1.0 - lambda_init)
        o_ref[:, _head_cols(h)] = oa.astype(o_ref.dtype)


def _log_sigmoid(z):
    return jnp.minimum(z, 0.0) - jnp.log(1.0 + jnp.exp(-jnp.abs(z)))


def _split_bf16(x):
    hi = x.astype(BF16)
    lo = (x - hi.astype(F32)).astype(BF16)
    return hi, lo


def _stick_breaking_attention(q_ref, k_ref, vt_sc, o_ref, c_sc, acc_sc):
    t = ATTN_TILE
    qi = pl.program_id(2)
    key = lax.broadcasted_iota(I32, (t, t), 0)
    qry = lax.broadcasted_iota(I32, (t, t), 1)
    later = qry > key
    after = later.astype(BF16)

    heads = range(ATTN_HEADS_PER_STEP)

    def block(j, strict, first):
        kv_start = pl.multiple_of(j * t, t)
        zs = [_dot_nt(k_ref[pl.ds(kv_start, t), _head_cols(h)], q_ref[:, _head_cols(h)]) for h in heads]
        log_betas, splits, col_sums = [], [], []
        for h in heads:
            z = zs[h] * SB_SCALE
            log_beta = _log_sigmoid(z)
            log_1mb = log_beta - z
            if strict is not None:
                log_1mb = jnp.where(strict, log_1mb, 0.0)
            log_betas.append(log_beta)
            splits.append(_split_bf16(log_1mb))
            col_sums.append(jnp.sum(log_1mb, axis=0, keepdims=True))
        tails = [jnp.dot(after, hi, preferred_element_type=F32) + jnp.dot(after, lo, preferred_element_type=F32)
                 for hi, lo in splits]
        weights = []
        for h in heads:
            tail = tails[h] if first else tails[h] + c_sc[h]
            w = jnp.exp(log_betas[h] + tail)
            if strict is not None:
                w = jnp.where(strict, w, 0.0)
            weights.append(w.astype(BF16))
            c_sc[h] = col_sums[h] if first else c_sc[h] + col_sums[h]
        for h in heads:
            pv = jnp.dot(vt_sc[h, j][0:HEAD_DIM], weights[h], preferred_element_type=F32)
            acc_sc[h, 0, 0:HEAD_DIM] = pv if first else acc_sc[h, 0, 0:HEAD_DIM] + pv

    block(qi, later, True)

    def body(step, carry):
        block(qi - 1 - step, None, False)
        return carry

    lax.fori_loop(0, qi, body, 0)
    for h in range(ATTN_HEADS_PER_STEP):
        o_ref[:, _head_cols(h)] = acc_sc[h, 0, 0:HEAD_DIM].T.astype(o_ref.dtype)


def _attn_kernel(dl_ref, g_ref, q_ref, k_ref, v_ref, bias_ref, o_ref, vt_sc, m_sc, c_sc, acc_sc, *, lambda_init):
    grp = pl.program_id(1)

    @pl.when(pl.program_id(2) == 0)
    def _():
        _value_transposes(v_ref, vt_sc)

    @pl.when(grp < N_HEADS_A // ATTN_HEADS_PER_STEP)
    def _():
        _diff_attention(dl_ref, g_ref, q_ref, k_ref, vt_sc, bias_ref, o_ref, m_sc, acc_sc, lambda_init)

    @pl.when(grp >= N_HEADS_A // ATTN_HEADS_PER_STEP)
    def _():
        _stick_breaking_attention(q_ref, k_ref, vt_sc, o_ref, c_sc, acc_sc)


def _even_attention(h, bias_tiles, diff_lambda, subln_g, lambda_init):
    bsz, s, _ = h.shape
    t = ATTN_TILE
    nh = ATTN_HEADS_PER_STEP
    width = nh * HEAD_DIM
    groups_a = N_HEADS_A // nh
    n_groups = (N_HEADS_A + N_HEADS_B) // nh

    def q_col(g):
        return g + jnp.where(g >= groups_a, 2 * groups_a, 0)

    return pl.pallas_call(
        functools.partial(_attn_kernel, lambda_init=lambda_init),
        grid=(bsz, n_groups, s // t),
        in_specs=[
            pl.BlockSpec((4, DIFF_HALF), lambda b, g, qi: (0, 0)),
            pl.BlockSpec((1, HEAD_DIM), lambda b, g, qi: (0, 0)),
            pl.BlockSpec((None, t, width), lambda b, g, qi: (b, qi, q_col(g))),
            pl.BlockSpec((None, s, width), lambda b, g, qi: (b, 0, q_col(g) + groups_a)),
            pl.BlockSpec((None, s, width), lambda b, g, qi: (b, 0, q_col(g) + 2 * groups_a)),
            pl.BlockSpec((nh, 2, t, t), lambda b, g, qi: (jnp.minimum(g, groups_a - 1), 0, 0, 0)),
        ],
        out_specs=pl.BlockSpec((None, t, width), lambda b, g, qi: (b, qi, g)),
        out_shape=jax.ShapeDtypeStruct((bsz, s, n_groups * width), BF16),
        scratch_shapes=[
            pltpu.VMEM((nh, s // t, HEAD_DIM + ATTN_ONES_ROWS, t), BF16),
            pltpu.VMEM((nh, 2, 1, t), F32),
            pltpu.VMEM((nh, 1, t), F32),
            pltpu.VMEM((nh, 2, HEAD_DIM + ATTN_ONES_ROWS, t), F32),
        ],
        compiler_params=_params("arbitrary", "arbitrary", "arbitrary"),
        name="even_attention",
    )(diff_lambda, subln_g.reshape(1, HEAD_DIM), h, h, h, bias_tiles)


def _silu(x):
    return x * jax.nn.sigmoid(x)


def _softplus(x):
    return jnp.maximum(x, 0.0) + jnp.log(1.0 + jnp.exp(-jnp.abs(x)))


def _ssd_kernel(z_ref, x_ref, b_ref, c_ref, dt_ref, cw_ref, cb_ref, dtb_ref, alog_ref, dskip_ref, ng_ref,
                o_ref, pad_sc, state_sc, acst_sc, dtt_sc):
    pad_sc[0:SUBLANES, :] = jnp.zeros((SUBLANES, pad_sc.shape[1]), F32)
    state_sc[...] = jnp.zeros_like(state_sc)

    def chunk_body(chunk, carry):
        _ssd_chunk(chunk, z_ref, x_ref, b_ref, c_ref, dt_ref, cw_ref, cb_ref, dtb_ref, alog_ref, dskip_ref,
                   ng_ref, o_ref, pad_sc, state_sc, acst_sc, dtt_sc)
        return carry

    lax.fori_loop(0, x_ref.shape[0] // SSD_CHUNK, chunk_body, 0)


def _ssd_chunk(chunk, z_ref, x_ref, b_ref, c_ref, dt_ref, cw_ref, cb_ref, dtb_ref, alog_ref, dskip_ref, ng_ref,
               o_ref, pad_sc, state_sc, acst_sc, dtt_sc):
    grp = pl.program_id(1)
    L = SSD_CHUNK
    P2 = 2 * SSD_HEAD_DIM
    n_conv = SSD_GROUP_DIM + 2 * SSD_D_STATE
    halo = SUBLANES
    rows = pl.ds(pl.multiple_of(chunk * L, L), L)

    pad_sc[halo:halo + L, 0:SSD_GROUP_DIM] = x_ref[rows, :]
    pad_sc[halo:halo + L, SSD_GROUP_DIM:SSD_GROUP_DIM + SSD_D_STATE] = b_ref[rows, :]
    pad_sc[halo:halo + L, SSD_GROUP_DIM + SSD_D_STATE:n_conv] = c_ref[rows, :]
    conv = cb_ref[...]
    for j in range(SSD_CONV):
        start = halo - (SSD_CONV - 1) + j
        conv = conv + cw_ref[j:j + 1, :] * pad_sc[start:start + L, :]
    pad_sc[0:halo, :] = pad_sc[L:L + halo, :]
    xbc = _silu(conv)
    xg = xbc[:, 0:SSD_GROUP_DIM]
    bg = xbc[:, SSD_GROUP_DIM:SSD_GROUP_DIM + SSD_D_STATE]
    cg = xbc[:, SSD_GROUP_DIM + SSD_D_STATE:n_conv]
    xg_bf = xg.astype(BF16)

    dt = _softplus(dt_ref[rows, :] + dtb_ref[...])
    adt = dt * (-jnp.exp(alog_ref[...]))
    row = lax.broadcasted_iota(I32, (L, L), 0)
    col = lax.broadcasted_iota(I32, (L, L), 1)
    causal = row >= col
    acs = jnp.dot(causal.astype(F32), adt, preferred_element_type=F32, precision=lax.Precision.HIGHEST)
    acst_sc[...] = acs.T
    dtt_sc[...] = dt.T

    cb = _dot_nt(cg.astype(BF16), bg.astype(BF16))
    bgt = bg.T
    cg_bf = cg.astype(BF16)
    prev = state_sc[...]
    y_off = jnp.dot(cg_bf, prev.astype(BF16), preferred_element_type=F32)

    lane = lax.broadcasted_iota(I32, (L, P2), 1)
    first_head = lane < SSD_HEAD_DIM
    lane_row = lax.broadcasted_iota(I32, (1, P2), 1) < SSD_HEAD_DIM
    y_pairs = []
    for pair in range(SSD_HEADS_PER_GROUP // 2):
        x_pair = xg_bf[:, pair * P2:(pair + 1) * P2]
        y_diag, st, e_acs, decay = [], [], [], []
        for sub in range(2):
            head = grp * SSD_HEADS_PER_GROUP + 2 * pair + sub
            acs_row = acst_sc[pl.ds(head, 1), :]
            dt_row = dtt_sc[pl.ds(head, 1), :]
            row_b = jnp.broadcast_to(acs_row, (L, L))
            col_b = row_b.T
            decay_m = jnp.exp(jnp.where(causal, col_b - row_b, -jnp.inf))
            m = (cb * decay_m * dt_row).astype(BF16)
            y_diag.append(jnp.dot(m, x_pair, preferred_element_type=F32))
            acs_last = col_b[L - 1:L, :]
            w_row = jnp.exp(acs_last - acs_row) * dt_row
            st.append(jnp.dot((bgt * w_row).astype(BF16), x_pair, preferred_element_type=F32))
            e_acs.append(jnp.exp(col_b))
            decay.append(jnp.exp(acs_last))
        cols = slice(pair * P2, (pair + 1) * P2)
        y_pair = (jnp.where(first_head, y_diag[0], y_diag[1])
                  + jnp.where(first_head, e_acs[0], e_acs[1]) * y_off[:, cols])
        state_sc[:, cols] = (prev[:, cols] * jnp.where(lane_row, decay[0], decay[1])
                             + jnp.where(first_head, st[0], st[1]))
        y_pair = y_pair + dskip_ref[:, cols] * xg[:, cols]
        y_pairs.append(y_pair * _silu(z_ref[rows, cols]))

    y = jnp.concatenate(y_pairs, axis=-1)
    ms = jnp.mean(y * y, axis=-1, keepdims=True)
    o_ref[rows, :] = (y * lax.rsqrt(ms + LN_EPS) * ng_ref[...]).astype(o_ref.dtype)


def _ssd(h_zx, h_dt, bsz, conv_w, conv_b, dt_bias, a_log, d_skip, norm_g):
    t_tokens = h_zx.shape[0]
    n_chunks = t_tokens // bsz // SSD_CHUNK
    L = SSD_CHUNK
    gd, ns, G = SSD_GROUP_DIM, SSD_D_STATE, SSD_N_GROUPS
    n_conv = gd + 2 * ns

    def pack(p):
        xs = p[..., :SSD_D_INNER].reshape(p.shape[:-1] + (G, gd))
        bs_ = p[..., SSD_D_INNER:SSD_D_INNER + SSD_GN].reshape(p.shape[:-1] + (G, ns))
        cs = p[..., SSD_D_INNER + SSD_GN:].reshape(p.shape[:-1] + (G, ns))
        return jnp.moveaxis(jnp.concatenate([xs, bs_, cs], axis=-1), -2, 0)

    cw = pack(conv_w)
    cb = pack(conv_b.reshape(1, -1))
    pad_heads = LANES - SSD_N_HEADS
    dtb = jnp.pad(dt_bias, (0, pad_heads)).reshape(1, LANES)
    alog = jnp.pad(a_log, (0, pad_heads)).reshape(1, LANES)
    dskip = jnp.repeat(d_skip, SSD_HEAD_DIM).reshape(1, SSD_D_INNER)

    seq = n_chunks * L
    x_blk0 = SSD_D_INNER // gd
    b_blk0 = (2 * SSD_D_INNER) // ns
    c_blk0 = (2 * SSD_D_INNER + SSD_GN) // ns
    return pl.pallas_call(
        _ssd_kernel,
        grid=(bsz, G),
        in_specs=[
            pl.BlockSpec((seq, gd), lambda b, g: (b, g)),
            pl.BlockSpec((seq, gd), lambda b, g: (b, x_blk0 + g)),
            pl.BlockSpec((seq, ns), lambda b, g: (b, b_blk0 + g)),
            pl.BlockSpec((seq, ns), lambda b, g: (b, c_blk0 + g)),
            pl.BlockSpec((seq, LANES), lambda b, g: (b, 0)),
            pl.BlockSpec((None, SSD_CONV, n_conv), lambda b, g: (g, 0, 0)),
            pl.BlockSpec((None, 1, n_conv), lambda b, g: (g, 0, 0)),
            pl.BlockSpec((1, LANES), lambda b, g: (0, 0)),
            pl.BlockSpec((1, LANES), lambda b, g: (0, 0)),
            pl.BlockSpec((1, gd), lambda b, g: (0, g)),
            pl.BlockSpec((1, gd), lambda b, g: (0, g)),
        ],
        out_specs=pl.BlockSpec((seq, gd), lambda b, g: (b, g)),
        out_shape=jax.ShapeDtypeStruct((t_tokens, SSD_D_INNER), BF16),
        scratch_shapes=[
            pltpu.VMEM((SUBLANES + L, n_conv), F32),
            pltpu.VMEM((ns, gd), F32),
            pltpu.VMEM((LANES, L), F32),
            pltpu.VMEM((LANES, L), F32),
        ],
        compiler_params=_params("arbitrary", "arbitrary"),
        name="ssd",
    )(h_zx, h_zx, h_zx, h_zx, h_dt, cw, cb, dtb, alog, dskip, norm_g.reshape(1, SSD_D_INNER))


def _gelu_tanh(x):
    return 0.5 * x * (1.0 + jnp.tanh(math.sqrt(2.0 / math.pi) * (x + 0.044715 * (x * x * x))))


def _gmlp_kernel(uv_ref, lng_ref, lnb_ref, ws_ref, bs_ref, o_ref):
    L = GMLP_CHUNK
    gdim = GMLP_GROUP_DIM
    row = lax.broadcasted_iota(I32, (L, L), 0)
    col = lax.broadcasted_iota(I32, (L, L), 1)
    causal = row >= col
    for g in range(GMLP_GROUPS):
        cols = slice(g * gdim, (g + 1) * gdim)
        u = _gelu_tanh(uv_ref[:, cols])
        v = _gelu_tanh(uv_ref[:, GMLP_WIDTH + g * gdim:GMLP_WIDTH + (g + 1) * gdim])
        v = _layer_norm_rows(v, lng_ref[:, cols], lnb_ref[:, cols])
        ws = jnp.where(causal, ws_ref[g], 0.0).astype(BF16)
        sv = jnp.dot(ws, v.astype(BF16), preferred_element_type=F32) + bs_ref[:, cols]
        o_ref[:, cols] = (u * sv).astype(o_ref.dtype)


def _gmlp(h_uv, ln_g, ln_b, ws, bs):
    t_tokens = h_uv.shape[0]
    L = GMLP_CHUNK
    bs_cols = jnp.repeat(bs.T, GMLP_GROUP_DIM, axis=1)
    return pl.pallas_call(
        _gmlp_kernel,
        grid=(t_tokens // L,),
        in_specs=[
            pl.BlockSpec((L, 2 * GMLP_WIDTH), lambda i: (i, 0)),
            pl.BlockSpec((1, GMLP_WIDTH), lambda i: (0, 0)),
            pl.BlockSpec((1, GMLP_WIDTH), lambda i: (0, 0)),
            pl.BlockSpec((GMLP_GROUPS, L, L), lambda i: (0, 0, 0)),
            pl.BlockSpec((L, GMLP_WIDTH), lambda i: (0, 0)),
        ],
        out_specs=pl.BlockSpec((L, GMLP_WIDTH), lambda i: (i, 0)),
        out_shape=jax.ShapeDtypeStruct((t_tokens, GMLP_WIDTH), BF16),
        compiler_params=_params("arbitrary"),
        name="gmlp",
    )(h_uv, ln_g.reshape(1, GMLP_WIDTH), ln_b.reshape(1, GMLP_WIDTH), ws, bs_cols)


def _router_kernel(lg_ref, rb_ref, eidx_ref, gate_ref, rank_ref, cnt_ref, carry_sc):
    tm = lg_ref.shape[0]
    G, K = N_EXPERT_GROUPS, EXPERTS_PER_GROUP

    @pl.when(pl.program_id(0) == 0)
    def _():
        carry_sc[...] = jnp.zeros_like(carry_sc)

    logits = lg_ref[...].T[0:N_EXPERTS]
    scores = jax.nn.sigmoid(logits)
    biased = scores + rb_ref[...]
    a = [biased[k * G:(k + 1) * G] for k in range(K)]
    sc = [scores[k * G:(k + 1) * G] for k in range(K)]
    hi01, lo01 = jnp.maximum(a[0], a[1]), jnp.minimum(a[0], a[1])
    hi23, lo23 = jnp.maximum(a[2], a[3]), jnp.minimum(a[2], a[3])
    grp_score = jnp.maximum(hi01, hi23) + jnp.maximum(jnp.minimum(hi01, hi23), jnp.maximum(lo01, lo23))
    gidx = lax.broadcasted_iota(I32, (G, tm), 0)
    g_best = jnp.max(grp_score, axis=0, keepdims=True)
    g_sel = jnp.min(jnp.where(grp_score == g_best, gidx, G), axis=0, keepdims=True)
    in_grp = gidx == g_sel
    cand = [jnp.sum(jnp.where(in_grp, a[k], 0.0), axis=0, keepdims=True) for k in range(K)]
    cand_sc = [jnp.sum(jnp.where(in_grp, sc[k], 0.0), axis=0, keepdims=True) for k in range(K)]

    def first_argmax(vals):
        best, idx = vals[0], jnp.zeros((1, tm), I32)
        for k in range(1, K):
            gt = vals[k] > best
            best = jnp.where(gt, vals[k], best)
            idx = jnp.where(gt, k, idx)
        return idx

    i1 = first_argmax(cand)
    i2 = first_argmax([jnp.where(i1 == k, -jnp.inf, cand[k]) for k in range(K)])
    v1 = sum(jnp.where(i1 == k, cand_sc[k], 0.0) for k in range(K))
    v2 = sum(jnp.where(i2 == k, cand_sc[k], 0.0) for k in range(K))
    den = v1 + v2
    eidx_ref[0:1, :] = g_sel * K + i1
    eidx_ref[1:2, :] = g_sel * K + i2
    gate_ref[0:1, :] = v1 / den
    gate_ref[1:2, :] = v2 / den

    member = jnp.concatenate(
        [jnp.where(in_grp & ((i1 == k) | (i2 == k)), 1.0, 0.0) for k in range(K)], axis=0)
    trow = lax.broadcasted_iota(I32, (tm, tm), 0)
    tcol = lax.broadcasted_iota(I32, (tm, tm), 1)
    before = (trow < tcol).astype(BF16)
    prefix = jnp.dot(member.astype(BF16), before, preferred_element_type=F32) + carry_sc[...]
    r1 = sum(jnp.sum(jnp.where(in_grp & (i1 == k), prefix[k * G:(k + 1) * G], 0.0), axis=0, keepdims=True)
             for k in range(K))
    r2 = sum(jnp.sum(jnp.where(in_grp & (i2 == k), prefix[k * G:(k + 1) * G], 0.0), axis=0, keepdims=True)
             for k in range(K))
    rank_ref[0:1, :] = r1.astype(I32)
    rank_ref[1:2, :] = r2.astype(I32)
    carry_sc[...] = carry_sc[...] + jnp.sum(member, axis=1, keepdims=True)
    cnt_ref[...] = jnp.broadcast_to(carry_sc[...], cnt_ref.shape).astype(I32)


def _slot_of_expert(e):
    return (e % EXPERTS_PER_GROUP) * N_EXPERT_GROUPS + e // EXPERTS_PER_GROUP


def _expert_of_slot(r):
    return (r % N_EXPERT_GROUPS) * EXPERTS_PER_GROUP + r // N_EXPERT_GROUPS


def _route(logits, router_bias):
    t_tokens = logits.shape[0]
    tm = ROUTER_ROWS
    rb = router_bias[_expert_of_slot(jnp.arange(N_EXPERTS))].reshape(N_EXPERTS, 1)
    tok = pl.BlockSpec((TOP_K, tm), lambda i: (0, i))
    return pl.pallas_call(
        _router_kernel,
        grid=(t_tokens // tm,),
        in_specs=[
            pl.BlockSpec((tm, LANES), lambda i: (i, 0)),
            pl.BlockSpec((N_EXPERTS, 1), lambda i: (0, 0)),
        ],
        out_specs=[tok, tok, tok, pl.BlockSpec((N_EXPERTS, LANES), lambda i: (0, 0))],
        out_shape=[
            jax.ShapeDtypeStruct((TOP_K, t_tokens), I32),
            jax.ShapeDtypeStruct((TOP_K, t_tokens), F32),
            jax.ShapeDtypeStruct((TOP_K, t_tokens), I32),
            jax.ShapeDtypeStruct((N_EXPERTS, LANES), I32),
        ],
        scratch_shapes=[pltpu.VMEM((N_EXPERTS, 1), F32)],
        compiler_params=_params("arbitrary"),
        name="router",
    )(logits, rb)


def _ffn_kernel(nused_ref, iexp_ref, nval_ref, src_ref, src_next_ref, xp_hbm, wg_ref, wu_ref, wd_ref,
                yk_hbm, xbuf, hg_sc, hu_sc, hid_sc, obuf, gsem, ssem):
    item = pl.program_id(0)
    step = pl.program_id(1)
    n_used = nused_ref[0]
    slot = lax.rem(item, 2)
    sub = MOE_SUB_ROWS
    half = MOE_K_CHUNK // 2

    n_tokens = xp_hbm.shape[0]
    unroll = MOE_DMA_UNROLL

    def for_rows(count, fn):
        def body(i, carry):
            fn(i)
            return carry
        lax.fori_loop(0, count, body, 0)

    def for_rows_unrolled(count, fn):
        def body(b, carry):
            base = pl.multiple_of(b * unroll, unroll)
            for u in range(unroll):
                fn(base + u)
            return carry
        n_full = count // unroll
        lax.fori_loop(0, n_full, body, 0)
        lax.fori_loop(n_full * unroll, count, lambda i, c: (fn(i), c)[1], 0)

    def gather_rows(slot_rows, count, to_slot):
        def start(i):
            slot_id = slot_rows[0, i]
            if n_tokens & (n_tokens - 1) == 0:
                tok = slot_id & (n_tokens - 1)
            else:
                tok = lax.rem(slot_id, n_tokens)
            pltpu.make_async_copy(xp_hbm.at[pl.ds(tok, 1), :], xbuf.at[to_slot, pl.ds(i, 1), :],
                                  gsem.at[to_slot]).start(priority=1)
        for_rows_unrolled(count, start)

    def scatter_rows(count):
        def start(i):
            pltpu.make_async_copy(obuf.at[pl.ds(i, 1), :], yk_hbm.at[pl.ds(src_ref[0, i], 1), :],
                                  ssem).start(priority=1)
        for_rows_unrolled(count, start)

    def wait_rows(count, n_rows_copy):
        n_full = count // unroll
        for_rows(n_full, lambda i: n_rows_copy(unroll).wait())
        for_rows(count - n_full * unroll, lambda i: n_rows_copy(1).wait())

    def gathered(to_slot):
        return lambda n: pltpu.make_async_copy(xp_hbm.at[pl.ds(0, n), :], xbuf.at[to_slot, pl.ds(0, n), :],
                                               gsem.at[to_slot])

    def scattered(n):
        return pltpu.make_async_copy(obuf.at[pl.ds(0, n), :], yk_hbm.at[pl.ds(0, n), :], ssem)

    @pl.when(item < n_used)
    def _():
        n_rows = nval_ref[item]
        n_sub = (n_rows + sub - 1) // sub

        @pl.when(step == 0)
        def _():
            @pl.when(item == 0)
            def _():
                xbuf[...] = jnp.zeros_like(xbuf)
                gather_rows(src_ref, n_rows, 0)

            @pl.when(item + 1 < n_used)
            def _():
                gather_rows(src_next_ref, nval_ref[item + 1], 1 - slot)

            wait_rows(n_rows, gathered(slot))

        for c in range(MOE_K_STEPS):
            @pl.when(step == c)
            def _(c=c):
                wg = wg_ref[...].astype(BF16)
                wu = wu_ref[...].astype(BF16)

                def sub_block(r):
                    rows = pl.ds(pl.multiple_of(r * sub, sub), sub)
                    lo, hi = _unpack_bf16_pair(xbuf[slot, rows, c * half:(c + 1) * half])
                    xk = jnp.concatenate([lo, hi], axis=1).astype(BF16)
                    g = jnp.dot(xk, wg, preferred_element_type=F32)
                    u = jnp.dot(xk, wu, preferred_element_type=F32)
                    if c > 0:
                        g = g + hg_sc[rows, :]
                        u = u + hu_sc[rows, :]
                    if c < MOE_K_STEPS - 1:
                        hg_sc[rows, :] = g
                        hu_sc[rows, :] = u
                    else:
                        hid_sc[rows, :] = (_silu(g) * u).astype(BF16)

                for_rows(n_sub, sub_block)

        @pl.when(step == MOE_K_STEPS)
        def _():
            @pl.when(item > 0)
            def _():
                wait_rows(nval_ref[item - 1], scattered)

            wd = wd_ref[...].astype(BF16)

            def sub_block(r):
                rows = pl.ds(pl.multiple_of(r * sub, sub), sub)
                o = jnp.dot(hid_sc[rows, :], wd, preferred_element_type=F32)
                obuf[rows, :] = lax.bitcast_convert_type(o.astype(BF16).astype(F32), U32) >> 16

            for_rows(n_sub, sub_block)

        @pl.when(step == MOE_K_STEPS + 1)
        def _():
            wd = wd_ref[...].astype(BF16)

            def sub_block(r):
                rows = pl.ds(pl.multiple_of(r * sub, sub), sub)
                o = jnp.dot(hid_sc[rows, :], wd, preferred_element_type=F32)
                obuf[rows, :] = obuf[rows, :] | (lax.bitcast_convert_type(o.astype(BF16).astype(F32), U32)
                                                 & HIGH_HALF)

            for_rows(n_sub, sub_block)
            scatter_rows(n_rows)

            @pl.when(item == n_used - 1)
            def _():
                wait_rows(n_rows, scattered)


def _expert_ffn(xp, item_expert, item_rows, n_used, row_slot, w_gate, w_up, w_down, layer):
    t_tokens, dp = xp.shape
    d = 2 * dp
    rows = MOE_ROWS
    n_items = item_expert.shape[0]
    oc = d // MOE_O_STEPS
    last = MOE_K_STEPS + MOE_O_STEPS - 1

    def pos(i, s, nu):
        used = i < nu[0]
        return jnp.where(used, i, nu[0] - 1), jnp.where(used, s, last)

    def w_in_map(i, s, nu, ie, nv):
        ii, ss = pos(i, s, nu)
        return layer, ie[ii], jnp.minimum(ss, MOE_K_STEPS - 1), 0

    def w_down_map(i, s, nu, ie, nv):
        ii, ss = pos(i, s, nu)
        early = ss < MOE_K_STEPS
        return (layer, ie[jnp.where(early, jnp.maximum(ii - 1, 0), ii)], 0,
                jnp.where(early, MOE_O_STEPS - 1, ss - MOE_K_STEPS))

    def rows_map(i, s, nu, ie, nv):
        return pos(i, s, nu)[0], 0, 0

    def next_rows_map(i, s, nu, ie, nv):
        return jnp.minimum(pos(i, s, nu)[0] + 1, n_items - 1), 0, 0

    idx_block = (None, 1, rows)
    return pl.pallas_call(
        _ffn_kernel,
        grid_spec=pltpu.PrefetchScalarGridSpec(
            num_scalar_prefetch=3,
            grid=(n_items, MOE_K_STEPS + MOE_O_STEPS),
            in_specs=[
                pl.BlockSpec(idx_block, rows_map, memory_space=pltpu.SMEM),
                pl.BlockSpec(idx_block, next_rows_map, memory_space=pltpu.SMEM),
                pl.BlockSpec(memory_space=pl.ANY),
                pl.BlockSpec((None, None, MOE_K_CHUNK, D_FF), w_in_map),
                pl.BlockSpec((None, None, MOE_K_CHUNK, D_FF), w_in_map),
                pl.BlockSpec((None, None, D_FF, oc), w_down_map),
            ],
            out_specs=pl.BlockSpec(memory_space=pl.ANY),
            scratch_shapes=[
                pltpu.VMEM((2, rows, dp), U32),
                pltpu.VMEM((rows, D_FF), F32),
                pltpu.VMEM((rows, D_FF), F32),
                pltpu.VMEM((rows, D_FF), BF16),
                pltpu.VMEM((rows, dp), U32),
                pltpu.SemaphoreType.DMA((2,)),
                pltpu.SemaphoreType.DMA(()),
            ],
        ),
        out_shape=jax.ShapeDtypeStruct((TOP_K * t_tokens, dp), U32),
        compiler_params=_params("arbitrary", "arbitrary"),
        name="moe_ffn",
    )(n_used, item_expert, item_rows, row_slot.reshape(n_items, 1, rows), row_slot.reshape(n_items, 1, rows),
      xp, w_gate, w_up, w_down)


def _combine_kernel(y0_ref, y1_ref, gate_ref, x_ref, g_ref, b_ref, o_ref, obf_ref):
    gates = gate_ref[...]
    ffn = (gates[:, 0:1] * jnp.concatenate(_unpack_bf16_pair(y0_ref[...]), axis=1)
           + gates[:, 1:2] * jnp.concatenate(_unpack_bf16_pair(y1_ref[...]), axis=1))
    y = _layer_norm_rows(ALPHA * x_ref[...] + ffn, g_ref[...], b_ref[...])
    o_ref[...] = y
    obf_ref[...] = y.astype(BF16)


def _combine_layer_norm(x, yk, gate, g, b):
    t_tokens, d = x.shape
    tm = LN_ROWS
    n_steps = t_tokens // tm
    row = pl.BlockSpec((tm, d), lambda i: (i, 0))
    vec = pl.BlockSpec((1, d), lambda i: (0, 0))
    return pl.pallas_call(
        _combine_kernel,
        grid=(n_steps,),
        in_specs=[
            pl.BlockSpec((tm, d // 2), lambda i: (i, 0)),
            pl.BlockSpec((tm, d // 2), lambda i: (i + n_steps, 0)),
            pl.BlockSpec((tm, TOP_K), lambda i: (i, 0)),
            row, vec, vec,
        ],
        out_specs=[row, row],
        out_shape=[jax.ShapeDtypeStruct((t_tokens, d), F32), jax.ShapeDtypeStruct((t_tokens, d), BF16)],
        compiler_params=_params("arbitrary"),
        name="moe_combine",
    )(yk, yk, gate.T, x, g.reshape(1, d), b.reshape(1, d))


def _moe_layer_norm(x, xp, logits, layer, router_bias, w_gate, w_up, w_down, ln_g, ln_b):
    t_tokens, _ = x.shape
    rows = MOE_ROWS
    eidx, gate, rank, counts = _route(logits, router_bias)
    cnt = counts[:, 0]
    items_per_slot = (cnt + rows - 1) // rows
    item_end = jnp.cumsum(items_per_slot)
    row_start = (item_end - items_per_slot) * rows
    in_slot = _slot_of_expert(eidx)[..., None] == jnp.arange(N_EXPERTS, dtype=I32)
    dest = (jnp.sum(jnp.where(in_slot, row_start, 0), axis=-1) + rank).reshape(-1)
    n_items = (t_tokens * TOP_K) // rows + N_EXPERTS
    n_used = item_end[-1].astype(I32).reshape(1)
    item_id = jnp.arange(n_items, dtype=I32)
    item_slot = jnp.minimum(jnp.sum(item_id[:, None] >= item_end[None, :], axis=1), N_EXPERTS - 1)
    item_expert = _expert_of_slot(item_slot).astype(I32)
    item_rows = jnp.clip(row_start[item_slot] + cnt[item_slot] - item_id * rows, 0, rows)
    item_rows = jnp.where(item_id < n_used[0], item_rows, 0).astype(I32)
    row_slot = jnp.zeros((n_items * rows,), I32).at[dest].set(jnp.arange(TOP_K * t_tokens, dtype=I32))

    yk = _expert_ffn(xp, item_expert, item_rows, n_used, row_slot, w_gate, w_up, w_down, layer)
    return _combine_layer_norm(x, yk, gate, ln_g, ln_b)


def kernel(x, rel_bias, even_w_in, even_w_out, diff_lambda, diff_subln_g, odd_w_in, odd_w_out, ssd_conv_w, ssd_conv_b, ssd_dt_bias, ssd_a_log, ssd_d, ssd_norm_g, gmlp_ln_g, gmlp_ln_b, gmlp_ws, gmlp_bs, router_w, router_bias, moe_w_gate, moe_w_up, moe_w_down, ln_mix_g, ln_mix_b, ln_ffn_g, ln_ffn_b):
    bsz, s, d = x.shape
    t_tokens = bsz * s
    xf = x.reshape(t_tokens, d)
    xb = xf.astype(BF16)
    moe_args = (router_bias, moe_w_gate, moe_w_up, moe_w_down)

    lambda_init = 0.8 - 0.6 * math.exp(-0.3 * 0)
    h = _project([xb], even_w_in[0], [0], 0, EVEN_IN, BF16, tm=1024, tn=768)
    attn = _even_attention(h.reshape(bsz, s, EVEN_IN), _bias_tiles(rel_bias), diff_lambda[0],
                           diff_subln_g[0], lambda_init)
    mix = _project([attn.reshape(t_tokens, d)], even_w_out[0], [0], 0, d, F32, tm=1024, tn=512)
    xf, xp, logits = _residual_layer_norm(xf, mix, ln_mix_g[0], ln_mix_b[0], router_w)
    xf, xb = _moe_layer_norm(xf, xp, logits, 0, *moe_args, ln_ffn_g[0], ln_ffn_b[0])

    zx_cols = SSD_D_INNER + SSD_CONV_DIM
    w_in_t = jnp.swapaxes(odd_w_in[0], 0, 1)
    h_zx = _project([xb], w_in_t, [0], 0, zx_cols, F32, tm=1024, tn=512, w_transposed=True)
    h_dt = _project([xb], w_in_t, [0], zx_cols, LANES, F32, tm=1024, tn=LANES, w_transposed=True)
    h_uv = _project([xb], w_in_t, [0], C_IN, 2 * GMLP_WIDTH, F32, tm=1024, tn=512, w_transposed=True)
    y_ssd = _ssd(h_zx, h_dt, bsz, ssd_conv_w[0], ssd_conv_b[0], ssd_dt_bias[0], ssd_a_log[0], ssd_d[0],
                 ssd_norm_g[0])
    y_gmlp = _gmlp(h_uv, gmlp_ln_g[0], gmlp_ln_b[0], gmlp_ws[0], gmlp_bs[0])
    mix = _project([y_ssd, y_gmlp], odd_w_out[0], [0, SSD_D_INNER // GMLP_WIDTH], 0, d, F32, tm=512, tn=512)
    xf, xp, logits = _residual_layer_norm(xf, mix, ln_mix_g[1], ln_mix_b[1], router_w)
    xf, xb = _moe_layer_norm(xf, xp, logits, 1, *moe_args, ln_ffn_g[1], ln_ffn_b[1])
    return xf.reshape(bsz, s, d)
```
